```python
import jax, jax.numpy as jnp
from jax import lax
import numpy as np

D_MODEL = 2048
BATCH = 4
SEQ = 2048
DEPTH = 1

MIX_WIDTH = D_MODEL
POOL_WIDTH = D_MODEL // 2
POOL_WINDOWS = (2, 4, 8, 16)
N_POOL_GROUPS = len(POOL_WINDOWS)
POOL_GROUP_DIM = POOL_WIDTH // N_POOL_GROUPS
HGRN_WIDTH = MIX_WIDTH - POOL_WIDTH
HGRN_EXPAND = 128
HGRN_HEADS = HGRN_WIDTH // HGRN_EXPAND
HGRN_HEAD_I = HGRN_WIDTH // HGRN_HEADS
HGRN_QK = HGRN_HEADS * HGRN_EXPAND
IN_WIDTH = POOL_WIDTH + 2 * HGRN_QK + 2 * HGRN_WIDTH
CHUNK = 64
D_FF = 5632
N_MOD = 9
EPS = 1e-6

kernel_name = "hybrid_pool_hgrn2_macaron_adaln"


def rmsnorm(x, w):
    xf = x.astype(jnp.float32)
    y = xf * lax.rsqrt(jnp.mean(xf * xf, axis=-1, keepdims=True) + EPS)
    return (y * w.astype(jnp.float32)).astype(x.dtype)


def modulate(h, shift, scale):
    return h * (1.0 + scale[:, None, :]) + shift[:, None, :]


def swiglu(h, w_gate, w_up, w_down):
    return (jax.nn.silu(h @ w_gate) * (h @ w_up)) @ w_down


def causal_multiscale_pool(u):
    B, S, G, Cg = u.shape
    uf = u.astype(jnp.float32)
    wmax = max(POOL_WINDOWS)
    cs = jnp.cumsum(uf, axis=1)
    csp = jnp.pad(cs, ((0, 0), (wmax, 0), (0, 0), (0, 0)))
    t = jnp.arange(S)
    means = []
    for g, w in enumerate(POOL_WINDOWS):
        win = csp[:, wmax:, g] - csp[:, wmax - w:wmax - w + S, g]
        cnt = jnp.minimum(t + 1, w).astype(jnp.float32)
        means.append(win / cnt[None, :, None])
    mean = jnp.stack(means, axis=2)
    return (mean - uf).astype(u.dtype)


def hgrn2_chunk_scan(q, k, v, logf):
    B, S, H, Dk = q.shape
    Dv = v.shape[-1]
    N = S // CHUNK

    def to_chunks(a):
        return a.reshape(B, N, CHUNK, H, a.shape[-1]).transpose(1, 0, 3, 2, 4)

    qc, kc, vc, gc = to_chunks(q), to_chunks(k), to_chunks(v), to_chunks(logf)
    causal = jnp.tril(jnp.ones((CHUNK, CHUNK), dtype=bool))

    def step(s_prev, inp):
        qb, kb, vb, gb = inp
        b = jnp.cumsum(gb, axis=2)
        diff = b[:, :, :, None, :] - b[:, :, None, :, :]
        decay = jnp.exp(jnp.where(causal[:, :, None], diff, -jnp.inf))
        attn = jnp.einsum('bhtd,bhsd,bhtsd->bhts', qb, kb, decay)
        o = (jnp.einsum('bhts,bhsv->bhtv', attn, vb)
             + jnp.einsum('bhtd,bhdv->bhtv', qb * jnp.exp(b), s_prev))
        b_last = b[:, :, -1:, :]
        s_new = (jnp.exp(b_last[:, :, 0, :])[..., None] * s_prev
                 + jnp.einsum('bhsd,bhsv->bhdv', kb * jnp.exp(b_last - b), vb))
        return s_new, o

    s0 = jnp.zeros((B, H, Dk, Dv), jnp.float32)
    _, o = lax.scan(step, s0, (qc, kc, vc, gc))
    return o.transpose(1, 0, 3, 2, 4).reshape(B, S, H, Dv)


def setup_inputs(seed: int = 0) -> dict:
    key = jax.random.key(seed)
    ks = jax.random.split(key, 20)
    L, D = DEPTH, D_MODEL
    nrm = jax.random.normal
    return {
        "x": nrm(ks[0], (BATCH, SEQ, D), jnp.float32),
        "c": nrm(ks[1], (BATCH, D), jnp.float32),
        "w_ada": nrm(ks[2], (L, D, N_MOD * D), jnp.float32) * (0.5 * D ** -0.5),
        "b_ada": nrm(ks[3], (L, N_MOD * D), jnp.float32) * 0.01,
        "norm1_w": 1.0 + 0.02 * nrm(ks[4], (L, D), jnp.float32),
        "ffn1_gate": nrm(ks[5], (L, D, D_FF), jnp.float32) * D ** -0.5,
        "ffn1_up": nrm(ks[6], (L, D, D_FF), jnp.float32) * D ** -0.5,
        "ffn1_down": nrm(ks[7], (L, D_FF, D), jnp.float32) * D_FF ** -0.5,
        "norm2_w": 1.0 + 0.02 * nrm(ks[8], (L, D), jnp.float32),
        "w_in": nrm(ks[9], (L, D, IN_WIDTH), jnp.float32) * D ** -0.5,
        "pool_w": nrm(ks[10], (L, N_POOL_GROUPS, POOL_GROUP_DIM, POOL_GROUP_DIM), jnp.float32) * POOL_GROUP_DIM ** -0.5,
        "pool_scale": 1.0 + 0.1 * nrm(ks[11], (L, POOL_WIDTH), jnp.float32),
        "lb_logits": 0.1 * nrm(ks[12], (L + 1, HGRN_QK), jnp.float32),
        "gnorm_w": 1.0 + 0.02 * nrm(ks[13], (L, HGRN_HEAD_I), jnp.float32),
        "w_out": nrm(ks[14], (L, MIX_WIDTH, D), jnp.float32) * MIX_WIDTH ** -0.5,
        "norm3_w": 1.0 + 0.02 * nrm(ks[15], (L, D), jnp.float32),
        "ffn2_gate": nrm(ks[16], (L, D, D_FF), jnp.float32) * D ** -0.5,
        "ffn2_up": nrm(ks[17], (L, D, D_FF), jnp.float32) * D ** -0.5,
        "ffn2_down": nrm(ks[18], (L, D_FF, D), jnp.float32) * D_FF ** -0.5,
        "final_norm_w": 1.0 + 0.02 * nrm(ks[19], (D,), jnp.float32),
    }


def reference(x, c, w_ada, b_ada, norm1_w, ffn1_gate, ffn1_up, ffn1_down, norm2_w, w_in,
              pool_w, pool_scale, lb_logits, gnorm_w, w_out, norm3_w, ffn2_gate, ffn2_up,
              ffn2_down, final_norm_w):
    B, S, _ = x.shape
    p = jax.nn.softmax(lb_logits.astype(jnp.float32), axis=0)
    lb_all = jnp.cumsum(p, axis=0) - p[0]
    c_act = jax.nn.silu(c)
    splits = [POOL_WIDTH, POOL_WIDTH + HGRN_QK, POOL_WIDTH + 2 * HGRN_QK,
              POOL_WIDTH + 2 * HGRN_QK + HGRN_WIDTH]

    for l in range(DEPTH):
        mod = c_act @ w_ada[l] + b_ada[l]
        sh1, sc1, g1, sh2, sc2, g2, sh3, sc3, g3 = jnp.split(mod, N_MOD, axis=-1)

        h = modulate(rmsnorm(x, norm1_w[l]), sh1, sc1)
        x = x + 0.5 * g1[:, None, :] * swiglu(h, ffn1_gate[l], ffn1_up[l], ffn1_down[l])

        h = modulate(rmsnorm(x, norm2_w[l]), sh2, sc2)
        z = h @ w_in[l]
        zp, zq, zf, zi, zg = jnp.split(z, splits, axis=-1)

        u = zp.reshape(B, S, N_POOL_GROUPS, POOL_GROUP_DIM)
        pooled = causal_multiscale_pool(u)
        y_pool = (jnp.einsum('bsgc,gcd->bsgd', pooled, pool_w[l]).reshape(B, S, POOL_WIDTH)
                  * pool_scale[l])

        lb = lb_all[l + 1]
        forget = lb + (1.0 - lb) * jax.nn.sigmoid(zf.astype(jnp.float32))
        qh = jax.nn.silu(zq.astype(jnp.float32)).reshape(B, S, HGRN_HEADS, HGRN_EXPAND)
        kh = (1.0 - forget).reshape(B, S, HGRN_HEADS, HGRN_EXPAND)
        gh = jnp.log(forget).reshape(B, S, HGRN_HEADS, HGRN_EXPAND)
        vh = zi.astype(jnp.float32).reshape(B, S, HGRN_HEADS, HGRN_HEAD_I)
        o = hgrn2_chunk_scan(qh, kh, vh, gh)
        o = rmsnorm(o, gnorm_w[l]) * jax.nn.silu(
            zg.astype(jnp.float32).reshape(B, S, HGRN_HEADS, HGRN_HEAD_I))
        y_hgrn = o.reshape(B, S, HGRN_WIDTH).astype(x.dtype)

        mix = jnp.concatenate([y_pool.astype(x.dtype), y_hgrn], axis=-1) @ w_out[l]
        x = x + g2[:, None, :] * mix

        h = modulate(rmsnorm(x, norm3_w[l]), sh3, sc3)
        x = x + 0.5 * g3[:, None, :] * swiglu(h, ffn2_gate[l], ffn2_up[l], ffn2_down[l])

    return rmsnorm(x, final_norm_w)
```

```python
import functools

import jax
import jax.numpy as jnp
from jax import lax
from jax.experimental import pallas as pl
from jax.experimental.pallas import tpu as pltpu

F32 = jnp.float32
BF16 = jnp.bfloat16

D_MODEL = 2048
BATCH = 4
SEQ = 2048
POOL_WIDTH = 1024
POOL_WINDOWS = (2, 4, 8, 16)
POOL_GROUP_DIM = 256
HGRN_WIDTH = 1024
HGRN_HEADS = 8
HEAD_DIM = 128
IN_WIDTH = 5120
CHUNK = 64
D_FF = 5632
N_MOD = 9
EPS = 1e-6

VMEM_LIMIT = 56 * 1024 * 1024


def _silu(v):
    return v * jax.nn.sigmoid(v)


def _norm_modulate(x, nw, sh, sc):
    ms = jnp.mean(x * x, axis=-1, keepdims=True)
    y = x * lax.rsqrt(ms + EPS) * nw
    return y * (1.0 + sc) + sh


def _ada_kernel(c_ref, w_ref, b_ref, o_ref):
    ca = _silu(c_ref[...])
    o_ref[...] = jnp.dot(ca, w_ref[...], preferred_element_type=F32,
                         precision=lax.Precision.HIGHEST) + b_ref[...]


def _ada(c_pad, w_ada, b_ada):
    n = w_ada.shape[1]
    tn = 1024
    return pl.pallas_call(
        _ada_kernel,
        grid=(n // tn,),
        in_specs=[
            pl.BlockSpec((8, D_MODEL), lambda j: (0, 0)),
            pl.BlockSpec((D_MODEL, tn), lambda j: (0, j)),
            pl.BlockSpec((1, tn), lambda j: (0, j)),
        ],
        out_specs=pl.BlockSpec((8, tn), lambda j: (0, j)),
        out_shape=jax.ShapeDtypeStruct((8, n), F32),
        compiler_params=pltpu.CompilerParams(
            dimension_semantics=("arbitrary",), vmem_limit_bytes=VMEM_LIMIT),
        name="ada_mod",
    )(c_pad, w_ada, b_ada)


def _ffn_kernel(x_ref, nw_ref, sh_ref, sc_ref, gt_ref, wg_ref, wu_ref, wd_ref, fw_ref,
                o_ref, h_scr, acc_scr, *, nj, final_norm):
    j = pl.program_id(1)

    @pl.when(j == 0)
    def _():
        h = _norm_modulate(x_ref[...], nw_ref[...], sh_ref[0], sc_ref[0])
        h_scr[...] = h.astype(BF16)
        acc_scr[...] = jnp.zeros_like(acc_scr)

    h = h_scr[...]
    g = jnp.dot(h, wg_ref[...], preferred_element_type=F32)
    u = jnp.dot(h, wu_ref[...], preferred_element_type=F32)
    a = (_silu(g) * u).astype(BF16)
    acc_scr[...] += jnp.dot(a, wd_ref[...], preferred_element_type=F32)

    @pl.when(j == nj - 1)
    def _():
        out = x_ref[...] + 0.5 * gt_ref[0] * acc_scr[...]
        if final_norm:
            ms = jnp.mean(out * out, axis=-1, keepdims=True)
            out = out * lax.rsqrt(ms + EPS) * fw_ref[...]
        o_ref[...] = out


def _ffn(x2d, nw, mod3, mod_base, wg, wu, wd, fw, *, final_norm):
    m = x2d.shape[0]
    tm, tf = 512, 512
    ni, nj = m // tm, D_FF // tf
    tiles_per_batch = SEQ // tm

    def mod_map(k):
        return lambda i, j: ((i // tiles_per_batch) * N_MOD + mod_base + k, 0, 0)

    kern = functools.partial(_ffn_kernel, nj=nj, final_norm=final_norm)
    return pl.pallas_call(
        kern,
        grid=(ni, nj),
        in_specs=[
            pl.BlockSpec((tm, D_MODEL), lambda i, j: (i, 0)),
            pl.BlockSpec((1, D_MODEL), lambda i, j: (0, 0)),
            pl.BlockSpec((1, 1, D_MODEL), mod_map(0)),
            pl.BlockSpec((1, 1, D_MODEL), mod_map(1)),
            pl.BlockSpec((1, 1, D_MODEL), mod_map(2)),
            pl.BlockSpec((D_MODEL, tf), lambda i, j: (0, j)),
            pl.BlockSpec((D_MODEL, tf), lambda i, j: (0, j)),
            pl.BlockSpec((tf, D_MODEL), lambda i, j: (j, 0)),
            pl.BlockSpec((1, D_MODEL), lambda i, j: (0, 0)),
        ],
        out_specs=pl.BlockSpec((tm, D_MODEL), lambda i, j: (i, 0)),
        out_shape=jax.ShapeDtypeStruct((m, D_MODEL), F32),
        scratch_shapes=[pltpu.VMEM((tm, D_MODEL), BF16), pltpu.VMEM((tm, D_MODEL), F32)],
        compiler_params=pltpu.CompilerParams(
            dimension_semantics=("arbitrary", "arbitrary"), vmem_limit_bytes=VMEM_LIMIT),
        name="ffn_final" if final_norm else "ffn",
    )(x2d, nw, mod3, mod3, mod3, wg, wu, wd, fw)


def _win_kernel(x_ref, nw_ref, sh_ref, sc_ref, w_ref, z_ref, h_scr):
    @pl.when(pl.program_id(1) == 0)
    def _():
        h = _norm_modulate(x_ref[...], nw_ref[...], sh_ref[0], sc_ref[0])
        h_scr[...] = h.astype(BF16)

    z_ref[...] = jnp.dot(h_scr[...], w_ref[...], preferred_element_type=F32)


def _win(x2d, nw, mod3, w_in):
    m = x2d.shape[0]
    tm, tn = 512, 1024
    tiles_per_batch = SEQ // tm

    def mod_map(k):
        return lambda i, j: ((i // tiles_per_batch) * N_MOD + 3 + k, 0, 0)

    return pl.pallas_call(
        _win_kernel,
        grid=(m // tm, IN_WIDTH // tn),
        in_specs=[
            pl.BlockSpec((tm, D_MODEL), lambda i, j: (i, 0)),
            pl.BlockSpec((1, D_MODEL), lambda i, j: (0, 0)),
            pl.BlockSpec((1, 1, D_MODEL), mod_map(0)),
            pl.BlockSpec((1, 1, D_MODEL), mod_map(1)),
            pl.BlockSpec((D_MODEL, tn), lambda i, j: (0, j)),
        ],
        out_specs=pl.BlockSpec((tm, tn), lambda i, j: (i, j)),
        out_shape=jax.ShapeDtypeStruct((m, IN_WIDTH), F32),
        scratch_shapes=[pltpu.VMEM((tm, D_MODEL), BF16)],
        compiler_params=pltpu.CompilerParams(
            dimension_semantics=("arbitrary", "arbitrary"), vmem_limit_bytes=VMEM_LIMIT),
        name="mix_in",
    )(x2d, nw, mod3, mod3, w_in)


def _pool_kernel(u_ref, pw_ref, ps_ref, o_ref):
    g = pl.program_id(1)
    u = u_ref[0]
    t = lax.broadcasted_iota(jnp.int32, (SEQ, 1), 0)

    def shifted(v, k):
        return jnp.where(t >= k, pltpu.roll(v, k, axis=0), 0.0)

    s2 = u + shifted(u, 1)
    s4 = s2 + shifted(s2, 2)
    s8 = s4 + shifted(s4, 4)
    s16 = s8 + shifted(s8, 8)
    win = jnp.where(g == 0, s2, jnp.where(g == 1, s4, jnp.where(g == 2, s8, s16)))
    cnt = jnp.minimum(t + 1, 2 << g).astype(F32)
    pooled = win / cnt - u
    y = jnp.dot(pooled.astype(BF16), pw_ref[0], preferred_element_type=F32) * ps_ref[...]
    o_ref[0] = y.astype(BF16)


def _pool(z3, pool_w, pool_scale):
    assert POOL_WINDOWS == (2, 4, 8, 16)
    return pl.pallas_call(
        _pool_kernel,
        grid=(BATCH, len(POOL_WINDOWS)),
        in_specs=[
            pl.BlockSpec((1, SEQ, POOL_GROUP_DIM), lambda b, g: (b, 0, g)),
            pl.BlockSpec((1, POOL_GROUP_DIM, POOL_GROUP_DIM), lambda b, g: (g, 0, 0)),
            pl.BlockSpec((1, POOL_GROUP_DIM), lambda b, g: (0, g)),
        ],
        out_specs=pl.BlockSpec((1, SEQ, POOL_GROUP_DIM), lambda b, g: (b, 0, g)),
        out_shape=jax.ShapeDtypeStruct((BATCH, SEQ, POOL_WIDTH), BF16),
        compiler_params=pltpu.CompilerParams(
            dimension_semantics=("arbitrary", "arbitrary"), vmem_limit_bytes=VMEM_LIMIT),
        name="pool",
    )(z3, pool_w, pool_scale)


def _dot_nt(a, b):
    return lax.dot_general(a, b, (((1,), (1,)), ((), ())), preferred_element_type=F32)


def _dot_tn(a, b):
    return lax.dot_general(a, b, (((0,), (0,)), ((), ())), preferred_element_type=F32)


def _level_ref(b, hs, row):
    c = b.shape[0]
    if hs >= 8:
        parts = [jnp.broadcast_to(b[p + hs - 1:p + hs, :], (2 * hs, b.shape[1]))
                 for p in range(0, c, 2 * hs)]
        return parts[0] if len(parts) == 1 else jnp.concatenate(parts, axis=0)
    off = row % hs
    second = (row // hs) % 2 == 1
    ref = b
    for d in range(1, hs + 1):
        ref = jnp.where(second & (off + 1 == d), pltpu.roll(b, d, axis=0), ref)
    for d in range(1, hs):
        ref = jnp.where((~second) & (hs - 1 - off == d), pltpu.roll(b, c - d, axis=0), ref)
    return ref


def _scan_kernel(zq_ref, zf_ref, zi_ref, zg_ref, lbl_ref, gw_ref, o_ref, st_scr):
    lbl = lbl_ref[...]
    pe = jnp.exp(lbl - jnp.max(lbl, axis=0, keepdims=True))
    p = pe / jnp.sum(pe, axis=0, keepdims=True)
    lb = (p[0:1, :] + p[1:2, :]) - p[0:1, :]
    gw = gw_ref[...]
    level_sizes = (32, 16, 8, 4, 2, 1)
    st_scr[...] = jnp.zeros_like(st_scr)

    def body(c, carry):
        row = lax.broadcasted_iota(jnp.int32, (CHUNK, 1), 0)
        ti = lax.broadcasted_iota(jnp.int32, (CHUNK, CHUNK), 0)
        si = lax.broadcasted_iota(jnp.int32, (CHUNK, CHUNK), 1)
        masks = [((ti // (2 * hs)) == (si // (2 * hs))) & ((ti // hs) > (si // hs))
                 for hs in level_sizes]
        diag = ti == si
        r0 = pl.multiple_of(c * CHUNK, CHUNK)
        zq = zq_ref[0, pl.ds(r0, CHUNK), :]
        zf = zf_ref[0, pl.ds(r0, CHUNK), :]
        v = zi_ref[0, pl.ds(r0, CHUNK), :]
        zg = zg_ref[0, pl.ds(r0, CHUNK), :]
        forget = lb + (1.0 - lb) * jax.nn.sigmoid(zf)
        q = _silu(zq)
        k = 1.0 - forget
        b = jnp.log(forget)
        for s in (1, 2, 4, 8, 16, 32):
            b = b + jnp.where(row >= s, pltpu.roll(b, s, axis=0), 0.0)
        vb = v.astype(BF16)

        attn = jnp.where(diag, _dot_nt(q.astype(BF16), k.astype(BF16)), 0.0)
        for hs, mask in zip(level_sizes, masks):
            e = jnp.exp(-jnp.abs(b - _level_ref(b, hs, row)))
            lvl = _dot_nt((q * e).astype(BF16), (k * e).astype(BF16))
            attn = jnp.where(mask, lvl, attn)

        st_prev = st_scr[...]
        b_last = b[CHUNK - 1:CHUNK, :]
        o = (jnp.dot(attn.astype(BF16), vb, preferred_element_type=F32)
             + _dot_nt((q * jnp.exp(b)).astype(BF16), st_prev.astype(BF16)))
        kd = (k * jnp.exp(b_last - b)).astype(BF16)
        st_scr[...] = jnp.exp(b_last) * st_prev + _dot_tn(vb, kd)

        ms = jnp.mean(o * o, axis=-1, keepdims=True)
        y = o * lax.rsqrt(ms + EPS) * gw * _silu(zg)
        o_ref[0, pl.ds(r0, CHUNK), :] = y.astype(BF16)
        return carry

    lax.fori_loop(0, SEQ // CHUNK, body, 0)


def _scan(z3, lb_logits, gnorm_w):
    qb, fb, ib, gb = (POOL_WIDTH // HEAD_DIM + k * HGRN_HEADS for k in range(4))

    def col_map(base):
        return lambda b, h: (b, 0, base + h)

    blk = (1, SEQ, HEAD_DIM)
    return pl.pallas_call(
        _scan_kernel,
        grid=(BATCH, HGRN_HEADS),
        in_specs=[
            pl.BlockSpec(blk, col_map(qb)),
            pl.BlockSpec(blk, col_map(fb)),
            pl.BlockSpec(blk, col_map(ib)),
            pl.BlockSpec(blk, col_map(gb)),
            pl.BlockSpec((2, HEAD_DIM), lambda b, h: (0, h)),
            pl.BlockSpec((1, HEAD_DIM), lambda b, h: (0, 0)),
        ],
        out_specs=pl.BlockSpec(blk, lambda b, h: (b, 0, h)),
        out_shape=jax.ShapeDtypeStruct((BATCH, SEQ, HGRN_WIDTH), BF16),
        scratch_shapes=[pltpu.VMEM((HEAD_DIM, HEAD_DIM), F32)],
        compiler_params=pltpu.CompilerParams(
            dimension_semantics=("arbitrary", "arbitrary"), vmem_limit_bytes=VMEM_LIMIT),
        name="hgrn_scan",
    )(z3, z3, z3, z3, lb_logits, gnorm_w)


def _wout_kernel(x_ref, yp_ref, yh_ref, wp_ref, wh_ref, gt_ref, o_ref):
    mix = (jnp.dot(yp_ref[...], wp_ref[...], preferred_element_type=F32)
           + jnp.dot(yh_ref[...], wh_ref[...], preferred_element_type=F32))
    o_ref[...] = x_ref[...] + gt_ref[0] * mix


def _wout(x2d, yp, yh, w_out, mod3):
    m = x2d.shape[0]
    tm, tn = 512, 1024
    tiles_per_batch = SEQ // tm
    return pl.pallas_call(
        _wout_kernel,
        grid=(m // tm, D_MODEL // tn),
        in_specs=[
            pl.BlockSpec((tm, tn), lambda i, j: (i, j)),
            pl.BlockSpec((tm, POOL_WIDTH), lambda i, j: (i, 0)),
            pl.BlockSpec((tm, HGRN_WIDTH), lambda i, j: (i, 0)),
            pl.BlockSpec((POOL_WIDTH, tn), lambda i, j: (0, j)),
            pl.BlockSpec((HGRN_WIDTH, tn), lambda i, j: (1, j)),
            pl.BlockSpec((1, 1, tn), lambda i, j: ((i // tiles_per_batch) * N_MOD + 5, 0, j)),
        ],
        out_specs=pl.BlockSpec((tm, tn), lambda i, j: (i, j)),
        out_shape=jax.ShapeDtypeStruct((m, D_MODEL), F32),
        compiler_params=pltpu.CompilerParams(
            dimension_semantics=("arbitrary", "arbitrary"), vmem_limit_bytes=VMEM_LIMIT),
        name="mix_out",
    )(x2d, yp, yh, w_out, w_out, mod3)


def kernel(x, c, w_ada, b_ada, norm1_w, ffn1_gate, ffn1_up, ffn1_down, norm2_w, w_in, pool_w,
           pool_scale, lb_logits, gnorm_w, w_out, norm3_w, ffn2_gate, ffn2_up, ffn2_down,
           final_norm_w):
    B, S, D = x.shape
    assert (B, S, D) == (BATCH, SEQ, D_MODEL) and w_ada.shape[0] == 1
    assert lb_logits.shape == (2, HGRN_WIDTH)

    c_pad = jnp.pad(c, ((0, 8 - B), (0, 0)))
    mod = _ada(c_pad, w_ada[0], b_ada)[:B]
    mod3 = mod.reshape(B * N_MOD, 1, D)

    x2d = x.reshape(B * S, D)
    fw = final_norm_w.reshape(1, D)

    x1 = _ffn(x2d, norm1_w, mod3, 0, ffn1_gate[0].astype(BF16), ffn1_up[0].astype(BF16),
              ffn1_down[0].astype(BF16), fw, final_norm=False)

    z = _win(x1, norm2_w, mod3, w_in[0].astype(BF16))
    z3 = z.reshape(B, S, IN_WIDTH)
    y_pool = _pool(z3, pool_w[0].astype(BF16), pool_scale)
    y_hgrn = _scan(z3, lb_logits, gnorm_w)
    x2 = _wout(x1, y_pool.reshape(B * S, POOL_WIDTH), y_hgrn.reshape(B * S, HGRN_WIDTH),
               w_out[0].astype(BF16), mod3)

    out = _ffn(x2, norm3_w, mod3, 6, ffn2_gate[0].astype(BF16), ffn2_up[0].astype(BF16),
               ffn2_down[0].astype(BF16), fw, final_norm=True)
    return out.reshape(B, S, D)
```

```python
import functools

import jax
import jax.numpy as jnp
from jax import lax
from jax.experimental import pallas as pl
from jax.experimental.pallas import tpu as pltpu

F32 = jnp.float32
BF16 = jnp.bfloat16

D_MODEL = 2048
BATCH = 4
SEQ = 2048
POOL_WIDTH = 1024
POOL_WINDOWS = (2, 4, 8, 16)
POOL_GROUP_DIM = 256
HGRN_WIDTH = 1024
HGRN_HEADS = 8
HEAD_DIM = 128
IN_WIDTH = 5120
CHUNK = 64
SUBLANES = 8
D_FF = 5632
N_MOD = 9
EPS = 1e-6

VMEM_LIMIT = 56 * 1024 * 1024


def _silu(v):
    return v * jax.nn.sigmoid(v)


def _norm_modulate(x, nw, sh, sc):
    ms = jnp.mean(x * x, axis=-1, keepdims=True)
    y = x * lax.rsqrt(ms + EPS) * nw
    return y * (1.0 + sc) + sh


def _ada_kernel(c_ref, w_ref, b_ref, o_ref):
    ca = _silu(c_ref[...])
    o_ref[...] = jnp.dot(ca, w_ref[...], preferred_element_type=F32,
                         precision=lax.Precision.HIGHEST) + b_ref[...]


def _ada(c_pad, w_ada, b_ada):
    n = w_ada.shape[1]
    tn = 1024
    return pl.pallas_call(
        _ada_kernel,
        grid=(n // tn,),
        in_specs=[
            pl.BlockSpec((8, D_MODEL), lambda j: (0, 0)),
            pl.BlockSpec((D_MODEL, tn), lambda j: (0, j)),
            pl.BlockSpec((1, tn), lambda j: (0, j)),
        ],
        out_specs=pl.BlockSpec((8, tn), lambda j: (0, j)),
        out_shape=jax.ShapeDtypeStruct((8, n), F32),
        compiler_params=pltpu.CompilerParams(
            dimension_semantics=("arbitrary",), vmem_limit_bytes=VMEM_LIMIT),
        name="ada_mod",
    )(c_pad, w_ada, b_ada)


def _ffn_kernel(x_ref, nw_ref, sh_ref, sc_ref, gt_ref, wg_ref, wu_ref, wd_ref, fw_ref,
                o_ref, h_scr, acc_scr, *, nj, final_norm):
    j = pl.program_id(1)

    @pl.when(j == 0)
    def _():
        h = _norm_modulate(x_ref[...], nw_ref[...], sh_ref[0], sc_ref[0])
        h_scr[...] = h.astype(BF16)
        acc_scr[...] = jnp.zeros_like(acc_scr)

    h = h_scr[...]
    g = jnp.dot(h, wg_ref[...], preferred_element_type=F32)
    u = jnp.dot(h, wu_ref[...], preferred_element_type=F32)
    a = (_silu(g) * u).astype(BF16)
    acc_scr[...] += jnp.dot(a, wd_ref[...], preferred_element_type=F32)

    @pl.when(j == nj - 1)
    def _():
        out = x_ref[...] + 0.5 * gt_ref[0] * acc_scr[...]
        if final_norm:
            ms = jnp.mean(out * out, axis=-1, keepdims=True)
            out = out * lax.rsqrt(ms + EPS) * fw_ref[...]
        o_ref[...] = out


def _ffn(x2d, nw, mod3, mod_base, wg, wu, wd, fw, *, final_norm):
    m = x2d.shape[0]
    tm, tf = 512, 512
    ni, nj = m // tm, D_FF // tf
    tiles_per_batch = SEQ // tm

    def mod_map(k):
        return lambda i, j: ((i // tiles_per_batch) * N_MOD + mod_base + k, 0, 0)

    kern = functools.partial(_ffn_kernel, nj=nj, final_norm=final_norm)
    return pl.pallas_call(
        kern,
        grid=(ni, nj),
        in_specs=[
            pl.BlockSpec((tm, D_MODEL), lambda i, j: (i, 0)),
            pl.BlockSpec((1, D_MODEL), lambda i, j: (0, 0)),
            pl.BlockSpec((1, 1, D_MODEL), mod_map(0)),
            pl.BlockSpec((1, 1, D_MODEL), mod_map(1)),
            pl.BlockSpec((1, 1, D_MODEL), mod_map(2)),
            pl.BlockSpec((D_MODEL, tf), lambda i, j: (0, j)),
            pl.BlockSpec((D_MODEL, tf), lambda i, j: (0, j)),
            pl.BlockSpec((tf, D_MODEL), lambda i, j: (j, 0)),
            pl.BlockSpec((1, D_MODEL), lambda i, j: (0, 0)),
        ],
        out_specs=pl.BlockSpec((tm, D_MODEL), lambda i, j: (i, 0)),
        out_shape=jax.ShapeDtypeStruct((m, D_MODEL), F32),
        scratch_shapes=[pltpu.VMEM((tm, D_MODEL), BF16), pltpu.VMEM((tm, D_MODEL), F32)],
        compiler_params=pltpu.CompilerParams(
            dimension_semantics=("arbitrary", "arbitrary"), vmem_limit_bytes=VMEM_LIMIT),
        name="ffn_final" if final_norm else "ffn",
    )(x2d, nw, mod3, mod3, mod3, wg, wu, wd, fw)


def _win_kernel(x_ref, nw_ref, sh_ref, sc_ref, w_ref, lbl_ref,
                u_ref, q_ref, k_ref, gl_ref, v_ref, gt_ref, h_scr):
    j = pl.program_id(1)

    @pl.when(j == 0)
    def _():
        h = _norm_modulate(x_ref[...], nw_ref[...], sh_ref[0], sc_ref[0])
        h_scr[...] = h.astype(BF16)

    z = jnp.dot(h_scr[...], w_ref[...], preferred_element_type=F32)

    @pl.when(j == 0)
    def _():
        u_ref[...] = z

    @pl.when(j == 1)
    def _():
        q_ref[...] = _silu(z).astype(BF16)

    @pl.when(j == 2)
    def _():
        lbl = lbl_ref[...]
        pe = jnp.exp(lbl - jnp.max(lbl, axis=0, keepdims=True))
        p = pe / jnp.sum(pe, axis=0, keepdims=True)
        lb = (p[0:1, :] + p[1:2, :]) - p[0:1, :]
        forget = lb + (1.0 - lb) * jax.nn.sigmoid(z)
        k_ref[...] = (1.0 - forget).astype(BF16)
        gl_ref[...] = jnp.log(forget)

    @pl.when(j == 3)
    def _():
        v_ref[...] = z.astype(BF16)

    @pl.when(j == 4)
    def _():
        gt_ref[...] = _silu(z).astype(BF16)


def _win(x2d, nw, mod3, w_in, lb_logits):
    m = x2d.shape[0]
    tm, tn = 512, 1024
    assert IN_WIDTH == 5 * tn and POOL_WIDTH == tn and HGRN_WIDTH == tn
    tiles_per_batch = SEQ // tm

    def mod_map(k):
        return lambda i, j: ((i // tiles_per_batch) * N_MOD + 3 + k, 0, 0)

    seg = pl.BlockSpec((tm, tn), lambda i, j: (i, 0))
    return pl.pallas_call(
        _win_kernel,
        grid=(m // tm, IN_WIDTH // tn),
        in_specs=[
            pl.BlockSpec((tm, D_MODEL), lambda i, j: (i, 0)),
            pl.BlockSpec((1, D_MODEL), lambda i, j: (0, 0)),
            pl.BlockSpec((1, 1, D_MODEL), mod_map(0)),
            pl.BlockSpec((1, 1, D_MODEL), mod_map(1)),
            pl.BlockSpec((D_MODEL, tn), lambda i, j: (0, j)),
            pl.BlockSpec((2, tn), lambda i, j: (0, 0)),
        ],
        out_specs=[seg] * 6,
        out_shape=[
            jax.ShapeDtypeStruct((m, tn), F32),
            jax.ShapeDtypeStruct((m, tn), BF16),
            jax.ShapeDtypeStruct((m, tn), BF16),
            jax.ShapeDtypeStruct((m, tn), F32),
            jax.ShapeDtypeStruct((m, tn), BF16),
            jax.ShapeDtypeStruct((m, tn), BF16),
        ],
        scratch_shapes=[pltpu.VMEM((tm, D_MODEL), BF16)],
        compiler_params=pltpu.CompilerParams(
            dimension_semantics=("arbitrary", "arbitrary"), vmem_limit_bytes=VMEM_LIMIT),
        name="mix_in",
    )(x2d, nw, mod3, mod3, w_in, lb_logits)


def _pool_kernel(u_ref, pw_ref, ps_ref, o_ref):
    g = pl.program_id(1)
    u = u_ref[0]
    t = lax.broadcasted_iota(jnp.int32, (SEQ, 1), 0)

    def shifted(v, k):
        return jnp.where(t >= k, pltpu.roll(v, k, axis=0), 0.0)

    s2 = u + shifted(u, 1)
    s4 = s2 + shifted(s2, 2)
    s8 = s4 + shifted(s4, 4)
    s16 = s8 + shifted(s8, 8)
    win = jnp.where(g == 0, s2, jnp.where(g == 1, s4, jnp.where(g == 2, s8, s16)))
    cnt = jnp.minimum(t + 1, 2 << g).astype(F32)
    pooled = win / cnt - u
    y = jnp.dot(pooled.astype(BF16), pw_ref[0], preferred_element_type=F32) * ps_ref[...]
    o_ref[0] = y.astype(BF16)


def _pool(u3, pool_w, pool_scale):
    assert POOL_WINDOWS == (2, 4, 8, 16)
    return pl.pallas_call(
        _pool_kernel,
        grid=(BATCH, len(POOL_WINDOWS)),
        in_specs=[
            pl.BlockSpec((1, SEQ, POOL_GROUP_DIM), lambda b, g: (b, 0, g)),
            pl.BlockSpec((1, POOL_GROUP_DIM, POOL_GROUP_DIM), lambda b, g: (g, 0, 0)),
            pl.BlockSpec((1, POOL_GROUP_DIM), lambda b, g: (0, g)),
        ],
        out_specs=pl.BlockSpec((1, SEQ, POOL_GROUP_DIM), lambda b, g: (b, 0, g)),
        out_shape=jax.ShapeDtypeStruct((BATCH, SEQ, POOL_WIDTH), BF16),
        compiler_params=pltpu.CompilerParams(
            dimension_semantics=("arbitrary", "arbitrary"), vmem_limit_bytes=VMEM_LIMIT),
        name="pool",
    )(u3, pool_w, pool_scale)


def _dot_nt(a, b):
    return lax.dot_general(a, b, (((1,), (1,)), ((), ())), preferred_element_type=F32)


def _dot_tn(a, b):
    return lax.dot_general(a, b, (((0,), (0,)), ((), ())), preferred_element_type=F32)


LEVEL_SIZES = (32, 16, 8, 4, 2, 1)


def _row_bcast(v, r):
    return jnp.broadcast_to(v[r:r + 1, :], v.shape)


def _level_exponents(g):
    n = len(g)
    r8 = lax.broadcasted_iota(jnp.int32, (SUBLANES, 1), 0)
    p = list(g)
    for s in (1, 2, 4):
        p = [pi + jnp.where(r8 >= s, pltpu.roll(pi, s, axis=0), 0.0) for pi in p]
    off = [None, _row_bcast(p[0], SUBLANES - 1)]
    for i in range(1, n):
        off.append(off[i] + _row_bcast(p[i], SUBLANES - 1))
    b = [p[0]] + [p[i] + off[i] for i in range(1, n)]

    exps = []
    for hs in LEVEL_SIZES:
        if hs >= SUBLANES:
            w = hs // SUBLANES
            lvl = []
            for i in range(n):
                ref = off[(i // (2 * w)) * 2 * w + w]
                lvl.append(b[i] - ref if (i // w) % 2 == 1 else ref - b[i])
        elif hs == 1:
            lvl = [jnp.where(r8 % 2 == 1, gi, 0.0) for gi in g]
        else:
            lvl = []
            for pi in p:
                ref = _row_bcast(pi, hs - 1)
                for blk in range(1, SUBLANES // (2 * hs)):
                    ref = jnp.where(r8 >= blk * 2 * hs, _row_bcast(pi, blk * 2 * hs + hs - 1), ref)
                lvl.append(-jnp.abs(pi - ref))
        exps.append(lvl)
    return exps, b, off[n]


def _scan_kernel(q_ref, k_ref, gl_ref, v_ref, gt_ref, gw_ref, o_ref, st_scr):
    gw = gw_ref[...]
    nt = CHUNK // SUBLANES
    ti = lax.broadcasted_iota(jnp.int32, (CHUNK, CHUNK), 0)
    si = lax.broadcasted_iota(jnp.int32, (CHUNK, CHUNK), 1)
    owner = jnp.where(ti == si, len(LEVEL_SIZES), -1)
    for idx, hs in enumerate(LEVEL_SIZES):
        owner = jnp.where(((ti // (2 * hs)) == (si // (2 * hs))) & ((ti // hs) > (si // hs)),
                          idx, owner)
    st_scr[...] = jnp.zeros_like(st_scr)

    def tiles(a):
        return [a[i * SUBLANES:(i + 1) * SUBLANES, :] for i in range(nt)]

    def scaled(a_tiles, e_tiles):
        return jnp.concatenate([a * jnp.exp(e) for a, e in zip(a_tiles, e_tiles)],
                               axis=0).astype(BF16)

    def body(c, carry):
        r0 = pl.multiple_of(c * CHUNK, CHUNK)
        qb = q_ref[0, pl.ds(r0, CHUNK), :]
        kb = k_ref[0, pl.ds(r0, CHUNK), :]
        vb = v_ref[0, pl.ds(r0, CHUNK), :]
        q = tiles(qb.astype(F32))
        k = tiles(kb.astype(F32))
        exps, b, b_last = _level_exponents(tiles(gl_ref[0, pl.ds(r0, CHUNK), :]))

        attn = jnp.where(owner == len(LEVEL_SIZES), _dot_nt(qb, kb), 0.0)
        for idx, e in enumerate(exps):
            attn = jnp.where(owner == idx, _dot_nt(scaled(q, e), scaled(k, e)), attn)

        st_prev = st_scr[...]
        o = (jnp.dot(attn.astype(BF16), vb, preferred_element_type=F32)
             + _dot_nt(scaled(q, b), st_prev.astype(BF16)))
        kd = scaled(k, [b_last - bi for bi in b])
        st_scr[...] = jnp.exp(b_last[0:1, :]) * st_prev + _dot_tn(vb, kd)

        ms = jnp.mean(o * o, axis=-1, keepdims=True)
        y = o * lax.rsqrt(ms + EPS) * gw * gt_ref[0, pl.ds(r0, CHUNK), :].astype(F32)
        o_ref[0, pl.ds(r0, CHUNK), :] = y.astype(BF16)
        return carry

    lax.fori_loop(0, SEQ // CHUNK, body, 0, unroll=4)


def _scan(q3, k3, gl3, v3, gt3, gnorm_w):
    blk = pl.BlockSpec((1, SEQ, HEAD_DIM), lambda b, h: (b, 0, h))
    return pl.pallas_call(
        _scan_kernel,
        grid=(BATCH, HGRN_HEADS),
        in_specs=[blk, blk, blk, blk, blk, pl.BlockSpec((1, HEAD_DIM), lambda b, h: (0, 0))],
        out_specs=blk,
        out_shape=jax.ShapeDtypeStruct((BATCH, SEQ, HGRN_WIDTH), BF16),
        scratch_shapes=[pltpu.VMEM((HEAD_DIM, HEAD_DIM), F32)],
        compiler_params=pltpu.CompilerParams(
            dimension_semantics=("arbitrary", "arbitrary"), vmem_limit_bytes=VMEM_LIMIT),
        name="hgrn_scan",
    )(q3, k3, gl3, v3, gt3, gnorm_w)


def _wout_kernel(x_ref, yp_ref, yh_ref, wp_ref, wh_ref, gt_ref, o_ref):
    mix = (jnp.dot(yp_ref[...], wp_ref[...], preferred_element_type=F32)
           + jnp.dot(yh_ref[...], wh_ref[...], preferred_element_type=F32))
    o_ref[...] = x_ref[...] + gt_ref[0] * mix


def _wout(x2d, yp, yh, w_out, mod3):
    m = x2d.shape[0]
    tm, tn = 512, 1024
    tiles_per_batch = SEQ // tm
    return pl.pallas_call(
        _wout_kernel,
        grid=(m // tm, D_MODEL // tn),
        in_specs=[
            pl.BlockSpec((tm, tn), lambda i, j: (i, j)),
            pl.BlockSpec((tm, POOL_WIDTH), lambda i, j: (i, 0)),
            pl.BlockSpec((tm, HGRN_WIDTH), lambda i, j: (i, 0)),
            pl.BlockSpec((POOL_WIDTH, tn), lambda i, j: (0, j)),
            pl.BlockSpec((HGRN_WIDTH, tn), lambda i, j: (1, j)),
            pl.BlockSpec((1, 1, tn), lambda i, j: ((i // tiles_per_batch) * N_MOD + 5, 0, j)),
        ],
        out_specs=pl.BlockSpec((tm, tn), lambda i, j: (i, j)),
        out_shape=jax.ShapeDtypeStruct((m, D_MODEL), F32),
        compiler_params=pltpu.CompilerParams(
            dimension_semantics=("arbitrary", "arbitrary"), vmem_limit_bytes=VMEM_LIMIT),
        name="mix_out",
    )(x2d, yp, yh, w_out, w_out, mod3)


def kernel(x, c, w_ada, b_ada, norm1_w, ffn1_gate, ffn1_up, ffn1_down, norm2_w, w_in, pool_w,
           pool_scale, lb_logits, gnorm_w, w_out, norm3_w, ffn2_gate, ffn2_up, ffn2_down,
           final_norm_w):
    B, S, D = x.shape
    assert (B, S, D) == (BATCH, SEQ, D_MODEL) and w_ada.shape[0] == 1
    assert lb_logits.shape == (2, HGRN_WIDTH)

    c_pad = jnp.pad(c, ((0, 8 - B), (0, 0)))
    mod = _ada(c_pad, w_ada[0], b_ada)[:B]
    mod3 = mod.reshape(B * N_MOD, 1, D)

    x2d = x.reshape(B * S, D)
    fw = final_norm_w.reshape(1, D)

    x1 = _ffn(x2d, norm1_w, mod3, 0, ffn1_gate[0].astype(BF16), ffn1_up[0].astype(BF16),
              ffn1_down[0].astype(BF16), fw, final_norm=False)

    u, q, k, gl, v, gt = _win(x1, norm2_w, mod3, w_in[0].astype(BF16), lb_logits)
    y_pool = _pool(u.reshape(B, S, POOL_WIDTH), pool_w[0].astype(BF16), pool_scale)
    y_hgrn = _scan(*(a.reshape(B, S, HGRN_WIDTH) for a in (q, k, gl, v, gt)), gnorm_w)
    x2 = _wout(x1, y_pool.reshape(B * S, POOL_WIDTH), y_hgrn.reshape(B * S, HGRN_WIDTH),
               w_out[0].astype(BF16), mod3)

    out = _ffn(x2, norm3_w, mod3, 6, ffn2_gate[0].astype(BF16), ffn2_up[0].astype(BF16),
               ffn2_down[0].astype(BF16), fw, final_norm=True)
    return out.reshape(B, S, D)
```

```python
import functools

import jax
import jax.numpy as jnp
from jax import lax
from jax.experimental import pallas as pl
from jax.experimental.pallas import tpu as pltpu

F32 = jnp.float32
BF16 = jnp.bfloat16

D_MODEL = 2048
BATCH = 4
SEQ = 2048
POOL_WIDTH = 1024
POOL_WINDOWS = (2, 4, 8, 16)
POOL_GROUP_DIM = 256
HGRN_WIDTH = 1024
HGRN_HEADS = 8
HEAD_DIM = 128
IN_WIDTH = 5120
CHUNK = 64
SUBLANES = 8
BF16_SUBLANES = 16
WIN_SUB_ROWS = 256
D_FF = 5632
N_MOD = 9
EPS = 1e-6

VMEM_LIMIT = 56 * 1024 * 1024


def _silu(v):
    return v * jax.nn.sigmoid(v)


def _norm_modulate(x, nw, sh, sc):
    ms = jnp.mean(x * x, axis=-1, keepdims=True)
    y = x * lax.rsqrt(ms + EPS) * nw
    return y * (1.0 + sc) + sh


def _ada_kernel(c_ref, w_ref, b_ref, o_ref):
    ca = _silu(c_ref[...])
    o_ref[...] = jnp.dot(ca, w_ref[...], preferred_element_type=F32,
                         precision=lax.Precision.HIGHEST) + b_ref[...]


def _ada(c_pad, w_ada, b_ada):
    n = w_ada.shape[1]
    tn = 1024
    return pl.pallas_call(
        _ada_kernel,
        grid=(n // tn,),
        in_specs=[
            pl.BlockSpec((8, D_MODEL), lambda j: (0, 0)),
            pl.BlockSpec((D_MODEL, tn), lambda j: (0, j)),
            pl.BlockSpec((1, tn), lambda j: (0, j)),
        ],
        out_specs=pl.BlockSpec((8, tn), lambda j: (0, j)),
        out_shape=jax.ShapeDtypeStruct((8, n), F32),
        compiler_params=pltpu.CompilerParams(
            dimension_semantics=("arbitrary",), vmem_limit_bytes=VMEM_LIMIT),
        name="ada_mod",
    )(c_pad, w_ada, b_ada)


def _ffn_kernel(*refs, nj, final_norm, n_cast):
    x_ref, nw_ref, sh_ref, sc_ref, gt_ref, wg_ref, wu_ref, wd_ref, fw_ref = refs[:9]
    cast_in = refs[9:9 + n_cast]
    o_ref = refs[9 + n_cast]
    cast_out = refs[10 + n_cast:10 + 2 * n_cast]
    h_scr, acc_scr = refs[10 + 2 * n_cast:]
    j = pl.program_id(1)

    for w_ref, wb_ref in zip(cast_in, cast_out):
        wb_ref[...] = w_ref[...].astype(BF16)

    @pl.when(j == 0)
    def _():
        h = _norm_modulate(x_ref[...], nw_ref[...], sh_ref[0], sc_ref[0])
        h_scr[...] = h.astype(BF16)
        acc_scr[...] = jnp.zeros_like(acc_scr)

    h = h_scr[...]
    g = jnp.dot(h, wg_ref[...], preferred_element_type=F32)
    u = jnp.dot(h, wu_ref[...], preferred_element_type=F32)
    a = (_silu(g) * u).astype(BF16)
    acc_scr[...] += jnp.dot(a, wd_ref[...], preferred_element_type=F32)

    @pl.when(j == nj - 1)
    def _():
        out = x_ref[...] + 0.5 * gt_ref[0] * acc_scr[...]
        if final_norm:
            ms = jnp.mean(out * out, axis=-1, keepdims=True)
            out = out * lax.rsqrt(ms + EPS) * fw_ref[...]
        o_ref[...] = out


def _ffn(x2d, nw, mod3, mod_base, wg, wu, wd, fw, *, final_norm, cast=()):
    m = x2d.shape[0]
    tm, tf = 512, 512
    ni, nj = m // tm, D_FF // tf
    tiles_per_batch = SEQ // tm

    def mod_map(k):
        return lambda i, j: ((i // tiles_per_batch) * N_MOD + mod_base + k, 0, 0)

    cast_specs = []
    for w in cast:
        rows = w.shape[0]
        nblk = max(n for n in range(1, ni * nj + 1) if rows % (BF16_SUBLANES * n) == 0)
        cast_specs.append(pl.BlockSpec(
            (rows // nblk, w.shape[1]),
            lambda i, j, nblk=nblk: (jnp.minimum(i * nj + j, nblk - 1), 0)))

    kern = functools.partial(_ffn_kernel, nj=nj, final_norm=final_norm, n_cast=len(cast))
    res = pl.pallas_call(
        kern,
        grid=(ni, nj),
        in_specs=[
            pl.BlockSpec((tm, D_MODEL), lambda i, j: (i, 0)),
            pl.BlockSpec((1, D_MODEL), lambda i, j: (0, 0)),
            pl.BlockSpec((1, 1, D_MODEL), mod_map(0)),
            pl.BlockSpec((1, 1, D_MODEL), mod_map(1)),
            pl.BlockSpec((1, 1, D_MODEL), mod_map(2)),
            pl.BlockSpec((D_MODEL, tf), lambda i, j: (0, j)),
            pl.BlockSpec((D_MODEL, tf), lambda i, j: (0, j)),
            pl.BlockSpec((tf, D_MODEL), lambda i, j: (j, 0)),
            pl.BlockSpec((1, D_MODEL), lambda i, j: (0, 0)),
        ] + cast_specs,
        out_specs=[pl.BlockSpec((tm, D_MODEL), lambda i, j: (i, 0))] + cast_specs,
        out_shape=[jax.ShapeDtypeStruct((m, D_MODEL), F32)]
        + [jax.ShapeDtypeStruct(w.shape, BF16) for w in cast],
        scratch_shapes=[pltpu.VMEM((tm, D_MODEL), BF16), pltpu.VMEM((tm, D_MODEL), F32)],
        compiler_params=pltpu.CompilerParams(
            dimension_semantics=("arbitrary", "arbitrary"), vmem_limit_bytes=VMEM_LIMIT),
        name="ffn_final" if final_norm else "ffn",
    )(x2d, nw, mod3, mod3, mod3, wg, wu, wd, fw, *cast)
    return res


def _win_kernel(x_ref, nw_ref, sh_ref, sc_ref, w_ref, lbl_ref,
                u_ref, q_ref, k_ref, gl_ref, v_ref, gt_ref, h_scr):
    j = pl.program_id(1)
    tm = x_ref.shape[0]
    subs = [pl.ds(r, WIN_SUB_ROWS) for r in range(0, tm, WIN_SUB_ROWS)]

    def project(rows):
        return jnp.dot(h_scr[rows, :], w_ref[...], preferred_element_type=F32)

    @pl.when(j == 0)
    def _():
        for rows in subs:
            h = _norm_modulate(x_ref[rows, :], nw_ref[...], sh_ref[0], sc_ref[0])
            h_scr[rows, :] = h.astype(BF16)
            u_ref[rows, :] = project(rows)

    @pl.when(j == 1)
    def _():
        for rows in subs:
            q_ref[rows, :] = _silu(project(rows)).astype(BF16)

    @pl.when(j == 2)
    def _():
        lbl = lbl_ref[...]
        pe = jnp.exp(lbl - jnp.max(lbl, axis=0, keepdims=True))
        p = pe / jnp.sum(pe, axis=0, keepdims=True)
        lb = (p[0:1, :] + p[1:2, :]) - p[0:1, :]
        for rows in subs:
            forget = lb + (1.0 - lb) * jax.nn.sigmoid(project(rows))
            k_ref[rows, :] = (1.0 - forget).astype(BF16)
            gl_ref[rows, :] = jnp.log(forget)

    @pl.when(j == 3)
    def _():
        for rows in subs:
            v_ref[rows, :] = project(rows).astype(BF16)

    @pl.when(j == 4)
    def _():
        for rows in subs:
            gt_ref[rows, :] = _silu(project(rows)).astype(BF16)


def _win(x2d, nw, mod3, w_in, lb_logits):
    m = x2d.shape[0]
    tm, tn = 512, 1024
    assert IN_WIDTH == 5 * tn and POOL_WIDTH == tn and HGRN_WIDTH == tn
    tiles_per_batch = SEQ // tm

    def mod_map(k):
        return lambda i, j: ((i // tiles_per_batch) * N_MOD + 3 + k, 0, 0)

    seg = pl.BlockSpec((tm, tn), lambda i, j: (i, 0))
    return pl.pallas_call(
        _win_kernel,
        grid=(m // tm, IN_WIDTH // tn),
        in_specs=[
            pl.BlockSpec((tm, D_MODEL), lambda i, j: (i, 0)),
            pl.BlockSpec((1, D_MODEL), lambda i, j: (0, 0)),
            pl.BlockSpec((1, 1, D_MODEL), mod_map(0)),
            pl.BlockSpec((1, 1, D_MODEL), mod_map(1)),
            pl.BlockSpec((D_MODEL, tn), lambda i, j: (0, j)),
            pl.BlockSpec((2, tn), lambda i, j: (0, 0)),
        ],
        out_specs=[seg] * 6,
        out_shape=[
            jax.ShapeDtypeStruct((m, tn), F32),
            jax.ShapeDtypeStruct((m, tn), BF16),
            jax.ShapeDtypeStruct((m, tn), BF16),
            jax.ShapeDtypeStruct((m, tn), F32),
            jax.ShapeDtypeStruct((m, tn), BF16),
            jax.ShapeDtypeStruct((m, tn), BF16),
        ],
        scratch_shapes=[pltpu.VMEM((tm, D_MODEL), BF16)],
        compiler_params=pltpu.CompilerParams(
            dimension_semantics=("arbitrary", "arbitrary"), vmem_limit_bytes=VMEM_LIMIT),
        name="mix_in",
    )(x2d, nw, mod3, mod3, w_in, lb_logits)


def _pool_kernel(u_ref, pw_ref, ps_ref, o_ref):
    g = pl.program_id(1)
    u = u_ref[0]
    t = lax.broadcasted_iota(jnp.int32, (SEQ, 1), 0)

    def shifted(v, k):
        return jnp.where(t >= k, pltpu.roll(v, k, axis=0), 0.0)

    s2 = u + shifted(u, 1)
    s4 = s2 + shifted(s2, 2)
    s8 = s4 + shifted(s4, 4)
    s16 = s8 + shifted(s8, 8)
    win = jnp.where(g == 0, s2, jnp.where(g == 1, s4, jnp.where(g == 2, s8, s16)))
    cnt = jnp.minimum(t + 1, 2 << g).astype(F32)
    pooled = win / cnt - u
    y = jnp.dot(pooled.astype(BF16), pw_ref[0], preferred_element_type=F32) * ps_ref[...]
    o_ref[0] = y.astype(BF16)


def _pool(u3, pool_w, pool_scale):
    assert POOL_WINDOWS == (2, 4, 8, 16)
    return pl.pallas_call(
        _pool_kernel,
        grid=(BATCH, len(POOL_WINDOWS)),
        in_specs=[
            pl.BlockSpec((1, SEQ, POOL_GROUP_DIM), lambda b, g: (b, 0, g)),
            pl.BlockSpec((1, POOL_GROUP_DIM, POOL_GROUP_DIM), lambda b, g: (g, 0, 0)),
            pl.BlockSpec((1, POOL_GROUP_DIM), lambda b, g: (0, g)),
        ],
        out_specs=pl.BlockSpec((1, SEQ, POOL_GROUP_DIM), lambda b, g: (b, 0, g)),
        out_shape=jax.ShapeDtypeStruct((BATCH, SEQ, POOL_WIDTH), BF16),
        compiler_params=pltpu.CompilerParams(
            dimension_semantics=("arbitrary", "arbitrary"), vmem_limit_bytes=VMEM_LIMIT),
        name="pool",
    )(u3, pool_w, pool_scale)


def _dot_nt(a, b):
    return lax.dot_general(a, b, (((1,), (1,)), ((), ())), preferred_element_type=F32)


def _dot_tn(a, b):
    return lax.dot_general(a, b, (((0,), (0,)), ((), ())), preferred_element_type=F32)


LEVEL_SIZES = (32, 16, 8, 4, 2, 1)


def _row_bcast(v, r):
    return jnp.broadcast_to(v[r:r + 1, :], v.shape)


def _level_exponents(g):
    n = len(g)
    r8 = lax.broadcasted_iota(jnp.int32, (SUBLANES, 1), 0)
    p = list(g)
    for s in (1, 2, 4):
        p = [pi + jnp.where(r8 >= s, pltpu.roll(pi, s, axis=0), 0.0) for pi in p]
    off = [None, _row_bcast(p[0], SUBLANES - 1)]
    for i in range(1, n):
        off.append(off[i] + _row_bcast(p[i], SUBLANES - 1))
    b = [p[0]] + [p[i] + off[i] for i in range(1, n)]

    exps = []
    for hs in LEVEL_SIZES:
        if hs >= SUBLANES:
            w = hs // SUBLANES
            lvl = []
            for i in range(n):
                ref = off[(i // (2 * w)) * 2 * w + w]
                lvl.append(b[i] - ref if (i // w) % 2 == 1 else ref - b[i])
        elif hs == 1:
            lvl = [jnp.where(r8 % 2 == 1, gi, 0.0) for gi in g]
        else:
            lvl = []
            for pi in p:
                ref = _row_bcast(pi, hs - 1)
                for blk in range(1, SUBLANES // (2 * hs)):
                    ref = jnp.where(r8 >= blk * 2 * hs, _row_bcast(pi, blk * 2 * hs + hs - 1), ref)
                lvl.append(-jnp.abs(pi - ref))
        exps.append(lvl)
    return exps, b, off[n]


def _scan_kernel(q_ref, k_ref, gl_ref, v_ref, gt_ref, gw_ref, o_ref, st_scr):
    gw = gw_ref[...]
    nt = CHUNK // SUBLANES
    ti = lax.broadcasted_iota(jnp.int32, (CHUNK, CHUNK), 0)
    si = lax.broadcasted_iota(jnp.int32, (CHUNK, CHUNK), 1)
    owner = jnp.where(ti == si, len(LEVEL_SIZES), -1)
    for idx, hs in enumerate(LEVEL_SIZES):
        owner = jnp.where(((ti // (2 * hs)) == (si // (2 * hs))) & ((ti // hs) > (si // hs)),
                          idx, owner)
    st_scr[...] = jnp.zeros_like(st_scr)

    def tiles(a):
        return [a[i * SUBLANES:(i + 1) * SUBLANES, :] for i in range(nt)]

    def scaled(a_tiles, e_tiles):
        return jnp.concatenate([a * jnp.exp(e) for a, e in zip(a_tiles, e_tiles)],
                               axis=0).astype(BF16)

    def body(c, carry):
        r0 = pl.multiple_of(c * CHUNK, CHUNK)
        qb = q_ref[0, pl.ds(r0, CHUNK), :]
        kb = k_ref[0, pl.ds(r0, CHUNK), :]
        vb = v_ref[0, pl.ds(r0, CHUNK), :]
        q = tiles(qb.astype(F32))
        k = tiles(kb.astype(F32))
        exps, b, b_last = _level_exponents(tiles(gl_ref[0, pl.ds(r0, CHUNK), :]))

        attn = jnp.where(owner == len(LEVEL_SIZES), _dot_nt(qb, kb), 0.0)
        for idx, e in enumerate(exps):
            attn = jnp.where(owner == idx, _dot_nt(scaled(q, e), scaled(k, e)), attn)

        st_prev = st_scr[...]
        o = (jnp.dot(attn.astype(BF16), vb, preferred_element_type=F32)
             + _dot_nt(scaled(q, b), st_prev.astype(BF16)))
        kd = scaled(k, [b_last - bi for bi in b])
        st_scr[...] = jnp.exp(b_last[0:1, :]) * st_prev + _dot_tn(vb, kd)

        ms = jnp.mean(o * o, axis=-1, keepdims=True)
        y = o * lax.rsqrt(ms + EPS) * gw * gt_ref[0, pl.ds(r0, CHUNK), :].astype(F32)
        o_ref[0, pl.ds(r0, CHUNK), :] = y.astype(BF16)
        return carry

    lax.fori_loop(0, SEQ // CHUNK, body, 0, unroll=4)


def _scan(q3, k3, gl3, v3, gt3, gnorm_w):
    blk = pl.BlockSpec((1, SEQ, HEAD_DIM), lambda b, h: (b, 0, h))
    return pl.pallas_call(
        _scan_kernel,
        grid=(BATCH, HGRN_HEADS),
        in_specs=[blk, blk, blk, blk, blk, pl.BlockSpec((1, HEAD_DIM), lambda b, h: (0, 0))],
        out_specs=blk,
        out_shape=jax.ShapeDtypeStruct((BATCH, SEQ, HGRN_WIDTH), BF16),
        scratch_shapes=[pltpu.VMEM((HEAD_DIM, HEAD_DIM), F32)],
        compiler_params=pltpu.CompilerParams(
            dimension_semantics=("arbitrary", "arbitrary"), vmem_limit_bytes=VMEM_LIMIT),
        name="hgrn_scan",
    )(q3, k3, gl3, v3, gt3, gnorm_w)


def _wout_kernel(x_ref, yp_ref, yh_ref, wp_ref, wh_ref, gt_ref, o_ref):
    mix = (jnp.dot(yp_ref[...], wp_ref[...], preferred_element_type=F32)
           + jnp.dot(yh_ref[...], wh_ref[...], preferred_element_type=F32))
    o_ref[...] = x_ref[...] + gt_ref[0] * mix


def _wout(x2d, yp, yh, w_out, mod3):
    m = x2d.shape[0]
    tm, tn = 512, 1024
    tiles_per_batch = SEQ // tm
    return pl.pallas_call(
        _wout_kernel,
        grid=(m // tm, D_MODEL // tn),
        in_specs=[
            pl.BlockSpec((tm, tn), lambda i, j: (i, j)),
            pl.BlockSpec((tm, POOL_WIDTH), lambda i, j: (i, 0)),
            pl.BlockSpec((tm, HGRN_WIDTH), lambda i, j: (i, 0)),
            pl.BlockSpec((POOL_WIDTH, tn), lambda i, j: (0, j)),
            pl.BlockSpec((HGRN_WIDTH, tn), lambda i, j: (1, j)),
            pl.BlockSpec((1, 1, tn), lambda i, j: ((i // tiles_per_batch) * N_MOD + 5, 0, j)),
        ],
        out_specs=pl.BlockSpec((tm, tn), lambda i, j: (i, j)),
        out_shape=jax.ShapeDtypeStruct((m, D_MODEL), F32),
        compiler_params=pltpu.CompilerParams(
            dimension_semantics=("arbitrary", "arbitrary"), vmem_limit_bytes=VMEM_LIMIT),
        name="mix_out",
    )(x2d, yp, yh, w_out, w_out, mod3)


def kernel(x, c, w_ada, b_ada, norm1_w, ffn1_gate, ffn1_up, ffn1_down, norm2_w, w_in, pool_w,
           pool_scale, lb_logits, gnorm_w, w_out, norm3_w, ffn2_gate, ffn2_up, ffn2_down,
           final_norm_w):
    B, S, D = x.shape
    assert (B, S, D) == (BATCH, SEQ, D_MODEL) and w_ada.shape[0] == 1
    assert lb_logits.shape == (2, HGRN_WIDTH)

    c_pad = jnp.pad(c, ((0, 8 - B), (0, 0)))
    mod = _ada(c_pad, w_ada[0], b_ada)[:B]
    mod3 = mod.reshape(B * N_MOD, 1, D)

    x2d = x.reshape(B * S, D)
    fw = final_norm_w.reshape(1, D)

    x1, w_in_b, w_out_b, g2_b, u2_b, d2_b = _ffn(
        x2d, norm1_w, mod3, 0, ffn1_gate[0].astype(BF16), ffn1_up[0].astype(BF16),
        ffn1_down[0].astype(BF16), fw, final_norm=False,
        cast=(w_in[0], w_out[0], ffn2_gate[0], ffn2_up[0], ffn2_down[0]))

    u, q, k, gl, v, gt = _win(x1, norm2_w, mod3, w_in_b, lb_logits)
    y_pool = _pool(u.reshape(B, S, POOL_WIDTH), pool_w[0].astype(BF16), pool_scale)
    y_hgrn = _scan(*(a.reshape(B, S, HGRN_WIDTH) for a in (q, k, gl, v, gt)), gnorm_w)
    x2 = _wout(x1, y_pool.reshape(B * S, POOL_WIDTH), y_hgrn.reshape(B * S, HGRN_WIDTH),
               w_out_b, mod3)

    out, = _ffn(x2, norm3_w, mod3, 6, g2_b, u2_b, d2_b, fw, final_norm=True)
    return out.reshape(B, S, D)
```

```python
import functools

import jax
import jax.numpy as jnp
from jax import lax
from jax.experimental import pallas as pl
from jax.experimental.pallas import tpu as pltpu

F32 = jnp.float32
BF16 = jnp.bfloat16

D_MODEL = 2048
BATCH = 4
SEQ = 2048
POOL_WIDTH = 1024
POOL_WINDOWS = (2, 4, 8, 16)
POOL_GROUP_DIM = 256
HGRN_WIDTH = 1024
HGRN_HEADS = 8
HEAD_DIM = 128
IN_WIDTH = 5120
CHUNK = 64
SUBLANES = 8
BF16_SUBLANES = 16
WIN_SUB_ROWS = 256
FFN_SUB_ROWS = 256
D_FF = 5632
N_MOD = 9
EPS = 1e-6

VMEM_LIMIT = 60 * 1024 * 1024


def _silu(v):
    return v * jax.nn.sigmoid(v)


def _norm_modulate(x, nw, sh, sc):
    ms = jnp.mean(x * x, axis=-1, keepdims=True)
    y = x * lax.rsqrt(ms + EPS) * nw
    return y * (1.0 + sc) + sh


def _ada_kernel(c_ref, w_ref, b_ref, o_ref):
    ca = _silu(c_ref[...])
    o_ref[...] = jnp.dot(ca, w_ref[...], preferred_element_type=F32,
                         precision=lax.Precision.HIGHEST) + b_ref[...]


def _ada(c_pad, w_ada, b_ada):
    n = w_ada.shape[1]
    tn = 1024
    return pl.pallas_call(
        _ada_kernel,
        grid=(n // tn,),
        in_specs=[
            pl.BlockSpec((8, D_MODEL), lambda j: (0, 0)),
            pl.BlockSpec((D_MODEL, tn), lambda j: (0, j)),
            pl.BlockSpec((1, tn), lambda j: (0, j)),
        ],
        out_specs=pl.BlockSpec((8, tn), lambda j: (0, j)),
        out_shape=jax.ShapeDtypeStruct((8, n), F32),
        compiler_params=pltpu.CompilerParams(
            dimension_semantics=("arbitrary",), vmem_limit_bytes=VMEM_LIMIT),
        name="ada_mod",
    )(c_pad, w_ada, b_ada)


def _ffn_kernel(*refs, nj, final_norm, n_cast):
    x_ref, nw_ref, sh_ref, sc_ref, gt_ref, wg_ref, wu_ref, wd_ref, fw_ref = refs[:9]
    cast_in = refs[9:9 + n_cast]
    o_ref = refs[9 + n_cast]
    cast_out = refs[10 + n_cast:10 + 2 * n_cast]
    h_scr, = refs[10 + 2 * n_cast:]
    j = pl.program_id(1)
    tm = x_ref.shape[0]
    subs = [pl.ds(r, FFN_SUB_ROWS) for r in range(0, tm, FFN_SUB_ROWS)]

    for w_ref, wb_ref in zip(cast_in, cast_out):
        wb_ref[...] = w_ref[...].astype(BF16)

    def partial_ffn(rows):
        h = h_scr[rows, :]
        g = jnp.dot(h, wg_ref[...], preferred_element_type=F32)
        u = jnp.dot(h, wu_ref[...], preferred_element_type=F32)
        a = (_silu(g) * u).astype(BF16)
        return jnp.dot(a, wd_ref[...], preferred_element_type=F32)

    @pl.when(j == 0)
    def _():
        for rows in subs:
            h = _norm_modulate(x_ref[rows, :], nw_ref[...], sh_ref[0], sc_ref[0])
            h_scr[rows, :] = h.astype(BF16)
            o_ref[rows, :] = partial_ffn(rows)

    @pl.when((j > 0) & (j < nj - 1))
    def _():
        for rows in subs:
            o_ref[rows, :] += partial_ffn(rows)

    @pl.when(j == nj - 1)
    def _():
        for rows in subs:
            out = x_ref[rows, :] + 0.5 * gt_ref[0] * (o_ref[rows, :] + partial_ffn(rows))
            if final_norm:
                ms = jnp.mean(out * out, axis=-1, keepdims=True)
                out = out * lax.rsqrt(ms + EPS) * fw_ref[...]
            o_ref[rows, :] = out


def _ffn(x2d, nw, mod3, mod_base, wg, wu, wd, fw, *, final_norm, cast=()):
    m = x2d.shape[0]
    tm, tf = 1024, 512
    ni, nj = m // tm, D_FF // tf
    tiles_per_batch = SEQ // tm

    def mod_map(k):
        return lambda i, j: ((i // tiles_per_batch) * N_MOD + mod_base + k, 0, 0)

    cast_specs = []
    for w in cast:
        rows = w.shape[0]
        nblk = max(n for n in range(1, ni * nj + 1) if rows % (BF16_SUBLANES * n) == 0)
        cast_specs.append(pl.BlockSpec(
            (rows // nblk, w.shape[1]),
            lambda i, j, nblk=nblk: (jnp.minimum(i * nj + j, nblk - 1), 0)))

    kern = functools.partial(_ffn_kernel, nj=nj, final_norm=final_norm, n_cast=len(cast))
    res = pl.pallas_call(
        kern,
        grid=(ni, nj),
        in_specs=[
            pl.BlockSpec((tm, D_MODEL), lambda i, j: (i, 0)),
            pl.BlockSpec((1, D_MODEL), lambda i, j: (0, 0)),
            pl.BlockSpec((1, 1, D_MODEL), mod_map(0)),
            pl.BlockSpec((1, 1, D_MODEL), mod_map(1)),
            pl.BlockSpec((1, 1, D_MODEL), mod_map(2)),
            pl.BlockSpec((D_MODEL, tf), lambda i, j: (0, j)),
            pl.BlockSpec((D_MODEL, tf), lambda i, j: (0, j)),
            pl.BlockSpec((tf, D_MODEL), lambda i, j: (j, 0)),
            pl.BlockSpec((1, D_MODEL), lambda i, j: (0, 0)),
        ] + cast_specs,
        out_specs=[pl.BlockSpec((tm, D_MODEL), lambda i, j: (i, 0))] + cast_specs,
        out_shape=[jax.ShapeDtypeStruct((m, D_MODEL), F32)]
        + [jax.ShapeDtypeStruct(w.shape, BF16) for w in cast],
        scratch_shapes=[pltpu.VMEM((tm, D_MODEL), BF16)],
        compiler_params=pltpu.CompilerParams(
            dimension_semantics=("arbitrary", "arbitrary"), vmem_limit_bytes=VMEM_LIMIT),
        name="ffn_final" if final_norm else "ffn",
    )(x2d, nw, mod3, mod3, mod3, wg, wu, wd, fw, *cast)
    return res


def _win_kernel(x_ref, nw_ref, sh_ref, sc_ref, w_ref, lbl_ref,
                u_ref, q_ref, k_ref, gl_ref, v_ref, gt_ref, h_scr):
    j = pl.program_id(1)
    tm = x_ref.shape[0]
    subs = [pl.ds(r, WIN_SUB_ROWS) for r in range(0, tm, WIN_SUB_ROWS)]

    def project(rows):
        return jnp.dot(h_scr[rows, :], w_ref[...], preferred_element_type=F32)

    @pl.when(j == 0)
    def _():
        for rows in subs:
            h = _norm_modulate(x_ref[rows, :], nw_ref[...], sh_ref[0], sc_ref[0])
            h_scr[rows, :] = h.astype(BF16)
            u_ref[rows, :] = project(rows)

    @pl.when(j == 1)
    def _():
        for rows in subs:
            q_ref[rows, :] = _silu(project(rows)).astype(BF16)

    @pl.when(j == 2)
    def _():
        lbl = lbl_ref[...]
        pe = jnp.exp(lbl - jnp.max(lbl, axis=0, keepdims=True))
        p = pe / jnp.sum(pe, axis=0, keepdims=True)
        lb = (p[0:1, :] + p[1:2, :]) - p[0:1, :]
        for rows in subs:
            forget = lb + (1.0 - lb) * jax.nn.sigmoid(project(rows))
            k_ref[rows, :] = (1.0 - forget).astype(BF16)
            gl_ref[rows, :] = jnp.log(forget)

    @pl.when(j == 3)
    def _():
        for rows in subs:
            v_ref[rows, :] = project(rows).astype(BF16)

    @pl.when(j == 4)
    def _():
        for rows in subs:
            gt_ref[rows, :] = _silu(project(rows)).astype(BF16)


def _win(x2d, nw, mod3, w_in, lb_logits):
    m = x2d.shape[0]
    tm, tn = 512, 1024
    assert IN_WIDTH == 5 * tn and POOL_WIDTH == tn and HGRN_WIDTH == tn
    tiles_per_batch = SEQ // tm

    def mod_map(k):
        return lambda i, j: ((i // tiles_per_batch) * N_MOD + 3 + k, 0, 0)

    seg = pl.BlockSpec((tm, tn), lambda i, j: (i, 0))
    return pl.pallas_call(
        _win_kernel,
        grid=(m // tm, IN_WIDTH // tn),
        in_specs=[
            pl.BlockSpec((tm, D_MODEL), lambda i, j: (i, 0)),
            pl.BlockSpec((1, D_MODEL), lambda i, j: (0, 0)),
            pl.BlockSpec((1, 1, D_MODEL), mod_map(0)),
            pl.BlockSpec((1, 1, D_MODEL), mod_map(1)),
            pl.BlockSpec((D_MODEL, tn), lambda i, j: (0, j)),
            pl.BlockSpec((2, tn), lambda i, j: (0, 0)),
        ],
        out_specs=[seg] * 6,
        out_shape=[
            jax.ShapeDtypeStruct((m, tn), F32),
            jax.ShapeDtypeStruct((m, tn), BF16),
            jax.ShapeDtypeStruct((m, tn), BF16),
            jax.ShapeDtypeStruct((m, tn), F32),
            jax.ShapeDtypeStruct((m, tn), BF16),
            jax.ShapeDtypeStruct((m, tn), BF16),
        ],
        scratch_shapes=[pltpu.VMEM((tm, D_MODEL), BF16)],
        compiler_params=pltpu.CompilerParams(
            dimension_semantics=("arbitrary", "arbitrary"), vmem_limit_bytes=VMEM_LIMIT),
        name="mix_in",
    )(x2d, nw, mod3, mod3, w_in, lb_logits)


def _pool_kernel(u_ref, pw_ref, ps_ref, o_ref):
    g = pl.program_id(1)
    u = u_ref[0]
    t = lax.broadcasted_iota(jnp.int32, (SEQ, 1), 0)

    def shifted(v, k):
        return jnp.where(t >= k, pltpu.roll(v, k, axis=0), 0.0)

    s2 = u + shifted(u, 1)
    s4 = s2 + shifted(s2, 2)
    s8 = s4 + shifted(s4, 4)
    s16 = s8 + shifted(s8, 8)
    win = jnp.where(g == 0, s2, jnp.where(g == 1, s4, jnp.where(g == 2, s8, s16)))
    cnt = jnp.minimum(t + 1, 2 << g).astype(F32)
    pooled = win / cnt - u
    y = jnp.dot(pooled.astype(BF16), pw_ref[0], preferred_element_type=F32) * ps_ref[...]
    o_ref[0] = y.astype(BF16)


def _pool(u3, pool_w, pool_scale):
    assert POOL_WINDOWS == (2, 4, 8, 16)
    return pl.pallas_call(
        _pool_kernel,
        grid=(BATCH, len(POOL_WINDOWS)),
        in_specs=[
            pl.BlockSpec((1, SEQ, POOL_GROUP_DIM), lambda b, g: (b, 0, g)),
            pl.BlockSpec((1, POOL_GROUP_DIM, POOL_GROUP_DIM), lambda b, g: (g, 0, 0)),
            pl.BlockSpec((1, POOL_GROUP_DIM), lambda b, g: (0, g)),
        ],
        out_specs=pl.BlockSpec((1, SEQ, POOL_GROUP_DIM), lambda b, g: (b, 0, g)),
        out_shape=jax.ShapeDtypeStruct((BATCH, SEQ, POOL_WIDTH), BF16),
        compiler_params=pltpu.CompilerParams(
            dimension_semantics=("arbitrary", "arbitrary"), vmem_limit_bytes=VMEM_LIMIT),
        name="pool",
    )(u3, pool_w, pool_scale)


def _dot_nt(a, b):
    return lax.dot_general(a, b, (((1,), (1,)), ((), ())), preferred_element_type=F32)


def _dot_tn(a, b):
    return lax.dot_general(a, b, (((0,), (0,)), ((), ())), preferred_element_type=F32)


LEVEL_SIZES = (32, 16, 8, 4, 2, 1)


def _row_bcast(v, r):
    return jnp.broadcast_to(v[r:r + 1, :], v.shape)


def _level_exponents(g):
    n = len(g)
    r8 = lax.broadcasted_iota(jnp.int32, (SUBLANES, 1), 0)
    p = list(g)
    for s in (1, 2, 4):
        p = [pi + jnp.where(r8 >= s, pltpu.roll(pi, s, axis=0), 0.0) for pi in p]
    off = [None, _row_bcast(p[0], SUBLANES - 1)]
    for i in range(1, n):
        off.append(off[i] + _row_bcast(p[i], SUBLANES - 1))
    b = [p[0]] + [p[i] + off[i] for i in range(1, n)]

    exps = []
    for hs in LEVEL_SIZES:
        if hs >= SUBLANES:
            w = hs // SUBLANES
            lvl = []
            for i in range(n):
                ref = off[(i // (2 * w)) * 2 * w + w]
                lvl.append(b[i] - ref if (i // w) % 2 == 1 else ref - b[i])
        elif hs == 1:
            lvl = [jnp.where(r8 % 2 == 1, gi, 0.0) for gi in g]
        else:
            lvl = []
            for pi in p:
                ref = _row_bcast(pi, hs - 1)
                for blk in range(1, SUBLANES // (2 * hs)):
                    ref = jnp.where(r8 >= blk * 2 * hs, _row_bcast(pi, blk * 2 * hs + hs - 1), ref)
                lvl.append(-jnp.abs(pi - ref))
        exps.append(lvl)
    return exps, b, off[n]


def _scan_kernel(q_ref, k_ref, gl_ref, v_ref, gt_ref, gw_ref, o_ref, st_scr):
    gw = gw_ref[...]
    nt = CHUNK // SUBLANES
    ti = lax.broadcasted_iota(jnp.int32, (CHUNK, CHUNK), 0)
    si = lax.broadcasted_iota(jnp.int32, (CHUNK, CHUNK), 1)
    owner = jnp.where(ti == si, len(LEVEL_SIZES), -1)
    for idx, hs in enumerate(LEVEL_SIZES):
        owner = jnp.where(((ti // (2 * hs)) == (si // (2 * hs))) & ((ti // hs) > (si // hs)),
                          idx, owner)
    st_scr[...] = jnp.zeros_like(st_scr)

    def tiles(a):
        return [a[i * SUBLANES:(i + 1) * SUBLANES, :] for i in range(nt)]

    def scaled(a_tiles, e_tiles):
        return jnp.concatenate([a * jnp.exp(e) for a, e in zip(a_tiles, e_tiles)],
                               axis=0).astype(BF16)

    def body(c, carry):
        r0 = pl.multiple_of(c * CHUNK, CHUNK)
        qb = q_ref[0, pl.ds(r0, CHUNK), :]
        kb = k_ref[0, pl.ds(r0, CHUNK), :]
        vb = v_ref[0, pl.ds(r0, CHUNK), :]
        q = tiles(qb.astype(F32))
        k = tiles(kb.astype(F32))
        exps, b, b_last = _level_exponents(tiles(gl_ref[0, pl.ds(r0, CHUNK), :]))

        attn = jnp.where(owner == len(LEVEL_SIZES), _dot_nt(qb, kb), 0.0)
        for idx, e in enumerate(exps):
            attn = jnp.where(owner == idx, _dot_nt(scaled(q, e), scaled(k, e)), attn)

        st_prev = st_scr[...]
        o = (jnp.dot(attn.astype(BF16), vb, preferred_element_type=F32)
             + _dot_nt(scaled(q, b), st_prev.astype(BF16)))
        kd = scaled(k, [b_last - bi for bi in b])
        st_scr[...] = jnp.exp(b_last[0:1, :]) * st_prev + _dot_tn(vb, kd)

        ms = jnp.mean(o * o, axis=-1, keepdims=True)
        y = o * lax.rsqrt(ms + EPS) * gw * gt_ref[0, pl.ds(r0, CHUNK), :].astype(F32)
        o_ref[0, pl.ds(r0, CHUNK), :] = y.astype(BF16)
        return carry

    lax.fori_loop(0, SEQ // CHUNK, body, 0, unroll=4)


def _scan(q3, k3, gl3, v3, gt3, gnorm_w):
    blk = pl.BlockSpec((1, SEQ, HEAD_DIM), lambda b, h: (b, 0, h))
    return pl.pallas_call(
        _scan_kernel,
        grid=(BATCH, HGRN_HEADS),
        in_specs=[blk, blk, blk, blk, blk, pl.BlockSpec((1, HEAD_DIM), lambda b, h: (0, 0))],
        out_specs=blk,
        out_shape=jax.ShapeDtypeStruct((BATCH, SEQ, HGRN_WIDTH), BF16),
        scratch_shapes=[pltpu.VMEM((HEAD_DIM, HEAD_DIM), F32)],
        compiler_params=pltpu.CompilerParams(
            dimension_semantics=("arbitrary", "arbitrary"), vmem_limit_bytes=VMEM_LIMIT),
        name="hgrn_scan",
    )(q3, k3, gl3, v3, gt3, gnorm_w)


def _wout_kernel(x_ref, yp_ref, yh_ref, wp_ref, wh_ref, gt_ref, o_ref):
    mix = (jnp.dot(yp_ref[...], wp_ref[...], preferred_element_type=F32)
           + jnp.dot(yh_ref[...], wh_ref[...], preferred_element_type=F32))
    o_ref[...] = x_ref[...] + gt_ref[0] * mix


def _wout(x2d, yp, yh, w_out, mod3):
    m = x2d.shape[0]
    tm, tn = 512, 1024
    tiles_per_batch = SEQ // tm
    return pl.pallas_call(
        _wout_kernel,
        grid=(m // tm, D_MODEL // tn),
        in_specs=[
            pl.BlockSpec((tm, tn), lambda i, j: (i, j)),
            pl.BlockSpec((tm, POOL_WIDTH), lambda i, j: (i, 0)),
            pl.BlockSpec((tm, HGRN_WIDTH), lambda i, j: (i, 0)),
            pl.BlockSpec((POOL_WIDTH, tn), lambda i, j: (0, j)),
            pl.BlockSpec((HGRN_WIDTH, tn), lambda i, j: (1, j)),
            pl.BlockSpec((1, 1, tn), lambda i, j: ((i // tiles_per_batch) * N_MOD + 5, 0, j)),
        ],
        out_specs=pl.BlockSpec((tm, tn), lambda i, j: (i, j)),
        out_shape=jax.ShapeDtypeStruct((m, D_MODEL), F32),
        compiler_params=pltpu.CompilerParams(
            dimension_semantics=("arbitrary", "arbitrary"), vmem_limit_bytes=VMEM_LIMIT),
        name="mix_out",
    )(x2d, yp, yh, w_out, w_out, mod3)


def kernel(x, c, w_ada, b_ada, norm1_w, ffn1_gate, ffn1_up, ffn1_down, norm2_w, w_in, pool_w,
           pool_scale, lb_logits, gnorm_w, w_out, norm3_w, ffn2_gate, ffn2_up, ffn2_down,
           final_norm_w):
    B, S, D = x.shape
    assert (B, S, D) == (BATCH, SEQ, D_MODEL) and w_ada.shape[0] == 1
    assert lb_logits.shape == (2, HGRN_WIDTH)

    c_pad = jnp.pad(c, ((0, 8 - B), (0, 0)))
    mod = _ada(c_pad, w_ada[0], b_ada)[:B]
    mod3 = mod.reshape(B * N_MOD, 1, D)

    x2d = x.reshape(B * S, D)
    fw = final_norm_w.reshape(1, D)

    x1, w_in_b, w_out_b, g2_b, u2_b, d2_b = _ffn(
        x2d, norm1_w, mod3, 0, ffn1_gate[0].astype(BF16), ffn1_up[0].astype(BF16),
        ffn1_down[0].astype(BF16), fw, final_norm=False,
        cast=(w_in[0], w_out[0], ffn2_gate[0], ffn2_up[0], ffn2_down[0]))

    u, q, k, gl, v, gt = _win(x1, norm2_w, mod3, w_in_b, lb_logits)
    y_pool = _pool(u.reshape(B, S, POOL_WIDTH), pool_w[0].astype(BF16), pool_scale)
    y_hgrn = _scan(*(a.reshape(B, S, HGRN_WIDTH) for a in (q, k, gl, v, gt)), gnorm_w)
    x2 = _wout(x1, y_pool.reshape(B * S, POOL_WIDTH), y_hgrn.reshape(B * S, HGRN_WIDTH),
               w_out_b, mod3)

    out, = _ffn(x2, norm3_w, mod3, 6, g2_b, u2_b, d2_b, fw, final_norm=True)
    return out.reshape(B, S, D)
```

```python
import functools

import jax
import jax.numpy as jnp
from jax import lax
from jax.experimental import pallas as pl
from jax.experimental.pallas import tpu as pltpu

F32 = jnp.float32
BF16 = jnp.bfloat16

D_MODEL = 2048
BATCH = 4
SEQ = 2048
POOL_WIDTH = 1024
POOL_WINDOWS = (2, 4, 8, 16)
POOL_GROUP_DIM = 256
HGRN_WIDTH = 1024
HGRN_HEADS = 8
HEAD_DIM = 128
IN_WIDTH = 5120
CHUNK = 64
SUBLANES = 8
BF16_SUBLANES = 16
WIN_SUB_ROWS = 256
FFN_SUB_ROWS = 256
D_FF = 5632
N_MOD = 9
EPS = 1e-6

VMEM_LIMIT = 60 * 1024 * 1024


def _silu(v):
    return v * jax.nn.sigmoid(v)


def _norm_modulate(x, nw, sh, sc):
    ms = jnp.mean(x * x, axis=-1, keepdims=True)
    y = x * lax.rsqrt(ms + EPS) * nw
    return y * (1.0 + sc) + sh


def _ada_kernel(c_ref, w_ref, b_ref, o_ref):
    ca = _silu(c_ref[...]).astype(BF16)
    o_ref[...] = jnp.dot(ca, w_ref[...].astype(BF16), preferred_element_type=F32) + b_ref[...]


def _ada(c_pad, w_ada, b_ada):
    n = w_ada.shape[1]
    tn = 1024
    return pl.pallas_call(
        _ada_kernel,
        grid=(n // tn,),
        in_specs=[
            pl.BlockSpec((8, D_MODEL), lambda j: (0, 0)),
            pl.BlockSpec((D_MODEL, tn), lambda j: (0, j)),
            pl.BlockSpec((1, tn), lambda j: (0, j)),
        ],
        out_specs=pl.BlockSpec((8, tn), lambda j: (0, j)),
        out_shape=jax.ShapeDtypeStruct((8, n), F32),
        compiler_params=pltpu.CompilerParams(
            dimension_semantics=("arbitrary",), vmem_limit_bytes=VMEM_LIMIT),
        name="ada_mod",
    )(c_pad, w_ada, b_ada)


def _ffn_kernel(*refs, nj, final_norm, n_cast):
    x_ref, nw_ref, sh_ref, sc_ref, gt_ref, wg_ref, wu_ref, wd_ref, fw_ref = refs[:9]
    cast_in = refs[9:9 + n_cast]
    o_ref = refs[9 + n_cast]
    cast_out = refs[10 + n_cast:10 + 2 * n_cast]
    h_scr, = refs[10 + 2 * n_cast:]
    j = pl.program_id(1)
    tm = x_ref.shape[0]
    subs = [pl.ds(r, FFN_SUB_ROWS) for r in range(0, tm, FFN_SUB_ROWS)]

    for w_ref, wb_ref in zip(cast_in, cast_out):
        wb_ref[...] = w_ref[...].astype(BF16)

    def partial_ffn(rows):
        h = h_scr[rows, :]
        g = jnp.dot(h, wg_ref[...], preferred_element_type=F32)
        u = jnp.dot(h, wu_ref[...], preferred_element_type=F32)
        a = (_silu(g) * u).astype(BF16)
        return jnp.dot(a, wd_ref[...], preferred_element_type=F32)

    @pl.when(j == 0)
    def _():
        for rows in subs:
            h = _norm_modulate(x_ref[rows, :], nw_ref[...], sh_ref[0], sc_ref[0])
            h_scr[rows, :] = h.astype(BF16)
            o_ref[rows, :] = partial_ffn(rows)

    @pl.when((j > 0) & (j < nj - 1))
    def _():
        for rows in subs:
            o_ref[rows, :] += partial_ffn(rows)

    @pl.when(j == nj - 1)
    def _():
        for rows in subs:
            out = x_ref[rows, :] + 0.5 * gt_ref[0] * (o_ref[rows, :] + partial_ffn(rows))
            if final_norm:
                ms = jnp.mean(out * out, axis=-1, keepdims=True)
                out = out * lax.rsqrt(ms + EPS) * fw_ref[...]
            o_ref[rows, :] = out


def _ffn(x2d, nw, mod3, mod_base, wg, wu, wd, fw, *, final_norm, cast=()):
    m = x2d.shape[0]
    tm, tf = 1024, 512
    ni, nj = m // tm, D_FF // tf
    tiles_per_batch = SEQ // tm

    def mod_map(k):
        return lambda i, j: ((i // tiles_per_batch) * N_MOD + mod_base + k, 0, 0)

    cast_specs = []
    for w in cast:
        rows = w.shape[0]
        nblk = max(n for n in range(1, ni * nj + 1) if rows % (BF16_SUBLANES * n) == 0)
        cast_specs.append(pl.BlockSpec(
            (rows // nblk, w.shape[1]),
            lambda i, j, nblk=nblk: (jnp.minimum(i * nj + j, nblk - 1), 0)))

    kern = functools.partial(_ffn_kernel, nj=nj, final_norm=final_norm, n_cast=len(cast))
    res = pl.pallas_call(
        kern,
        grid=(ni, nj),
        in_specs=[
            pl.BlockSpec((tm, D_MODEL), lambda i, j: (i, 0)),
            pl.BlockSpec((1, D_MODEL), lambda i, j: (0, 0)),
            pl.BlockSpec((1, 1, D_MODEL), mod_map(0)),
            pl.BlockSpec((1, 1, D_MODEL), mod_map(1)),
            pl.BlockSpec((1, 1, D_MODEL), mod_map(2)),
            pl.BlockSpec((D_MODEL, tf), lambda i, j: (0, j)),
            pl.BlockSpec((D_MODEL, tf), lambda i, j: (0, j)),
            pl.BlockSpec((tf, D_MODEL), lambda i, j: (j, 0)),
            pl.BlockSpec((1, D_MODEL), lambda i, j: (0, 0)),
        ] + cast_specs,
        out_specs=[pl.BlockSpec((tm, D_MODEL), lambda i, j: (i, 0))] + cast_specs,
        out_shape=[jax.ShapeDtypeStruct((m, D_MODEL), F32)]
        + [jax.ShapeDtypeStruct(w.shape, BF16) for w in cast],
        scratch_shapes=[pltpu.VMEM((tm, D_MODEL), BF16)],
        compiler_params=pltpu.CompilerParams(
            dimension_semantics=("arbitrary", "arbitrary"), vmem_limit_bytes=VMEM_LIMIT),
        name="ffn_final" if final_norm else "ffn",
    )(x2d, nw, mod3, mod3, mod3, wg, wu, wd, fw, *cast)
    return res


def _win_kernel(x_ref, nw_ref, sh_ref, sc_ref, w_ref, lbl_ref,
                u_ref, q_ref, k_ref, gl_ref, v_ref, gt_ref, h_scr):
    j = pl.program_id(1)
    tm = x_ref.shape[0]
    subs = [pl.ds(r, WIN_SUB_ROWS) for r in range(0, tm, WIN_SUB_ROWS)]

    def project(rows):
        return jnp.dot(h_scr[rows, :], w_ref[...], preferred_element_type=F32)

    @pl.when(j == 0)
    def _():
        for rows in subs:
            h = _norm_modulate(x_ref[rows, :], nw_ref[...], sh_ref[0], sc_ref[0])
            h_scr[rows, :] = h.astype(BF16)
            u_ref[rows, :] = project(rows)

    @pl.when(j == 1)
    def _():
        for rows in subs:
            q_ref[rows, :] = _silu(project(rows)).astype(BF16)

    @pl.when(j == 2)
    def _():
        lbl = lbl_ref[...]
        pe = jnp.exp(lbl - jnp.max(lbl, axis=0, keepdims=True))
        p = pe / jnp.sum(pe, axis=0, keepdims=True)
        lb = (p[0:1, :] + p[1:2, :]) - p[0:1, :]
        for rows in subs:
            forget = lb + (1.0 - lb) * jax.nn.sigmoid(project(rows))
            k_ref[rows, :] = (1.0 - forget).astype(BF16)
            gl_ref[rows, :] = jnp.log(forget)

    @pl.when(j == 3)
    def _():
        for rows in subs:
            v_ref[rows, :] = project(rows).astype(BF16)

    @pl.when(j == 4)
    def _():
        for rows in subs:
            gt_ref[rows, :] = _silu(project(rows)).astype(BF16)


def _win(x2d, nw, mod3, w_in, lb_logits):
    m = x2d.shape[0]
    tm, tn = 512, 1024
    assert IN_WIDTH == 5 * tn and POOL_WIDTH == tn and HGRN_WIDTH == tn
    tiles_per_batch = SEQ // tm

    def mod_map(k):
        return lambda i, j: ((i // tiles_per_batch) * N_MOD + 3 + k, 0, 0)

    seg = pl.BlockSpec((tm, tn), lambda i, j: (i, 0))
    return pl.pallas_call(
        _win_kernel,
        grid=(m // tm, IN_WIDTH // tn),
        in_specs=[
            pl.BlockSpec((tm, D_MODEL), lambda i, j: (i, 0)),
            pl.BlockSpec((1, D_MODEL), lambda i, j: (0, 0)),
            pl.BlockSpec((1, 1, D_MODEL), mod_map(0)),
            pl.BlockSpec((1, 1, D_MODEL), mod_map(1)),
            pl.BlockSpec((D_MODEL, tn), lambda i, j: (0, j)),
            pl.BlockSpec((2, tn), lambda i, j: (0, 0)),
        ],
        out_specs=[seg] * 6,
        out_shape=[
            jax.ShapeDtypeStruct((m, tn), F32),
            jax.ShapeDtypeStruct((m, tn), BF16),
            jax.ShapeDtypeStruct((m, tn), BF16),
            jax.ShapeDtypeStruct((m, tn), F32),
            jax.ShapeDtypeStruct((m, tn), BF16),
            jax.ShapeDtypeStruct((m, tn), BF16),
        ],
        scratch_shapes=[pltpu.VMEM((tm, D_MODEL), BF16)],
        compiler_params=pltpu.CompilerParams(
            dimension_semantics=("arbitrary", "arbitrary"), vmem_limit_bytes=VMEM_LIMIT),
        name="mix_in",
    )(x2d, nw, mod3, mod3, w_in, lb_logits)


def _pool_kernel(u_ref, pw_ref, ps_ref, o_ref):
    g = pl.program_id(1)
    u = u_ref[0]
    t = lax.broadcasted_iota(jnp.int32, (SEQ, 1), 0)

    def shifted(v, k):
        return jnp.where(t >= k, pltpu.roll(v, k, axis=0), 0.0)

    s2 = u + shifted(u, 1)
    s4 = s2 + shifted(s2, 2)
    s8 = s4 + shifted(s4, 4)
    s16 = s8 + shifted(s8, 8)
    win = jnp.where(g == 0, s2, jnp.where(g == 1, s4, jnp.where(g == 2, s8, s16)))
    cnt = jnp.minimum(t + 1, 2 << g).astype(F32)
    pooled = win / cnt - u
    y = jnp.dot(pooled.astype(BF16), pw_ref[0], preferred_element_type=F32) * ps_ref[...]
    o_ref[0] = y.astype(BF16)


def _pool(u3, pool_w, pool_scale):
    assert POOL_WINDOWS == (2, 4, 8, 16)
    return pl.pallas_call(
        _pool_kernel,
        grid=(BATCH, len(POOL_WINDOWS)),
        in_specs=[
            pl.BlockSpec((1, SEQ, POOL_GROUP_DIM), lambda b, g: (b, 0, g)),
            pl.BlockSpec((1, POOL_GROUP_DIM, POOL_GROUP_DIM), lambda b, g: (g, 0, 0)),
            pl.BlockSpec((1, POOL_GROUP_DIM), lambda b, g: (0, g)),
        ],
        out_specs=pl.BlockSpec((1, SEQ, POOL_GROUP_DIM), lambda b, g: (b, 0, g)),
        out_shape=jax.ShapeDtypeStruct((BATCH, SEQ, POOL_WIDTH), BF16),
        compiler_params=pltpu.CompilerParams(
            dimension_semantics=("arbitrary", "arbitrary"), vmem_limit_bytes=VMEM_LIMIT),
        name="pool",
    )(u3, pool_w, pool_scale)


def _dot_nt(a, b):
    return lax.dot_general(a, b, (((1,), (1,)), ((), ())), preferred_element_type=F32)


def _dot_tn(a, b):
    return lax.dot_general(a, b, (((0,), (0,)), ((), ())), preferred_element_type=F32)


LEVEL_SIZES = (32, 16, 8, 4, 2, 1)


def _row_bcast(v, r):
    return jnp.broadcast_to(v[r:r + 1, :], v.shape)


def _level_exponents(g):
    n = len(g)
    r8 = lax.broadcasted_iota(jnp.int32, (SUBLANES, 1), 0)
    p = list(g)
    for s in (1, 2, 4):
        p = [pi + jnp.where(r8 >= s, pltpu.roll(pi, s, axis=0), 0.0) for pi in p]
    off = [None, _row_bcast(p[0], SUBLANES - 1)]
    for i in range(1, n):
        off.append(off[i] + _row_bcast(p[i], SUBLANES - 1))
    b = [p[0]] + [p[i] + off[i] for i in range(1, n)]

    exps = []
    for hs in LEVEL_SIZES:
        if hs >= SUBLANES:
            w = hs // SUBLANES
            lvl = []
            for i in range(n):
                ref = off[(i // (2 * w)) * 2 * w + w]
                lvl.append(b[i] - ref if (i // w) % 2 == 1 else ref - b[i])
        elif hs == 1:
            lvl = [jnp.where(r8 % 2 == 1, gi, 0.0) for gi in g]
        else:
            lvl = []
            for pi in p:
                ref = _row_bcast(pi, hs - 1)
                for blk in range(1, SUBLANES // (2 * hs)):
                    ref = jnp.where(r8 >= blk * 2 * hs, _row_bcast(pi, blk * 2 * hs + hs - 1), ref)
                lvl.append(-jnp.abs(pi - ref))
        exps.append(lvl)
    return exps, b, off[n]


def _scan_kernel(q_ref, k_ref, gl_ref, v_ref, gt_ref, gw_ref, o_ref, st_scr):
    gw = gw_ref[...]
    nt = CHUNK // SUBLANES
    ti = lax.broadcasted_iota(jnp.int32, (CHUNK, CHUNK), 0)
    si = lax.broadcasted_iota(jnp.int32, (CHUNK, CHUNK), 1)
    owner = jnp.where(ti == si, len(LEVEL_SIZES), -1)
    for idx, hs in enumerate(LEVEL_SIZES):
        owner = jnp.where(((ti // (2 * hs)) == (si // (2 * hs))) & ((ti // hs) > (si // hs)),
                          idx, owner)
    st_scr[...] = jnp.zeros_like(st_scr)

    def tiles(a):
        return [a[i * SUBLANES:(i + 1) * SUBLANES, :] for i in range(nt)]

    def scaled(a_tiles, e_tiles):
        return jnp.concatenate([a * jnp.exp(e) for a, e in zip(a_tiles, e_tiles)],
                               axis=0).astype(BF16)

    def body(c, carry):
        r0 = pl.multiple_of(c * CHUNK, CHUNK)
        qb = q_ref[0, pl.ds(r0, CHUNK), :]
        kb = k_ref[0, pl.ds(r0, CHUNK), :]
        vb = v_ref[0, pl.ds(r0, CHUNK), :]
        q = tiles(qb.astype(F32))
        k = tiles(kb.astype(F32))
        exps, b, b_last = _level_exponents(tiles(gl_ref[0, pl.ds(r0, CHUNK), :]))

        attn = jnp.where(owner == len(LEVEL_SIZES), _dot_nt(qb, kb), 0.0)
        for idx, e in enumerate(exps):
            attn = jnp.where(owner == idx, _dot_nt(scaled(q, e), scaled(k, e)), attn)

        st_prev = st_scr[...]
        o = (jnp.dot(attn.astype(BF16), vb, preferred_element_type=F32)
             + _dot_nt(scaled(q, b), st_prev.astype(BF16)))
        kd = scaled(k, [b_last - bi for bi in b])
        st_scr[...] = jnp.exp(b_last[0:1, :]) * st_prev + _dot_tn(vb, kd)

        ms = jnp.mean(o * o, axis=-1, keepdims=True)
        y = o * lax.rsqrt(ms + EPS) * gw * gt_ref[0, pl.ds(r0, CHUNK), :].astype(F32)
        o_ref[0, pl.ds(r0, CHUNK), :] = y.astype(BF16)
        return carry

    lax.fori_loop(0, SEQ // CHUNK, body, 0, unroll=4)


def _scan(q3, k3, gl3, v3, gt3, gnorm_w):
    blk = pl.BlockSpec((1, SEQ, HEAD_DIM), lambda b, h: (b, 0, h))
    return pl.pallas_call(
        _scan_kernel,
        grid=(BATCH, HGRN_HEADS),
        in_specs=[blk, blk, blk, blk, blk, pl.BlockSpec((1, HEAD_DIM), lambda b, h: (0, 0))],
        out_specs=blk,
        out_shape=jax.ShapeDtypeStruct((BATCH, SEQ, HGRN_WIDTH), BF16),
        scratch_shapes=[pltpu.VMEM((HEAD_DIM, HEAD_DIM), F32)],
        compiler_params=pltpu.CompilerParams(
            dimension_semantics=("arbitrary", "arbitrary"), vmem_limit_bytes=VMEM_LIMIT),
        name="hgrn_scan",
    )(q3, k3, gl3, v3, gt3, gnorm_w)


def _wout_kernel(x_ref, yp_ref, yh_ref, wp_ref, wh_ref, gt_ref, o_ref):
    mix = (jnp.dot(yp_ref[...], wp_ref[...], preferred_element_type=F32)
           + jnp.dot(yh_ref[...], wh_ref[...], preferred_element_type=F32))
    o_ref[...] = x_ref[...] + gt_ref[0] * mix


def _wout(x2d, yp, yh, w_out, mod3):
    m = x2d.shape[0]
    tm = 512
    tiles_per_batch = SEQ // tm
    return pl.pallas_call(
        _wout_kernel,
        grid=(m // tm,),
        in_specs=[
            pl.BlockSpec((tm, D_MODEL), lambda i: (i, 0)),
            pl.BlockSpec((tm, POOL_WIDTH), lambda i: (i, 0)),
            pl.BlockSpec((tm, HGRN_WIDTH), lambda i: (i, 0)),
            pl.BlockSpec((POOL_WIDTH, D_MODEL), lambda i: (0, 0)),
            pl.BlockSpec((HGRN_WIDTH, D_MODEL), lambda i: (1, 0)),
            pl.BlockSpec((1, 1, D_MODEL), lambda i: ((i // tiles_per_batch) * N_MOD + 5, 0, 0)),
        ],
        out_specs=pl.BlockSpec((tm, D_MODEL), lambda i: (i, 0)),
        out_shape=jax.ShapeDtypeStruct((m, D_MODEL), F32),
        compiler_params=pltpu.CompilerParams(
            dimension_semantics=("arbitrary",), vmem_limit_bytes=VMEM_LIMIT),
        name="mix_out",
    )(x2d, yp, yh, w_out, w_out, mod3)


def kernel(x, c, w_ada, b_ada, norm1_w, ffn1_gate, ffn1_up, ffn1_down, norm2_w, w_in, pool_w,
           pool_scale, lb_logits, gnorm_w, w_out, norm3_w, ffn2_gate, ffn2_up, ffn2_down,
           final_norm_w):
    B, S, D = x.shape
    assert (B, S, D) == (BATCH, SEQ, D_MODEL) and w_ada.shape[0] == 1
    assert lb_logits.shape == (2, HGRN_WIDTH)

    c_pad = jnp.pad(c, ((0, 8 - B), (0, 0)))
    mod = _ada(c_pad, w_ada[0], b_ada)[:B]
    mod3 = mod.reshape(B * N_MOD, 1, D)

    x2d = x.reshape(B * S, D)
    fw = final_norm_w.reshape(1, D)

    x1, w_in_b, w_out_b, g2_b, u2_b, d2_b = _ffn(
        x2d, norm1_w, mod3, 0, ffn1_gate[0].astype(BF16), ffn1_up[0].astype(BF16),
        ffn1_down[0].astype(BF16), fw, final_norm=False,
        cast=(w_in[0], w_out[0], ffn2_gate[0], ffn2_up[0], ffn2_down[0]))

    u, q, k, gl, v, gt = _win(x1, norm2_w, mod3, w_in_b, lb_logits)
    y_pool = _pool(u.reshape(B, S, POOL_WIDTH), pool_w[0].astype(BF16), pool_scale)
    y_hgrn = _scan(*(a.reshape(B, S, HGRN_WIDTH) for a in (q, k, gl, v, gt)), gnorm_w)
    x2 = _wout(x1, y_pool.reshape(B * S, POOL_WIDTH), y_hgrn.reshape(B * S, HGRN_WIDTH),
               w_out_b, mod3)

    out, = _ffn(x2, norm3_w, mod3, 6, g2_b, u2_b, d2_b, fw, final_norm=True)
    return out.reshape(B, S, D)
```

```python
import functools

import jax
import jax.numpy as jnp
from jax import lax
from jax.experimental import pallas as pl
from jax.experimental.pallas import tpu as pltpu

F32 = jnp.float32
BF16 = jnp.bfloat16

D_MODEL = 2048
BATCH = 4
SEQ = 2048
POOL_WIDTH = 1024
POOL_WINDOWS = (2, 4, 8, 16)
POOL_GROUP_DIM = 256
HGRN_WIDTH = 1024
HGRN_HEADS = 8
HEAD_DIM = 128
IN_WIDTH = 5120
CHUNK = 64
SUBLANES = 8
BF16_SUBLANES = 16
WIN_SUB_ROWS = 256
FFN_SUB_ROWS = 256
POOL_HALO = 16
D_FF = 5632
N_MOD = 9
EPS = 1e-6

VMEM_LIMIT = 60 * 1024 * 1024


def _silu(v):
    return v * jax.nn.sigmoid(v)


def _norm_modulate(x, nw, sh, sc):
    ms = jnp.mean(x * x, axis=-1, keepdims=True)
    y = x * lax.rsqrt(ms + EPS) * nw
    return y * (1.0 + sc) + sh


def _ada_kernel(c_ref, w_ref, b_ref, o_ref):
    ca = _silu(c_ref[...]).astype(BF16)
    o_ref[...] = jnp.dot(ca, w_ref[...].astype(BF16), preferred_element_type=F32) + b_ref[...]


def _ada(c_pad, w_ada, b_ada):
    n = w_ada.shape[1]
    tn = 1024
    return pl.pallas_call(
        _ada_kernel,
        grid=(n // tn,),
        in_specs=[
            pl.BlockSpec((8, D_MODEL), lambda j: (0, 0)),
            pl.BlockSpec((D_MODEL, tn), lambda j: (0, j)),
            pl.BlockSpec((1, tn), lambda j: (0, j)),
        ],
        out_specs=pl.BlockSpec((8, tn), lambda j: (0, j)),
        out_shape=jax.ShapeDtypeStruct((8, n), F32),
        compiler_params=pltpu.CompilerParams(
            dimension_semantics=("arbitrary",), vmem_limit_bytes=VMEM_LIMIT),
        name="ada_mod",
    )(c_pad, w_ada, b_ada)


def _ffn_kernel(*refs, nj, final_norm, n_cast):
    x_ref, nw_ref, sh_ref, sc_ref, gt_ref, wg_ref, wu_ref, wd_ref, fw_ref = refs[:9]
    cast_in = refs[9:9 + n_cast]
    o_ref = refs[9 + n_cast]
    cast_out = refs[10 + n_cast:10 + 2 * n_cast]
    h_scr, = refs[10 + 2 * n_cast:]
    j = pl.program_id(1)
    tm = x_ref.shape[0]
    subs = [pl.ds(r, FFN_SUB_ROWS) for r in range(0, tm, FFN_SUB_ROWS)]

    for w_ref, wb_ref in zip(cast_in, cast_out):
        wb_ref[...] = w_ref[...].astype(BF16)

    def partial_ffn(rows):
        h = h_scr[rows, :]
        g = jnp.dot(h, wg_ref[...], preferred_element_type=F32)
        u = jnp.dot(h, wu_ref[...], preferred_element_type=F32)
        a = (_silu(g) * u).astype(BF16)
        return jnp.dot(a, wd_ref[...], preferred_element_type=F32)

    @pl.when(j == 0)
    def _():
        for rows in subs:
            h = _norm_modulate(x_ref[rows, :], nw_ref[...], sh_ref[0], sc_ref[0])
            h_scr[rows, :] = h.astype(BF16)
            o_ref[rows, :] = partial_ffn(rows)

    @pl.when((j > 0) & (j < nj - 1))
    def _():
        for rows in subs:
            o_ref[rows, :] += partial_ffn(rows)

    @pl.when(j == nj - 1)
    def _():
        for rows in subs:
            out = x_ref[rows, :] + 0.5 * gt_ref[0] * (o_ref[rows, :] + partial_ffn(rows))
            if final_norm:
                ms = jnp.mean(out * out, axis=-1, keepdims=True)
                out = out * lax.rsqrt(ms + EPS) * fw_ref[...]
            o_ref[rows, :] = out


def _ffn(x2d, nw, mod3, mod_base, wg, wu, wd, fw, *, final_norm, cast=()):
    m = x2d.shape[0]
    tm, tf = 1024, 512
    ni, nj = m // tm, D_FF // tf
    tiles_per_batch = SEQ // tm

    def mod_map(k):
        return lambda i, j: ((i // tiles_per_batch) * N_MOD + mod_base + k, 0, 0)

    cast_specs = []
    for w in cast:
        rows = w.shape[0]
        nblk = max(n for n in range(1, ni * nj + 1) if rows % (BF16_SUBLANES * n) == 0)
        cast_specs.append(pl.BlockSpec(
            (rows // nblk, w.shape[1]),
            lambda i, j, nblk=nblk: (jnp.minimum(i * nj + j, nblk - 1), 0)))

    kern = functools.partial(_ffn_kernel, nj=nj, final_norm=final_norm, n_cast=len(cast))
    res = pl.pallas_call(
        kern,
        grid=(ni, nj),
        in_specs=[
            pl.BlockSpec((tm, D_MODEL), lambda i, j: (i, 0)),
            pl.BlockSpec((1, D_MODEL), lambda i, j: (0, 0)),
            pl.BlockSpec((1, 1, D_MODEL), mod_map(0)),
            pl.BlockSpec((1, 1, D_MODEL), mod_map(1)),
            pl.BlockSpec((1, 1, D_MODEL), mod_map(2)),
            pl.BlockSpec((D_MODEL, tf), lambda i, j: (0, j)),
            pl.BlockSpec((D_MODEL, tf), lambda i, j: (0, j)),
            pl.BlockSpec((tf, D_MODEL), lambda i, j: (j, 0)),
            pl.BlockSpec((1, D_MODEL), lambda i, j: (0, 0)),
        ] + cast_specs,
        out_specs=[pl.BlockSpec((tm, D_MODEL), lambda i, j: (i, 0))] + cast_specs,
        out_shape=[jax.ShapeDtypeStruct((m, D_MODEL), F32)]
        + [jax.ShapeDtypeStruct(w.shape, BF16) for w in cast],
        scratch_shapes=[pltpu.VMEM((tm, D_MODEL), BF16)],
        compiler_params=pltpu.CompilerParams(
            dimension_semantics=("arbitrary", "arbitrary"), vmem_limit_bytes=VMEM_LIMIT),
        name="ffn_final" if final_norm else "ffn",
    )(x2d, nw, mod3, mod3, mod3, wg, wu, wd, fw, *cast)
    return res


def _pool_group(ext, u, t_abs, window, pw, ps):
    s, k = ext, 1
    while k < window:
        s = s + pltpu.roll(s, k, axis=0)
        k *= 2
    cnt = jnp.minimum(t_abs + 1, window).astype(F32)
    pooled = s[POOL_HALO:, :] * (1.0 / cnt) - u
    return jnp.dot(pooled.astype(BF16), pw, preferred_element_type=F32) * ps


def _win_kernel(x_ref, nw_ref, sh_ref, sc_ref, w_ref, lbl_ref, pw_ref, ps_ref,
                yp_ref, q_ref, k_ref, gl_ref, v_ref, gt_ref, tail_scr, *, tiles_per_batch):
    tm, tn = yp_ref.shape
    cg = POOL_GROUP_DIM
    lbl = lbl_ref[...]
    pe = jnp.exp(lbl - jnp.max(lbl, axis=0, keepdims=True))
    p = pe / jnp.sum(pe, axis=0, keepdims=True)
    lb = (p[0:1, :] + p[1:2, :]) - p[0:1, :]

    tile_in_seq = pl.program_id(0) % tiles_per_batch

    @pl.when(tile_in_seq == 0)
    def _():
        tail_scr[...] = jnp.zeros_like(tail_scr)

    tail = tail_scr[...]
    row = lax.broadcasted_iota(jnp.int32, (WIN_SUB_ROWS, 1), 0)

    for r in range(0, tm, WIN_SUB_ROWS):
        rows = pl.ds(r, WIN_SUB_ROWS)
        h = _norm_modulate(x_ref[rows, :], nw_ref[...], sh_ref[0], sc_ref[0]).astype(BF16)

        def project(seg):
            return jnp.dot(h, w_ref[:, seg * tn:(seg + 1) * tn], preferred_element_type=F32)

        u = project(0)
        ext = jnp.concatenate([tail, u], axis=0)
        tail = u[WIN_SUB_ROWS - POOL_HALO:, :]
        t_abs = tile_in_seq * tm + r + row
        for g, window in enumerate(POOL_WINDOWS):
            cols = slice(g * cg, (g + 1) * cg)
            y = _pool_group(ext[:, cols], u[:, cols], t_abs, window, pw_ref[g], ps_ref[:, cols])
            yp_ref[rows, cols] = y.astype(BF16)
        q_ref[rows, :] = _silu(project(1)).astype(BF16)
        forget = lb + (1.0 - lb) * jax.nn.sigmoid(project(2))
        k_ref[rows, :] = (1.0 - forget).astype(BF16)
        gl_ref[rows, :] = jnp.log(forget)
        v_ref[rows, :] = project(3).astype(BF16)
        gt_ref[rows, :] = _silu(project(4)).astype(BF16)
    tail_scr[...] = tail


def _win(x2d, nw, mod3, w_in, lb_logits, pool_w, pool_scale):
    m = x2d.shape[0]
    tm, tn = 512, 1024
    assert IN_WIDTH == 5 * tn and POOL_WIDTH == tn and HGRN_WIDTH == tn
    assert max(POOL_WINDOWS) <= POOL_HALO and tn == len(POOL_WINDOWS) * POOL_GROUP_DIM
    tiles_per_batch = SEQ // tm

    def mod_map(k):
        return lambda i: ((i // tiles_per_batch) * N_MOD + 3 + k, 0, 0)

    seg = pl.BlockSpec((tm, tn), lambda i: (i, 0))
    return pl.pallas_call(
        functools.partial(_win_kernel, tiles_per_batch=tiles_per_batch),
        grid=(m // tm,),
        in_specs=[
            pl.BlockSpec((tm, D_MODEL), lambda i: (i, 0)),
            pl.BlockSpec((1, D_MODEL), lambda i: (0, 0)),
            pl.BlockSpec((1, 1, D_MODEL), mod_map(0)),
            pl.BlockSpec((1, 1, D_MODEL), mod_map(1)),
            pl.BlockSpec((D_MODEL, IN_WIDTH), lambda i: (0, 0), pipeline_mode=pl.Buffered(1)),
            pl.BlockSpec((2, tn), lambda i: (0, 0)),
            pl.BlockSpec(pool_w.shape, lambda i: (0, 0, 0)),
            pl.BlockSpec((1, tn), lambda i: (0, 0)),
        ],
        out_specs=[seg] * 6,
        out_shape=[
            jax.ShapeDtypeStruct((m, tn), BF16),
            jax.ShapeDtypeStruct((m, tn), BF16),
            jax.ShapeDtypeStruct((m, tn), BF16),
            jax.ShapeDtypeStruct((m, tn), F32),
            jax.ShapeDtypeStruct((m, tn), BF16),
            jax.ShapeDtypeStruct((m, tn), BF16),
        ],
        scratch_shapes=[pltpu.VMEM((POOL_HALO, tn), F32)],
        compiler_params=pltpu.CompilerParams(
            dimension_semantics=("arbitrary",), vmem_limit_bytes=VMEM_LIMIT),
        name="mix_in",
    )(x2d, nw, mod3, mod3, w_in, lb_logits, pool_w, pool_scale)


def _dot_nt(a, b):
    return lax.dot_general(a, b, (((1,), (1,)), ((), ())), preferred_element_type=F32)


def _dot_tn(a, b):
    return lax.dot_general(a, b, (((0,), (0,)), ((), ())), preferred_element_type=F32)


LEVEL_SIZES = (32, 16, 8, 4, 2, 1)


def _row_bcast(v, r):
    return jnp.broadcast_to(v[r:r + 1, :], v.shape)


def _level_exponents(g):
    n = len(g)
    r8 = lax.broadcasted_iota(jnp.int32, (SUBLANES, 1), 0)
    p = list(g)
    for s in (1, 2, 4):
        p = [pi + jnp.where(r8 >= s, pltpu.roll(pi, s, axis=0), 0.0) for pi in p]
    off = [None, _row_bcast(p[0], SUBLANES - 1)]
    for i in range(1, n):
        off.append(off[i] + _row_bcast(p[i], SUBLANES - 1))
    b = [p[0]] + [p[i] + off[i] for i in range(1, n)]

    exps = []
    for hs in LEVEL_SIZES:
        if hs >= SUBLANES:
            w = hs // SUBLANES
            lvl = []
            for i in range(n):
                ref = off[(i // (2 * w)) * 2 * w + w]
                lvl.append(b[i] - ref if (i // w) % 2 == 1 else ref - b[i])
        elif hs == 1:
            lvl = [jnp.where(r8 % 2 == 1, gi, 0.0) for gi in g]
        else:
            lvl = []
            for pi in p:
                ref = _row_bcast(pi, hs - 1)
                for blk in range(1, SUBLANES // (2 * hs)):
                    ref = jnp.where(r8 >= blk * 2 * hs, _row_bcast(pi, blk * 2 * hs + hs - 1), ref)
                lvl.append(-jnp.abs(pi - ref))
        exps.append(lvl)
    return exps, b, off[n]


def _scan_kernel(q_ref, k_ref, gl_ref, v_ref, gt_ref, gw_ref, o_ref, st_scr):
    gw = gw_ref[...]
    nt = CHUNK // SUBLANES
    ti = lax.broadcasted_iota(jnp.int32, (CHUNK, CHUNK), 0)
    si = lax.broadcasted_iota(jnp.int32, (CHUNK, CHUNK), 1)
    owner = jnp.where(ti == si, len(LEVEL_SIZES), -1)
    for idx, hs in enumerate(LEVEL_SIZES):
        owner = jnp.where(((ti // (2 * hs)) == (si // (2 * hs))) & ((ti // hs) > (si // hs)),
                          idx, owner)
    st_scr[...] = jnp.zeros_like(st_scr)

    def tiles(a):
        return [a[i * SUBLANES:(i + 1) * SUBLANES, :] for i in range(nt)]

    def scaled(a_tiles, e_tiles):
        return jnp.concatenate([a * jnp.exp(e) for a, e in zip(a_tiles, e_tiles)],
                               axis=0).astype(BF16)

    def body(c, carry):
        r0 = pl.multiple_of(c * CHUNK, CHUNK)
        qb = q_ref[0, pl.ds(r0, CHUNK), :]
        kb = k_ref[0, pl.ds(r0, CHUNK), :]
        vb = v_ref[0, pl.ds(r0, CHUNK), :]
        q = tiles(qb.astype(F32))
        k = tiles(kb.astype(F32))
        exps, b, b_last = _level_exponents(tiles(gl_ref[0, pl.ds(r0, CHUNK), :]))

        attn = jnp.where(owner == len(LEVEL_SIZES), _dot_nt(qb, kb), 0.0)
        for idx, e in enumerate(exps):
            attn = jnp.where(owner == idx, _dot_nt(scaled(q, e), scaled(k, e)), attn)

        st_prev = st_scr[...]
        o = (jnp.dot(attn.astype(BF16), vb, preferred_element_type=F32)
             + _dot_nt(scaled(q, b), st_prev.astype(BF16)))
        kd = scaled(k, [b_last - bi for bi in b])
        st_scr[...] = jnp.exp(b_last[0:1, :]) * st_prev + _dot_tn(vb, kd)

        ms = jnp.mean(o * o, axis=-1, keepdims=True)
        y = o * lax.rsqrt(ms + EPS) * gw * gt_ref[0, pl.ds(r0, CHUNK), :].astype(F32)
        o_ref[0, pl.ds(r0, CHUNK), :] = y.astype(BF16)
        return carry

    lax.fori_loop(0, SEQ // CHUNK, body, 0, unroll=4)


def _scan(q3, k3, gl3, v3, gt3, gnorm_w):
    blk = pl.BlockSpec((1, SEQ, HEAD_DIM), lambda b, h: (b, 0, h))
    return pl.pallas_call(
        _scan_kernel,
        grid=(BATCH, HGRN_HEADS),
        in_specs=[blk, blk, blk, blk, blk, pl.BlockSpec((1, HEAD_DIM), lambda b, h: (0, 0))],
        out_specs=blk,
        out_shape=jax.ShapeDtypeStruct((BATCH, SEQ, HGRN_WIDTH), BF16),
        scratch_shapes=[pltpu.VMEM((HEAD_DIM, HEAD_DIM), F32)],
        compiler_params=pltpu.CompilerParams(
            dimension_semantics=("arbitrary", "arbitrary"), vmem_limit_bytes=VMEM_LIMIT),
        name="hgrn_scan",
    )(q3, k3, gl3, v3, gt3, gnorm_w)


def _wout_kernel(x_ref, yp_ref, yh_ref, wp_ref, wh_ref, gt_ref, o_ref):
    mix = (jnp.dot(yp_ref[...], wp_ref[...], preferred_element_type=F32)
           + jnp.dot(yh_ref[...], wh_ref[...], preferred_element_type=F32))
    o_ref[...] = x_ref[...] + gt_ref[0] * mix


def _wout(x2d, yp, yh, w_out, mod3):
    m = x2d.shape[0]
    tm = 512
    tiles_per_batch = SEQ // tm
    return pl.pallas_call(
        _wout_kernel,
        grid=(m // tm,),
        in_specs=[
            pl.BlockSpec((tm, D_MODEL), lambda i: (i, 0)),
            pl.BlockSpec((tm, POOL_WIDTH), lambda i: (i, 0)),
            pl.BlockSpec((tm, HGRN_WIDTH), lambda i: (i, 0)),
            pl.BlockSpec((POOL_WIDTH, D_MODEL), lambda i: (0, 0)),
            pl.BlockSpec((HGRN_WIDTH, D_MODEL), lambda i: (1, 0)),
            pl.BlockSpec((1, 1, D_MODEL), lambda i: ((i // tiles_per_batch) * N_MOD + 5, 0, 0)),
        ],
        out_specs=pl.BlockSpec((tm, D_MODEL), lambda i: (i, 0)),
        out_shape=jax.ShapeDtypeStruct((m, D_MODEL), F32),
        compiler_params=pltpu.CompilerParams(
            dimension_semantics=("arbitrary",), vmem_limit_bytes=VMEM_LIMIT),
        name="mix_out",
    )(x2d, yp, yh, w_out, w_out, mod3)


def kernel(x, c, w_ada, b_ada, norm1_w, ffn1_gate, ffn1_up, ffn1_down, norm2_w, w_in, pool_w,
           pool_scale, lb_logits, gnorm_w, w_out, norm3_w, ffn2_gate, ffn2_up, ffn2_down,
           final_norm_w):
    B, S, D = x.shape
    assert (B, S, D) == (BATCH, SEQ, D_MODEL) and w_ada.shape[0] == 1
    assert lb_logits.shape == (2, HGRN_WIDTH)

    c_pad = jnp.pad(c, ((0, 8 - B), (0, 0)))
    mod = _ada(c_pad, w_ada[0], b_ada)[:B]
    mod3 = mod.reshape(B * N_MOD, 1, D)

    x2d = x.reshape(B * S, D)
    fw = final_norm_w.reshape(1, D)

    x1, w_in_b, w_out_b, g2_b, u2_b, d2_b = _ffn(
        x2d, norm1_w, mod3, 0, ffn1_gate[0].astype(BF16), ffn1_up[0].astype(BF16),
        ffn1_down[0].astype(BF16), fw, final_norm=False,
        cast=(w_in[0], w_out[0], ffn2_gate[0], ffn2_up[0], ffn2_down[0]))

    y_pool, q, k, gl, v, gt = _win(x1, norm2_w, mod3, w_in_b, lb_logits,
                                   pool_w[0].astype(BF16), pool_scale)
    y_hgrn = _scan(*(a.reshape(B, S, HGRN_WIDTH) for a in (q, k, gl, v, gt)), gnorm_w)
    x2 = _wout(x1, y_pool, y_hgrn.reshape(B * S, HGRN_WIDTH), w_out_b, mod3)

    out, = _ffn(x2, norm3_w, mod3, 6, g2_b, u2_b, d2_b, fw, final_norm=True)
    return out.reshape(B, S, D)
```

```python
import functools

import jax
import jax.numpy as jnp
from jax import lax
from jax.experimental import pallas as pl
from jax.experimental.pallas import tpu as pltpu

F32 = jnp.float32
BF16 = jnp.bfloat16

D_MODEL = 2048
BATCH = 4
SEQ = 2048
POOL_WIDTH = 1024
POOL_WINDOWS = (2, 4, 8, 16)
POOL_GROUP_DIM = 256
HGRN_WIDTH = 1024
HGRN_HEADS = 8
HEAD_DIM = 128
IN_WIDTH = 5120
CHUNK = 64
SUBLANES = 8
BF16_SUBLANES = 16
WIN_SUB_ROWS = 256
FFN_SUB_ROWS = 256
POOL_HALO = 16
D_FF = 5632
N_MOD = 9
EPS = 1e-6

VMEM_LIMIT = 60 * 1024 * 1024


def _silu(v):
    return v * jax.nn.sigmoid(v)


def _norm_modulate(x, nw, sh, sc):
    ms = jnp.mean(x * x, axis=-1, keepdims=True)
    y = x * lax.rsqrt(ms + EPS) * nw
    return y * (1.0 + sc) + sh


def _ada_kernel(c_ref, w_ref, b_ref, o_ref):
    ca = _silu(c_ref[...]).astype(BF16)
    o_ref[...] = jnp.dot(ca, w_ref[...].astype(BF16), preferred_element_type=F32) + b_ref[...]


def _ada(c_pad, w_ada, b_ada):
    n = w_ada.shape[1]
    tn = 1024
    return pl.pallas_call(
        _ada_kernel,
        grid=(n // tn,),
        in_specs=[
            pl.BlockSpec((8, D_MODEL), lambda j: (0, 0)),
            pl.BlockSpec((D_MODEL, tn), lambda j: (0, j)),
            pl.BlockSpec((1, tn), lambda j: (0, j)),
        ],
        out_specs=pl.BlockSpec((8, tn), lambda j: (0, j)),
        out_shape=jax.ShapeDtypeStruct((8, n), F32),
        compiler_params=pltpu.CompilerParams(
            dimension_semantics=("arbitrary",), vmem_limit_bytes=VMEM_LIMIT),
        name="ada_mod",
    )(c_pad, w_ada, b_ada)


def _ffn_kernel(*refs, nj, final_norm, n_cast):
    x_ref, nw_ref, sh_ref, sc_ref, gt_ref, wg_ref, wu_ref, wd_ref, fw_ref = refs[:9]
    cast_in = refs[9:9 + n_cast]
    o_ref = refs[9 + n_cast]
    cast_out = refs[10 + n_cast:10 + 2 * n_cast]
    h_scr, = refs[10 + 2 * n_cast:]
    j = pl.program_id(1)
    tm = x_ref.shape[0]
    subs = [pl.ds(r, FFN_SUB_ROWS) for r in range(0, tm, FFN_SUB_ROWS)]

    for w_ref, wb_ref in zip(cast_in, cast_out):
        wb_ref[...] = w_ref[...].astype(BF16)

    def partial_ffn(rows):
        h = h_scr[rows, :]
        g = jnp.dot(h, wg_ref[...], preferred_element_type=F32)
        u = jnp.dot(h, wu_ref[...], preferred_element_type=F32)
        a = (_silu(g) * u).astype(BF16)
        return jnp.dot(a, wd_ref[...], preferred_element_type=F32)

    @pl.when(j == 0)
    def _():
        for rows in subs:
            h = _norm_modulate(x_ref[rows, :], nw_ref[...], sh_ref[0], sc_ref[0])
            h_scr[rows, :] = h.astype(BF16)
            o_ref[rows, :] = partial_ffn(rows)

    @pl.when((j > 0) & (j < nj - 1))
    def _():
        for rows in subs:
            o_ref[rows, :] += partial_ffn(rows)

    @pl.when(j == nj - 1)
    def _():
        for rows in subs:
            out = x_ref[rows, :] + 0.5 * gt_ref[0] * (o_ref[rows, :] + partial_ffn(rows))
            if final_norm:
                ms = jnp.mean(out * out, axis=-1, keepdims=True)
                out = out * lax.rsqrt(ms + EPS) * fw_ref[...]
            o_ref[rows, :] = out


def _ffn(x2d, nw, mod3, mod_base, wg, wu, wd, fw, *, final_norm, cast=()):
    m = x2d.shape[0]
    tm, tf = 1024, 512
    ni, nj = m // tm, D_FF // tf
    tiles_per_batch = SEQ // tm

    def mod_map(k):
        return lambda i, j: ((i // tiles_per_batch) * N_MOD + mod_base + k, 0, 0)

    cast_specs = []
    for w in cast:
        rows = w.shape[0]
        nblk = max(n for n in range(1, ni * nj + 1) if rows % (BF16_SUBLANES * n) == 0)
        cast_specs.append(pl.BlockSpec(
            (rows // nblk, w.shape[1]),
            lambda i, j, nblk=nblk: (jnp.minimum(i * nj + j, nblk - 1), 0)))

    kern = functools.partial(_ffn_kernel, nj=nj, final_norm=final_norm, n_cast=len(cast))
    res = pl.pallas_call(
        kern,
        grid=(ni, nj),
        in_specs=[
            pl.BlockSpec((tm, D_MODEL), lambda i, j: (i, 0)),
            pl.BlockSpec((1, D_MODEL), lambda i, j: (0, 0)),
            pl.BlockSpec((1, 1, D_MODEL), mod_map(0)),
            pl.BlockSpec((1, 1, D_MODEL), mod_map(1)),
            pl.BlockSpec((1, 1, D_MODEL), mod_map(2)),
            pl.BlockSpec((D_MODEL, tf), lambda i, j: (0, j)),
            pl.BlockSpec((D_MODEL, tf), lambda i, j: (0, j)),
            pl.BlockSpec((tf, D_MODEL), lambda i, j: (j, 0)),
            pl.BlockSpec((1, D_MODEL), lambda i, j: (0, 0)),
        ] + cast_specs,
        out_specs=[pl.BlockSpec((tm, D_MODEL), lambda i, j: (i, 0))] + cast_specs,
        out_shape=[jax.ShapeDtypeStruct((m, D_MODEL), F32)]
        + [jax.ShapeDtypeStruct(w.shape, BF16) for w in cast],
        scratch_shapes=[pltpu.VMEM((tm, D_MODEL), BF16)],
        compiler_params=pltpu.CompilerParams(
            dimension_semantics=("arbitrary", "arbitrary"), vmem_limit_bytes=VMEM_LIMIT),
        name="ffn_final" if final_norm else "ffn",
    )(x2d, nw, mod3, mod3, mod3, wg, wu, wd, fw, *cast)
    return res


def _pool_group(ext, u, t_abs, window, pw, ps):
    s, k = ext, 1
    while k < window:
        s = s + pltpu.roll(s, k, axis=0)
        k *= 2
    cnt = jnp.minimum(t_abs + 1, window).astype(F32)
    pooled = s[POOL_HALO:, :] * (1.0 / cnt) - u
    return jnp.dot(pooled.astype(BF16), pw, preferred_element_type=F32) * ps


def _win_kernel(x_ref, nw_ref, sh_ref, sc_ref, w_ref, lbl_ref, pw_ref, ps_ref,
                yp_ref, q_ref, k_ref, gl_ref, v_ref, gt_ref, tail_scr, *, tiles_per_batch):
    tm, tn = yp_ref.shape
    cg = POOL_GROUP_DIM
    lbl = lbl_ref[...]
    pe = jnp.exp(lbl - jnp.max(lbl, axis=0, keepdims=True))
    p = pe / jnp.sum(pe, axis=0, keepdims=True)
    lb = (p[0:1, :] + p[1:2, :]) - p[0:1, :]

    tile_in_seq = pl.program_id(0) % tiles_per_batch

    @pl.when(tile_in_seq == 0)
    def _():
        tail_scr[...] = jnp.zeros_like(tail_scr)

    tail = tail_scr[...]
    row = lax.broadcasted_iota(jnp.int32, (WIN_SUB_ROWS, 1), 0)

    for r in range(0, tm, WIN_SUB_ROWS):
        rows = pl.ds(r, WIN_SUB_ROWS)
        h = _norm_modulate(x_ref[rows, :], nw_ref[...], sh_ref[0], sc_ref[0]).astype(BF16)

        def project(seg):
            return jnp.dot(h, w_ref[:, seg * tn:(seg + 1) * tn], preferred_element_type=F32)

        u = project(0)
        ext = jnp.concatenate([tail, u], axis=0)
        tail = u[WIN_SUB_ROWS - POOL_HALO:, :]
        t_abs = tile_in_seq * tm + r + row
        for g, window in enumerate(POOL_WINDOWS):
            cols = slice(g * cg, (g + 1) * cg)
            y = _pool_group(ext[:, cols], u[:, cols], t_abs, window, pw_ref[g], ps_ref[:, cols])
            yp_ref[rows, cols] = y.astype(BF16)
        q_ref[rows, :] = _silu(project(1)).astype(BF16)
        forget = lb + (1.0 - lb) * jax.nn.sigmoid(project(2))
        k_ref[rows, :] = (1.0 - forget).astype(BF16)
        gl_ref[rows, :] = jnp.log(forget)
        v_ref[rows, :] = project(3).astype(BF16)
        gt_ref[rows, :] = _silu(project(4)).astype(BF16)
    tail_scr[...] = tail


def _win(x2d, nw, mod3, w_in, lb_logits, pool_w, pool_scale):
    m = x2d.shape[0]
    tm, tn = 512, 1024
    assert IN_WIDTH == 5 * tn and POOL_WIDTH == tn and HGRN_WIDTH == tn
    assert max(POOL_WINDOWS) <= POOL_HALO and tn == len(POOL_WINDOWS) * POOL_GROUP_DIM
    tiles_per_batch = SEQ // tm

    def mod_map(k):
        return lambda i: ((i // tiles_per_batch) * N_MOD + 3 + k, 0, 0)

    seg = pl.BlockSpec((tm, tn), lambda i: (i, 0))
    return pl.pallas_call(
        functools.partial(_win_kernel, tiles_per_batch=tiles_per_batch),
        grid=(m // tm,),
        in_specs=[
            pl.BlockSpec((tm, D_MODEL), lambda i: (i, 0)),
            pl.BlockSpec((1, D_MODEL), lambda i: (0, 0)),
            pl.BlockSpec((1, 1, D_MODEL), mod_map(0)),
            pl.BlockSpec((1, 1, D_MODEL), mod_map(1)),
            pl.BlockSpec((D_MODEL, IN_WIDTH), lambda i: (0, 0), pipeline_mode=pl.Buffered(1)),
            pl.BlockSpec((2, tn), lambda i: (0, 0)),
            pl.BlockSpec(pool_w.shape, lambda i: (0, 0, 0)),
            pl.BlockSpec((1, tn), lambda i: (0, 0)),
        ],
        out_specs=[seg] * 6,
        out_shape=[
            jax.ShapeDtypeStruct((m, tn), BF16),
            jax.ShapeDtypeStruct((m, tn), BF16),
            jax.ShapeDtypeStruct((m, tn), BF16),
            jax.ShapeDtypeStruct((m, tn), F32),
            jax.ShapeDtypeStruct((m, tn), BF16),
            jax.ShapeDtypeStruct((m, tn), BF16),
        ],
        scratch_shapes=[pltpu.VMEM((POOL_HALO, tn), F32)],
        compiler_params=pltpu.CompilerParams(
            dimension_semantics=("arbitrary",), vmem_limit_bytes=VMEM_LIMIT),
        name="mix_in",
    )(x2d, nw, mod3, mod3, w_in, lb_logits, pool_w, pool_scale)


def _dot_nt(a, b):
    return lax.dot_general(a, b, (((1,), (1,)), ((), ())), preferred_element_type=F32)


def _dot_tn(a, b):
    return lax.dot_general(a, b, (((0,), (0,)), ((), ())), preferred_element_type=F32)


LEVEL_SIZES = (32, 16, 8, 4, 2, 1)
SCAN_GROUP = 4
LOG2E = 1.4426950408889634


def _row_bcast(v, r):
    return jnp.broadcast_to(v[r:r + 1, :], v.shape)


def _level_exponents(g):
    n = len(g)
    r8 = lax.broadcasted_iota(jnp.int32, (SUBLANES, 1), 0)
    p = list(g)
    for s in (1, 2, 4):
        p = [pi + jnp.where(r8 >= s, pltpu.roll(pi, s, axis=0), 0.0) for pi in p]
    off = [None, _row_bcast(p[0], SUBLANES - 1)]
    for i in range(1, n):
        off.append(off[i] + _row_bcast(p[i], SUBLANES - 1))
    b = [p[0]] + [p[i] + off[i] for i in range(1, n)]

    exps = []
    for hs in LEVEL_SIZES:
        if hs >= SUBLANES:
            w = hs // SUBLANES
            lvl = []
            for i in range(n):
                ref = off[(i // (2 * w)) * 2 * w + w]
                lvl.append(b[i] - ref if (i // w) % 2 == 1 else ref - b[i])
        elif hs == 1:
            lvl = [jnp.where(r8 % 2 == 1, gi, 0.0) for gi in g]
        else:
            lvl = []
            for pi in p:
                ref = _row_bcast(pi, hs - 1)
                for blk in range(1, SUBLANES // (2 * hs)):
                    ref = jnp.where(r8 >= blk * 2 * hs, _row_bcast(pi, blk * 2 * hs + hs - 1), ref)
                lvl.append(-jnp.abs(pi - ref))
        exps.append(lvl)
    return exps, b, off[n]


def _scan_kernel(q_ref, k_ref, gl_ref, v_ref, gt_ref, gw_ref, o_ref,
                 st_scr, attn_scr, qd_scr, kd_scr, dec_scr):
    gw = gw_ref[...]
    nt = CHUNK // SUBLANES
    n_groups = SEQ // (CHUNK * SCAN_GROUP)
    n_levels = len(LEVEL_SIZES)
    ti = lax.broadcasted_iota(jnp.int32, (CHUNK, CHUNK), 0)
    si = lax.broadcasted_iota(jnp.int32, (CHUNK, CHUNK), 1)
    owner = jnp.where(ti == si, n_levels, -1)
    for idx, hs in enumerate(LEVEL_SIZES):
        owner = jnp.where(((ti // (2 * hs)) == (si // (2 * hs))) & ((ti // hs) > (si // hs)),
                          idx, owner)
    st_scr[...] = jnp.zeros_like(st_scr)

    def tiles(a):
        return [a[i * SUBLANES:(i + 1) * SUBLANES, :] for i in range(nt)]

    owner_t = tiles(owner)

    def scaled(ab, e_tiles):
        return ab * jnp.exp2(jnp.concatenate(e_tiles, axis=0)).astype(BF16)

    def rows_of(c):
        return pl.ds(pl.multiple_of(c * CHUNK, CHUNK), CHUNK)

    def scores(c):
        rows = rows_of(c)
        qb = q_ref[0, rows, :]
        kb = k_ref[0, rows, :]
        exps, b, b_last = _level_exponents(tiles(gl_ref[0, rows, :] * LOG2E))

        attn = [jnp.where(o == n_levels, d, 0.0) for o, d in zip(owner_t, tiles(_dot_nt(qb, kb)))]
        for idx, (hs, e) in enumerate(zip(LEVEL_SIZES, exps)):
            lvl = tiles(_dot_nt(scaled(qb, e), scaled(kb, e)))
            for i in range(nt):
                if hs < SUBLANES or (i // (hs // SUBLANES)) % 2 == 1:
                    attn[i] = jnp.where(owner_t[i] == idx, lvl[i], attn[i])
        attn_scr[c] = jnp.concatenate(attn, axis=0).astype(BF16)
        qd_scr[rows, :] = scaled(qb, b)
        kd_scr[rows, :] = scaled(kb, [b_last - bi for bi in b])
        dec_scr[c] = jnp.exp2(b_last)

    def step(g, with_outputs=True, with_scores=True):
        prev = [(g - 1) * SCAN_GROUP + j for j in range(SCAN_GROUP)]
        cur = [g * SCAN_GROUP + j for j in range(SCAN_GROUP)] if with_scores else []
        half = len(cur) // 2
        if with_outputs:
            vbs = [v_ref[0, rows_of(c), :] for c in prev]
            intra = [jnp.dot(attn_scr[c], vb, preferred_element_type=F32)
                     for c, vb in zip(prev, vbs)]
            kv = [_dot_tn(vb, kd_scr[rows_of(c), :]) for c, vb in zip(prev, vbs)]
        for c in cur[:half]:
            scores(c)
        if with_outputs:
            st = st_scr[...]
            inter = []
            for j, c in enumerate(prev):
                inter.append(_dot_nt(qd_scr[rows_of(c), :], st.astype(BF16)))
                st = dec_scr[c][0:1, :] * st + kv[j]
            st_scr[...] = st
        for c in cur[half:]:
            scores(c)
        if with_outputs:
            for j, c in enumerate(prev):
                o = intra[j] + inter[j]
                ms = jnp.mean(o * o, axis=-1, keepdims=True)
                y = o * lax.rsqrt(ms + EPS) * gw * gt_ref[0, rows_of(c), :].astype(F32)
                o_ref[0, rows_of(c), :] = y.astype(BF16)

    step(0, with_outputs=False)

    def body(g, carry):
        step(g)
        return carry

    lax.fori_loop(1, n_groups, body, 0)
    step(n_groups, with_scores=False)


def _scan(q3, k3, gl3, v3, gt3, gnorm_w):
    blk = pl.BlockSpec((1, SEQ, HEAD_DIM), lambda b, h: (b, 0, h))
    return pl.pallas_call(
        _scan_kernel,
        grid=(BATCH, HGRN_HEADS),
        in_specs=[blk, blk, blk, blk, blk, pl.BlockSpec((1, HEAD_DIM), lambda b, h: (0, 0))],
        out_specs=blk,
        out_shape=jax.ShapeDtypeStruct((BATCH, SEQ, HGRN_WIDTH), BF16),
        scratch_shapes=[
            pltpu.VMEM((HEAD_DIM, HEAD_DIM), F32),
            pltpu.VMEM((SEQ // CHUNK, CHUNK, CHUNK), BF16),
            pltpu.VMEM((SEQ, HEAD_DIM), BF16),
            pltpu.VMEM((SEQ, HEAD_DIM), BF16),
            pltpu.VMEM((SEQ // CHUNK, SUBLANES, HEAD_DIM), F32),
        ],
        compiler_params=pltpu.CompilerParams(
            dimension_semantics=("arbitrary", "arbitrary"), vmem_limit_bytes=VMEM_LIMIT),
        name="hgrn_scan",
    )(q3, k3, gl3, v3, gt3, gnorm_w)


def _wout_kernel(x_ref, yp_ref, yh_ref, wp_ref, wh_ref, gt_ref, o_ref):
    mix = (jnp.dot(yp_ref[...], wp_ref[...], preferred_element_type=F32)
           + jnp.dot(yh_ref[...], wh_ref[...], preferred_element_type=F32))
    o_ref[...] = x_ref[...] + gt_ref[0] * mix


def _wout(x2d, yp, yh, w_out, mod3):
    m = x2d.shape[0]
    tm = 512
    tiles_per_batch = SEQ // tm
    return pl.pallas_call(
        _wout_kernel,
        grid=(m // tm,),
        in_specs=[
            pl.BlockSpec((tm, D_MODEL), lambda i: (i, 0)),
            pl.BlockSpec((tm, POOL_WIDTH), lambda i: (i, 0)),
            pl.BlockSpec((tm, HGRN_WIDTH), lambda i: (i, 0)),
            pl.BlockSpec((POOL_WIDTH, D_MODEL), lambda i: (0, 0)),
            pl.BlockSpec((HGRN_WIDTH, D_MODEL), lambda i: (1, 0)),
            pl.BlockSpec((1, 1, D_MODEL), lambda i: ((i // tiles_per_batch) * N_MOD + 5, 0, 0)),
        ],
        out_specs=pl.BlockSpec((tm, D_MODEL), lambda i: (i, 0)),
        out_shape=jax.ShapeDtypeStruct((m, D_MODEL), F32),
        compiler_params=pltpu.CompilerParams(
            dimension_semantics=("arbitrary",), vmem_limit_bytes=VMEM_LIMIT),
        name="mix_out",
    )(x2d, yp, yh, w_out, w_out, mod3)


def kernel(x, c, w_ada, b_ada, norm1_w, ffn1_gate, ffn1_up, ffn1_down, norm2_w, w_in, pool_w,
           pool_scale, lb_logits, gnorm_w, w_out, norm3_w, ffn2_gate, ffn2_up, ffn2_down,
           final_norm_w):
    B, S, D = x.shape
    assert (B, S, D) == (BATCH, SEQ, D_MODEL) and w_ada.shape[0] == 1
    assert lb_logits.shape == (2, HGRN_WIDTH)

    c_pad = jnp.pad(c, ((0, 8 - B), (0, 0)))
    mod = _ada(c_pad, w_ada[0], b_ada)[:B]
    mod3 = mod.reshape(B * N_MOD, 1, D)

    x2d = x.reshape(B * S, D)
    fw = final_norm_w.reshape(1, D)

    x1, w_in_b, w_out_b, g2_b, u2_b, d2_b = _ffn(
        x2d, norm1_w, mod3, 0, ffn1_gate[0].astype(BF16), ffn1_up[0].astype(BF16),
        ffn1_down[0].astype(BF16), fw, final_norm=False,
        cast=(w_in[0], w_out[0], ffn2_gate[0], ffn2_up[0], ffn2_down[0]))

    y_pool, q, k, gl, v, gt = _win(x1, norm2_w, mod3, w_in_b, lb_logits,
                                   pool_w[0].astype(BF16), pool_scale)
    y_hgrn = _scan(*(a.reshape(B, S, HGRN_WIDTH) for a in (q, k, gl, v, gt)), gnorm_w)
    x2 = _wout(x1, y_pool, y_hgrn.reshape(B * S, HGRN_WIDTH), w_out_b, mod3)

    out, = _ffn(x2, norm3_w, mod3, 6, g2_b, u2_b, d2_b, fw, final_norm=True)
    return out.reshape(B, S, D)
```

```python
import functools

import jax
import jax.numpy as jnp
from jax import lax
from jax.experimental import pallas as pl
from jax.experimental.pallas import tpu as pltpu

F32 = jnp.float32
BF16 = jnp.bfloat16

D_MODEL = 2048
BATCH = 4
SEQ = 2048
POOL_WIDTH = 1024
POOL_WINDOWS = (2, 4, 8, 16)
POOL_GROUP_DIM = 256
HGRN_WIDTH = 1024
HGRN_HEADS = 8
HEAD_DIM = 128
IN_WIDTH = 5120
CHUNK = 64
SUBLANES = 8
BF16_SUBLANES = 16
WIN_SUB_ROWS = 256
FFN_SUB_ROWS = 256
FFN_MID_SUB_ROWS = 512
POOL_HALO = 16
D_FF = 5632
N_MOD = 9
EPS = 1e-6

VMEM_LIMIT = 60 * 1024 * 1024


def _silu(v):
    return v * jax.nn.sigmoid(v)


def _norm_modulate(x, nw, sh, sc):
    ms = jnp.mean(x * x, axis=-1, keepdims=True)
    y = x * lax.rsqrt(ms + EPS) * nw
    return y * (1.0 + sc) + sh


def _ada_kernel(c_ref, w_ref, b_ref, o_ref):
    ca = _silu(c_ref[...]).astype(BF16)
    o_ref[...] = jnp.dot(ca, w_ref[...].astype(BF16), preferred_element_type=F32) + b_ref[...]


def _ada(c_pad, w_ada, b_ada):
    n = w_ada.shape[1]
    tn = 1024
    return pl.pallas_call(
        _ada_kernel,
        grid=(n // tn,),
        in_specs=[
            pl.BlockSpec((8, D_MODEL), lambda j: (0, 0)),
            pl.BlockSpec((D_MODEL, tn), lambda j: (0, j)),
            pl.BlockSpec((1, tn), lambda j: (0, j)),
        ],
        out_specs=pl.BlockSpec((8, tn), lambda j: (0, j)),
        out_shape=jax.ShapeDtypeStruct((8, n), F32),
        compiler_params=pltpu.CompilerParams(
            dimension_semantics=("arbitrary",), vmem_limit_bytes=VMEM_LIMIT),
        name="ada_mod",
    )(c_pad, w_ada, b_ada)


def _ffn_kernel(*refs, nj, final_norm, n_cast):
    x_ref, nw_ref, sh_ref, sc_ref, gt_ref, wg_ref, wu_ref, wd_ref, fw_ref = refs[:9]
    cast_in = refs[9:9 + n_cast]
    o_ref = refs[9 + n_cast]
    cast_out = refs[10 + n_cast:10 + 2 * n_cast]
    h_scr, = refs[10 + 2 * n_cast:]
    j = pl.program_id(1)
    tm = x_ref.shape[0]
    subs = [pl.ds(r, FFN_SUB_ROWS) for r in range(0, tm, FFN_SUB_ROWS)]
    mid_subs = [pl.ds(r, FFN_MID_SUB_ROWS) for r in range(0, tm, FFN_MID_SUB_ROWS)]

    for w_ref, wb_ref in zip(cast_in, cast_out):
        wb_ref[...] = w_ref[...].astype(BF16)

    def partial_ffn(rows):
        h = h_scr[rows, :]
        g = jnp.dot(h, wg_ref[...], preferred_element_type=F32)
        u = jnp.dot(h, wu_ref[...], preferred_element_type=F32)
        a = (_silu(g) * u).astype(BF16)
        return jnp.dot(a, wd_ref[...], preferred_element_type=F32)

    @pl.when(j == 0)
    def _():
        for rows in subs:
            h = _norm_modulate(x_ref[rows, :], nw_ref[...], sh_ref[0], sc_ref[0])
            h_scr[rows, :] = h.astype(BF16)
            o_ref[rows, :] = partial_ffn(rows)

    @pl.when((j > 0) & (j < nj - 1))
    def _():
        for rows in mid_subs:
            o_ref[rows, :] += partial_ffn(rows)

    @pl.when(j == nj - 1)
    def _():
        for rows in subs:
            out = x_ref[rows, :] + 0.5 * gt_ref[0] * (o_ref[rows, :] + partial_ffn(rows))
            if final_norm:
                ms = jnp.mean(out * out, axis=-1, keepdims=True)
                out = out * lax.rsqrt(ms + EPS) * fw_ref[...]
            o_ref[rows, :] = out


def _ffn(x2d, nw, mod3, mod_base, wg, wu, wd, fw, *, final_norm, cast=()):
    m = x2d.shape[0]
    tm, tf = 1024, 512
    ni, nj = m // tm, D_FF // tf
    tiles_per_batch = SEQ // tm

    def mod_map(k):
        return lambda i, j: ((i // tiles_per_batch) * N_MOD + mod_base + k, 0, 0)

    cast_specs = []
    for w in cast:
        rows = w.shape[0]
        nblk = max(n for n in range(1, ni * nj + 1) if rows % (BF16_SUBLANES * n) == 0)
        cast_specs.append(pl.BlockSpec(
            (rows // nblk, w.shape[1]),
            lambda i, j, nblk=nblk: (jnp.minimum(i * nj + j, nblk - 1), 0)))

    kern = functools.partial(_ffn_kernel, nj=nj, final_norm=final_norm, n_cast=len(cast))
    res = pl.pallas_call(
        kern,
        grid=(ni, nj),
        in_specs=[
            pl.BlockSpec((tm, D_MODEL), lambda i, j: (i, 0)),
            pl.BlockSpec((1, D_MODEL), lambda i, j: (0, 0)),
            pl.BlockSpec((1, 1, D_MODEL), mod_map(0)),
            pl.BlockSpec((1, 1, D_MODEL), mod_map(1)),
            pl.BlockSpec((1, 1, D_MODEL), mod_map(2)),
            pl.BlockSpec((D_MODEL, tf), lambda i, j: (0, j)),
            pl.BlockSpec((D_MODEL, tf), lambda i, j: (0, j)),
            pl.BlockSpec((tf, D_MODEL), lambda i, j: (j, 0)),
            pl.BlockSpec((1, D_MODEL), lambda i, j: (0, 0)),
        ] + cast_specs,
        out_specs=[pl.BlockSpec((tm, D_MODEL), lambda i, j: (i, 0))] + cast_specs,
        out_shape=[jax.ShapeDtypeStruct((m, D_MODEL), F32)]
        + [jax.ShapeDtypeStruct(w.shape, BF16) for w in cast],
        scratch_shapes=[pltpu.VMEM((tm, D_MODEL), BF16)],
        compiler_params=pltpu.CompilerParams(
            dimension_semantics=("arbitrary", "arbitrary"), vmem_limit_bytes=VMEM_LIMIT),
        name="ffn_final" if final_norm else "ffn",
    )(x2d, nw, mod3, mod3, mod3, wg, wu, wd, fw, *cast)
    return res


def _pool_group(ext, u, t_abs, window, pw, ps):
    s, k = ext, 1
    while k < window:
        s = s + pltpu.roll(s, k, axis=0)
        k *= 2
    cnt = jnp.minimum(t_abs + 1, window).astype(F32)
    pooled = s[POOL_HALO:, :] * (1.0 / cnt) - u
    return jnp.dot(pooled.astype(BF16), pw, preferred_element_type=F32) * ps


def _win_kernel(x_ref, nw_ref, sh_ref, sc_ref, w_ref, lbl_ref, pw_ref, ps_ref,
                yp_ref, q_ref, k_ref, gl_ref, v_ref, gt_ref, tail_scr, *, tiles_per_batch):
    tm, tn = yp_ref.shape
    cg = POOL_GROUP_DIM
    lbl = lbl_ref[...]
    pe = jnp.exp(lbl - jnp.max(lbl, axis=0, keepdims=True))
    p = pe / jnp.sum(pe, axis=0, keepdims=True)
    lb = (p[0:1, :] + p[1:2, :]) - p[0:1, :]

    tile_in_seq = pl.program_id(0) % tiles_per_batch

    @pl.when(tile_in_seq == 0)
    def _():
        tail_scr[...] = jnp.zeros_like(tail_scr)

    tail = tail_scr[...]
    row = lax.broadcasted_iota(jnp.int32, (WIN_SUB_ROWS, 1), 0)

    for r in range(0, tm, WIN_SUB_ROWS):
        rows = pl.ds(r, WIN_SUB_ROWS)
        h = _norm_modulate(x_ref[rows, :], nw_ref[...], sh_ref[0], sc_ref[0]).astype(BF16)

        def project(seg):
            return jnp.dot(h, w_ref[:, seg * tn:(seg + 1) * tn], preferred_element_type=F32)

        u = project(0)
        ext = jnp.concatenate([tail, u], axis=0)
        tail = u[WIN_SUB_ROWS - POOL_HALO:, :]
        t_abs = tile_in_seq * tm + r + row
        for g, window in enumerate(POOL_WINDOWS):
            cols = slice(g * cg, (g + 1) * cg)
            y = _pool_group(ext[:, cols], u[:, cols], t_abs, window, pw_ref[g], ps_ref[:, cols])
            yp_ref[rows, cols] = y.astype(BF16)
        q_ref[rows, :] = _silu(project(1)).astype(BF16)
        forget = lb + (1.0 - lb) * jax.nn.sigmoid(project(2))
        k_ref[rows, :] = (1.0 - forget).astype(BF16)
        gl_ref[rows, :] = jnp.log(forget)
        v_ref[rows, :] = project(3).astype(BF16)
        gt_ref[rows, :] = _silu(project(4)).astype(BF16)
    tail_scr[...] = tail


def _win(x2d, nw, mod3, w_in, lb_logits, pool_w, pool_scale):
    m = x2d.shape[0]
    tm, tn = 512, 1024
    assert IN_WIDTH == 5 * tn and POOL_WIDTH == tn and HGRN_WIDTH == tn
    assert max(POOL_WINDOWS) <= POOL_HALO and tn == len(POOL_WINDOWS) * POOL_GROUP_DIM
    tiles_per_batch = SEQ // tm

    def mod_map(k):
        return lambda i: ((i // tiles_per_batch) * N_MOD + 3 + k, 0, 0)

    seg = pl.BlockSpec((tm, tn), lambda i: (i, 0))
    return pl.pallas_call(
        functools.partial(_win_kernel, tiles_per_batch=tiles_per_batch),
        grid=(m // tm,),
        in_specs=[
            pl.BlockSpec((tm, D_MODEL), lambda i: (i, 0)),
            pl.BlockSpec((1, D_MODEL), lambda i: (0, 0)),
            pl.BlockSpec((1, 1, D_MODEL), mod_map(0)),
            pl.BlockSpec((1, 1, D_MODEL), mod_map(1)),
            pl.BlockSpec((D_MODEL, IN_WIDTH), lambda i: (0, 0), pipeline_mode=pl.Buffered(1)),
            pl.BlockSpec((2, tn), lambda i: (0, 0)),
            pl.BlockSpec(pool_w.shape, lambda i: (0, 0, 0)),
            pl.BlockSpec((1, tn), lambda i: (0, 0)),
        ],
        out_specs=[seg] * 6,
        out_shape=[
            jax.ShapeDtypeStruct((m, tn), BF16),
            jax.ShapeDtypeStruct((m, tn), BF16),
            jax.ShapeDtypeStruct((m, tn), BF16),
            jax.ShapeDtypeStruct((m, tn), F32),
            jax.ShapeDtypeStruct((m, tn), BF16),
            jax.ShapeDtypeStruct((m, tn), BF16),
        ],
        scratch_shapes=[pltpu.VMEM((POOL_HALO, tn), F32)],
        compiler_params=pltpu.CompilerParams(
            dimension_semantics=("arbitrary",), vmem_limit_bytes=VMEM_LIMIT),
        name="mix_in",
    )(x2d, nw, mod3, mod3, w_in, lb_logits, pool_w, pool_scale)


def _dot_nt(a, b):
    return lax.dot_general(a, b, (((1,), (1,)), ((), ())), preferred_element_type=F32)


def _dot_tn(a, b):
    return lax.dot_general(a, b, (((0,), (0,)), ((), ())), preferred_element_type=F32)


LEVEL_SIZES = (32, 16, 8, 4, 2, 1)
SCAN_GROUP = 4
SCAN_HEADS = 4
LOG2E = 1.4426950408889634


def _row_bcast(v, r):
    return jnp.broadcast_to(v[r:r + 1, :], v.shape)


def _level_exponents(g):
    n = len(g)
    r8 = lax.broadcasted_iota(jnp.int32, (SUBLANES, 1), 0)
    p = list(g)
    for s in (1, 2, 4):
        p = [pi + jnp.where(r8 >= s, pltpu.roll(pi, s, axis=0), 0.0) for pi in p]
    off = [None, _row_bcast(p[0], SUBLANES - 1)]
    for i in range(1, n):
        off.append(off[i] + _row_bcast(p[i], SUBLANES - 1))
    b = [p[0]] + [p[i] + off[i] for i in range(1, n)]

    exps = []
    for hs in LEVEL_SIZES:
        if hs >= SUBLANES:
            w = hs // SUBLANES
            lvl = []
            for i in range(n):
                ref = off[(i // (2 * w)) * 2 * w + w]
                lvl.append(b[i] - ref if (i // w) % 2 == 1 else ref - b[i])
        elif hs == 1:
            lvl = [jnp.where(r8 % 2 == 1, gi, 0.0) for gi in g]
        else:
            lvl = []
            for pi in p:
                ref = _row_bcast(pi, hs - 1)
                for blk in range(1, SUBLANES // (2 * hs)):
                    ref = jnp.where(r8 >= blk * 2 * hs, _row_bcast(pi, blk * 2 * hs + hs - 1), ref)
                lvl.append(-jnp.abs(pi - ref))
        exps.append(lvl)
    return exps, b, off[n]


def _scan_kernel(q_ref, k_ref, gl_ref, v_ref, gt_ref, gw_ref, o_ref,
                 st_scr, attn_scr, qd_scr, kd_scr, dec_scr):
    gw = gw_ref[...]
    nt = CHUNK // SUBLANES
    n_groups = SEQ // (CHUNK * SCAN_GROUP)
    n_levels = len(LEVEL_SIZES)
    ti = lax.broadcasted_iota(jnp.int32, (CHUNK, CHUNK), 0)
    si = lax.broadcasted_iota(jnp.int32, (CHUNK, CHUNK), 1)
    owner = jnp.where(ti == si, n_levels, -1)
    for idx, hs in enumerate(LEVEL_SIZES):
        owner = jnp.where(((ti // (2 * hs)) == (si // (2 * hs))) & ((ti // hs) > (si // hs)),
                          idx, owner)
    st_scr[...] = jnp.zeros_like(st_scr)

    def tiles(a):
        return [a[i * SUBLANES:(i + 1) * SUBLANES, :] for i in range(nt)]

    owner_t = tiles(owner)

    def scaled(ab, e_tiles):
        return ab * jnp.exp2(jnp.concatenate(e_tiles, axis=0)).astype(BF16)

    def rows_of(c):
        return pl.ds(pl.multiple_of(c * CHUNK, CHUNK), CHUNK)

    def cols_of(hd):
        return slice(hd * HEAD_DIM, (hd + 1) * HEAD_DIM)

    def scores(hd, c):
        rows, cols = rows_of(c), cols_of(hd)
        qb = q_ref[0, rows, cols]
        kb = k_ref[0, rows, cols]
        exps, b, b_last = _level_exponents(tiles(gl_ref[0, rows, cols] * LOG2E))

        attn = [jnp.where(o == n_levels, d, 0.0) for o, d in zip(owner_t, tiles(_dot_nt(qb, kb)))]
        for idx, (hs, e) in enumerate(zip(LEVEL_SIZES, exps)):
            lvl = tiles(_dot_nt(scaled(qb, e), scaled(kb, e)))
            for i in range(nt):
                if hs < SUBLANES or (i // (hs // SUBLANES)) % 2 == 1:
                    attn[i] = jnp.where(owner_t[i] == idx, lvl[i], attn[i])
        attn_scr[c] = jnp.concatenate(attn, axis=0).astype(BF16)
        qd_scr[rows, :] = scaled(qb, b)
        kd_scr[rows, :] = scaled(kb, [b_last - bi for bi in b])
        dec_scr[c] = jnp.exp2(b_last)

    def step(out_of, scores_of):
        prev, cur = [], []
        if out_of is not None:
            hp, gp = out_of
            prev = [gp * SCAN_GROUP + j for j in range(SCAN_GROUP)]
        if scores_of is not None:
            hc, gc = scores_of
            cur = [gc * SCAN_GROUP + j for j in range(SCAN_GROUP)]
        half = len(cur) // 2
        if prev:
            vbs = [v_ref[0, rows_of(c), cols_of(hp)] for c in prev]
            intra = [jnp.dot(attn_scr[c], vb, preferred_element_type=F32)
                     for c, vb in zip(prev, vbs)]
            kv = [_dot_tn(vb, kd_scr[rows_of(c), :]) for c, vb in zip(prev, vbs)]
        for c in cur[:half]:
            scores(hc, c)
        if prev:
            st = st_scr[...]
            inter = []
            for j, c in enumerate(prev):
                inter.append(_dot_nt(qd_scr[rows_of(c), :], st.astype(BF16)))
                st = dec_scr[c][0:1, :] * st + kv[j]
            last_of_head = isinstance(gp, int) and gp == n_groups - 1
            st_scr[...] = jnp.zeros_like(st) if last_of_head else st
        for c in cur[half:]:
            scores(hc, c)
        if prev:
            for j, c in enumerate(prev):
                o = intra[j] + inter[j]
                ms = jnp.mean(o * o, axis=-1, keepdims=True)
                gate = gt_ref[0, rows_of(c), cols_of(hp)].astype(F32)
                y = o * lax.rsqrt(ms + EPS) * gw * gate
                o_ref[0, rows_of(c), cols_of(hp)] = y.astype(BF16)

    n_heads = q_ref.shape[2] // HEAD_DIM
    step(None, (0, 0))
    for hd in range(n_heads):
        def body(g, carry, hd=hd):
            step((hd, g - 1), (hd, g))
            return carry

        lax.fori_loop(1, n_groups, body, 0)
        step((hd, n_groups - 1), (hd + 1, 0) if hd + 1 < n_heads else None)


def _scan(q3, k3, gl3, v3, gt3, gnorm_w):
    blk = pl.BlockSpec((1, SEQ, SCAN_HEADS * HEAD_DIM), lambda b, h: (b, 0, h))
    return pl.pallas_call(
        _scan_kernel,
        grid=(BATCH, HGRN_HEADS // SCAN_HEADS),
        in_specs=[blk, blk, blk, blk, blk, pl.BlockSpec((1, HEAD_DIM), lambda b, h: (0, 0))],
        out_specs=blk,
        out_shape=jax.ShapeDtypeStruct((BATCH, SEQ, HGRN_WIDTH), BF16),
        scratch_shapes=[
            pltpu.VMEM((HEAD_DIM, HEAD_DIM), F32),
            pltpu.VMEM((SEQ // CHUNK, CHUNK, CHUNK), BF16),
            pltpu.VMEM((SEQ, HEAD_DIM), BF16),
            pltpu.VMEM((SEQ, HEAD_DIM), BF16),
            pltpu.VMEM((SEQ // CHUNK, SUBLANES, HEAD_DIM), F32),
        ],
        compiler_params=pltpu.CompilerParams(
            dimension_semantics=("arbitrary", "arbitrary"), vmem_limit_bytes=VMEM_LIMIT),
        name="hgrn_scan",
    )(q3, k3, gl3, v3, gt3, gnorm_w)


def _wout_kernel(x_ref, yp_ref, yh_ref, wp_ref, wh_ref, gt_ref, o_ref):
    mix = (jnp.dot(yp_ref[...], wp_ref[...], preferred_element_type=F32)
           + jnp.dot(yh_ref[...], wh_ref[...], preferred_element_type=F32))
    o_ref[...] = x_ref[...] + gt_ref[0] * mix


def _wout(x2d, yp, yh, w_out, mod3):
    m = x2d.shape[0]
    tm = 512
    tiles_per_batch = SEQ // tm
    return pl.pallas_call(
        _wout_kernel,
        grid=(m // tm,),
        in_specs=[
            pl.BlockSpec((tm, D_MODEL), lambda i: (i, 0)),
            pl.BlockSpec((tm, POOL_WIDTH), lambda i: (i, 0)),
            pl.BlockSpec((tm, HGRN_WIDTH), lambda i: (i, 0)),
            pl.BlockSpec((POOL_WIDTH, D_MODEL), lambda i: (0, 0)),
            pl.BlockSpec((HGRN_WIDTH, D_MODEL), lambda i: (1, 0)),
            pl.BlockSpec((1, 1, D_MODEL), lambda i: ((i // tiles_per_batch) * N_MOD + 5, 0, 0)),
        ],
        out_specs=pl.BlockSpec((tm, D_MODEL), lambda i: (i, 0)),
        out_shape=jax.ShapeDtypeStruct((m, D_MODEL), F32),
        compiler_params=pltpu.CompilerParams(
            dimension_semantics=("arbitrary",), vmem_limit_bytes=VMEM_LIMIT),
        name="mix_out",
    )(x2d, yp, yh, w_out, w_out, mod3)


def kernel(x, c, w_ada, b_ada, norm1_w, ffn1_gate, ffn1_up, ffn1_down, norm2_w, w_in, pool_w,
           pool_scale, lb_logits, gnorm_w, w_out, norm3_w, ffn2_gate, ffn2_up, ffn2_down,
           final_norm_w):
    B, S, D = x.shape
    assert (B, S, D) == (BATCH, SEQ, D_MODEL) and w_ada.shape[0] == 1
    assert lb_logits.shape == (2, HGRN_WIDTH)

    c_pad = jnp.pad(c, ((0, 8 - B), (0, 0)))
    mod = _ada(c_pad, w_ada[0], b_ada)[:B]
    mod3 = mod.reshape(B * N_MOD, 1, D)

    x2d = x.reshape(B * S, D)
    fw = final_norm_w.reshape(1, D)

    x1, w_in_b, w_out_b, g2_b, u2_b, d2_b = _ffn(
        x2d, norm1_w, mod3, 0, ffn1_gate[0].astype(BF16), ffn1_up[0].astype(BF16),
        ffn1_down[0].astype(BF16), fw, final_norm=False,
        cast=(w_in[0], w_out[0], ffn2_gate[0], ffn2_up[0], ffn2_down[0]))

    y_pool, q, k, gl, v, gt = _win(x1, norm2_w, mod3, w_in_b, lb_logits,
                                   pool_w[0].astype(BF16), pool_scale)
    y_hgrn = _scan(*(a.reshape(B, S, HGRN_WIDTH) for a in (q, k, gl, v, gt)), gnorm_w)
    x2 = _wout(x1, y_pool, y_hgrn.reshape(B * S, HGRN_WIDTH), w_out_b, mod3)

    out, = _ffn(x2, norm3_w, mod3, 6, g2_b, u2_b, d2_b, fw, final_norm=True)
    return out.reshape(B, S, D)
```

```python
import functools

import jax
import jax.numpy as jnp
from jax import lax
from jax.experimental import pallas as pl
from jax.experimental.pallas import tpu as pltpu

F32 = jnp.float32
BF16 = jnp.bfloat16

D_MODEL = 2048
BATCH = 4
SEQ = 2048
POOL_WIDTH = 1024
POOL_WINDOWS = (2, 4, 8, 16)
POOL_GROUP_DIM = 256
HGRN_WIDTH = 1024
HGRN_HEADS = 8
HEAD_DIM = 128
IN_WIDTH = 5120
CHUNK = 64
SUBLANES = 8
BF16_SUBLANES = 16
WIN_SUB_ROWS = 512
FFN_SUB_ROWS = 512
FFN_MID_SUB_ROWS = 512
POOL_HALO = 16
D_FF = 5632
N_MOD = 9
EPS = 1e-6

VMEM_LIMIT = 60 * 1024 * 1024


def _silu(v):
    return v * jax.nn.sigmoid(v)


def _norm_modulate(x, nw, sh, sc):
    ms = jnp.mean(x * x, axis=-1, keepdims=True)
    y = x * lax.rsqrt(ms + EPS) * nw
    return y * (1.0 + sc) + sh


def _ada_kernel(c_ref, w_ref, b_ref, o_ref):
    ca = _silu(c_ref[...]).astype(BF16)
    o_ref[...] = jnp.dot(ca, w_ref[...].astype(BF16), preferred_element_type=F32) + b_ref[...]


def _ada(c_pad, w_ada, b_ada):
    n = w_ada.shape[1]
    tn = 1024
    return pl.pallas_call(
        _ada_kernel,
        grid=(n // tn,),
        in_specs=[
            pl.BlockSpec((8, D_MODEL), lambda j: (0, 0)),
            pl.BlockSpec((D_MODEL, tn), lambda j: (0, j)),
            pl.BlockSpec((1, tn), lambda j: (0, j)),
        ],
        out_specs=pl.BlockSpec((8, tn), lambda j: (0, j)),
        out_shape=jax.ShapeDtypeStruct((8, n), F32),
        compiler_params=pltpu.CompilerParams(
            dimension_semantics=("arbitrary",), vmem_limit_bytes=VMEM_LIMIT),
        name="ada_mod",
    )(c_pad, w_ada, b_ada)


def _ffn_kernel(*refs, nj, final_norm, n_cast):
    x_ref, nw_ref, sh_ref, sc_ref, gt_ref, wg_ref, wu_ref, wd_ref, fw_ref = refs[:9]
    cast_in = refs[9:9 + n_cast]
    o_ref = refs[9 + n_cast]
    cast_out = refs[10 + n_cast:10 + 2 * n_cast]
    h_scr, = refs[10 + 2 * n_cast:]
    j = pl.program_id(1)
    tm = x_ref.shape[0]
    subs = [pl.ds(r, FFN_SUB_ROWS) for r in range(0, tm, FFN_SUB_ROWS)]
    mid_subs = [pl.ds(r, FFN_MID_SUB_ROWS) for r in range(0, tm, FFN_MID_SUB_ROWS)]

    for w_ref, wb_ref in zip(cast_in, cast_out):
        wb_ref[...] = w_ref[...].astype(BF16)

    def partial_ffn(rows):
        h = h_scr[rows, :]
        g = jnp.dot(h, wg_ref[...], preferred_element_type=F32)
        u = jnp.dot(h, wu_ref[...], preferred_element_type=F32)
        a = (_silu(g) * u).astype(BF16)
        return jnp.dot(a, wd_ref[...], preferred_element_type=F32)

    @pl.when(j == 0)
    def _():
        for rows in subs:
            h = _norm_modulate(x_ref[rows, :], nw_ref[...], sh_ref[0], sc_ref[0])
            h_scr[rows, :] = h.astype(BF16)
            o_ref[rows, :] = partial_ffn(rows)

    @pl.when((j > 0) & (j < nj - 1))
    def _():
        for rows in mid_subs:
            o_ref[rows, :] += partial_ffn(rows)

    @pl.when(j == nj - 1)
    def _():
        for rows in subs:
            out = x_ref[rows, :] + 0.5 * gt_ref[0] * (o_ref[rows, :] + partial_ffn(rows))
            if final_norm:
                ms = jnp.mean(out * out, axis=-1, keepdims=True)
                out = out * lax.rsqrt(ms + EPS) * fw_ref[...]
            o_ref[rows, :] = out


def _ffn(x2d, nw, mod3, mod_base, wg, wu, wd, fw, *, final_norm, cast=()):
    m = x2d.shape[0]
    tm, tf = 1024, 512
    ni, nj = m // tm, D_FF // tf
    tiles_per_batch = SEQ // tm

    def mod_map(k):
        return lambda i, j: ((i // tiles_per_batch) * N_MOD + mod_base + k, 0, 0)

    cast_specs = []
    for w in cast:
        rows = w.shape[0]
        nblk = max(n for n in range(1, ni * nj + 1) if rows % (BF16_SUBLANES * n) == 0)
        cast_specs.append(pl.BlockSpec(
            (rows // nblk, w.shape[1]),
            lambda i, j, nblk=nblk: (jnp.minimum(i * nj + j, nblk - 1), 0)))

    kern = functools.partial(_ffn_kernel, nj=nj, final_norm=final_norm, n_cast=len(cast))
    res = pl.pallas_call(
        kern,
        grid=(ni, nj),
        in_specs=[
            pl.BlockSpec((tm, D_MODEL), lambda i, j: (i, 0)),
            pl.BlockSpec((1, D_MODEL), lambda i, j: (0, 0)),
            pl.BlockSpec((1, 1, D_MODEL), mod_map(0)),
            pl.BlockSpec((1, 1, D_MODEL), mod_map(1)),
            pl.BlockSpec((1, 1, D_MODEL), mod_map(2)),
            pl.BlockSpec((D_MODEL, tf), lambda i, j: (0, j)),
            pl.BlockSpec((D_MODEL, tf), lambda i, j: (0, j)),
            pl.BlockSpec((tf, D_MODEL), lambda i, j: (j, 0)),
            pl.BlockSpec((1, D_MODEL), lambda i, j: (0, 0)),
        ] + cast_specs,
        out_specs=[pl.BlockSpec((tm, D_MODEL), lambda i, j: (i, 0))] + cast_specs,
        out_shape=[jax.ShapeDtypeStruct((m, D_MODEL), F32)]
        + [jax.ShapeDtypeStruct(w.shape, BF16) for w in cast],
        scratch_shapes=[pltpu.VMEM((tm, D_MODEL), BF16)],
        compiler_params=pltpu.CompilerParams(
            dimension_semantics=("arbitrary", "arbitrary"), vmem_limit_bytes=VMEM_LIMIT),
        name="ffn_final" if final_norm else "ffn",
    )(x2d, nw, mod3, mod3, mod3, wg, wu, wd, fw, *cast)
    return res


def _pool_group(ext, u, t_abs, window, pw, ps):
    s, k = ext, 1
    while k < window:
        s = s + pltpu.roll(s, k, axis=0)
        k *= 2
    cnt = jnp.minimum(t_abs + 1, window).astype(F32)
    pooled = s[POOL_HALO:, :] * (1.0 / cnt) - u
    return jnp.dot(pooled.astype(BF16), pw, preferred_element_type=F32) * ps


def _win_kernel(x_ref, nw_ref, sh_ref, sc_ref, w_ref, lbl_ref, pw_ref, ps_ref,
                yp_ref, q_ref, k_ref, gl_ref, v_ref, gt_ref, tail_scr, *, tiles_per_batch):
    tm, tn = yp_ref.shape
    cg = POOL_GROUP_DIM
    lbl = lbl_ref[...]
    pe = jnp.exp(lbl - jnp.max(lbl, axis=0, keepdims=True))
    p = pe / jnp.sum(pe, axis=0, keepdims=True)
    lb = (p[0:1, :] + p[1:2, :]) - p[0:1, :]

    tile_in_seq = pl.program_id(0) % tiles_per_batch

    @pl.when(tile_in_seq == 0)
    def _():
        tail_scr[...] = jnp.zeros_like(tail_scr)

    tail = tail_scr[...]
    row = lax.broadcasted_iota(jnp.int32, (WIN_SUB_ROWS, 1), 0)

    for r in range(0, tm, WIN_SUB_ROWS):
        rows = pl.ds(r, WIN_SUB_ROWS)
        h = _norm_modulate(x_ref[rows, :], nw_ref[...], sh_ref[0], sc_ref[0]).astype(BF16)

        def project(seg):
            return jnp.dot(h, w_ref[:, seg * tn:(seg + 1) * tn], preferred_element_type=F32)

        u = project(0)
        ext = jnp.concatenate([tail, u], axis=0)
        tail = u[WIN_SUB_ROWS - POOL_HALO:, :]
        t_abs = tile_in_seq * tm + r + row
        for g, window in enumerate(POOL_WINDOWS):
            cols = slice(g * cg, (g + 1) * cg)
            y = _pool_group(ext[:, cols], u[:, cols], t_abs, window, pw_ref[g], ps_ref[:, cols])
            yp_ref[rows, cols] = y.astype(BF16)
        q_ref[rows, :] = _silu(project(1)).astype(BF16)
        forget = lb + (1.0 - lb) * jax.nn.sigmoid(project(2))
        k_ref[rows, :] = (1.0 - forget).astype(BF16)
        gl_ref[rows, :] = jnp.log(forget)
        v_ref[rows, :] = project(3).astype(BF16)
        gt_ref[rows, :] = _silu(project(4)).astype(BF16)
    tail_scr[...] = tail


def _win(x2d, nw, mod3, w_in, lb_logits, pool_w, pool_scale):
    m = x2d.shape[0]
    tm, tn = 512, 1024
    assert IN_WIDTH == 5 * tn and POOL_WIDTH == tn and HGRN_WIDTH == tn
    assert max(POOL_WINDOWS) <= POOL_HALO and tn == len(POOL_WINDOWS) * POOL_GROUP_DIM
    tiles_per_batch = SEQ // tm

    def mod_map(k):
        return lambda i: ((i // tiles_per_batch) * N_MOD + 3 + k, 0, 0)

    seg = pl.BlockSpec((tm, tn), lambda i: (i, 0))
    return pl.pallas_call(
        functools.partial(_win_kernel, tiles_per_batch=tiles_per_batch),
        grid=(m // tm,),
        in_specs=[
            pl.BlockSpec((tm, D_MODEL), lambda i: (i, 0)),
            pl.BlockSpec((1, D_MODEL), lambda i: (0, 0)),
            pl.BlockSpec((1, 1, D_MODEL), mod_map(0)),
            pl.BlockSpec((1, 1, D_MODEL), mod_map(1)),
            pl.BlockSpec((D_MODEL, IN_WIDTH), lambda i: (0, 0), pipeline_mode=pl.Buffered(1)),
            pl.BlockSpec((2, tn), lambda i: (0, 0)),
            pl.BlockSpec(pool_w.shape, lambda i: (0, 0, 0)),
            pl.BlockSpec((1, tn), lambda i: (0, 0)),
        ],
        out_specs=[seg] * 6,
        out_shape=[
            jax.ShapeDtypeStruct((m, tn), BF16),
            jax.ShapeDtypeStruct((m, tn), BF16),
            jax.ShapeDtypeStruct((m, tn), BF16),
            jax.ShapeDtypeStruct((m, tn), F32),
            jax.ShapeDtypeStruct((m, tn), BF16),
            jax.ShapeDtypeStruct((m, tn), BF16),
        ],
        scratch_shapes=[pltpu.VMEM((POOL_HALO, tn), F32)],
        compiler_params=pltpu.CompilerParams(
            dimension_semantics=("arbitrary",), vmem_limit_bytes=VMEM_LIMIT),
        name="mix_in",
    )(x2d, nw, mod3, mod3, w_in, lb_logits, pool_w, pool_scale)


def _dot_nt(a, b):
    return lax.dot_general(a, b, (((1,), (1,)), ((), ())), preferred_element_type=F32)


def _dot_tn(a, b):
    return lax.dot_general(a, b, (((0,), (0,)), ((), ())), preferred_element_type=F32)


LEVEL_SIZES = (32, 16, 8, 4, 2, 1)
SCAN_GROUP = 4
SCAN_HEADS = 4
LOG2E = 1.4426950408889634


def _row_bcast(v, r):
    return jnp.broadcast_to(v[r:r + 1, :], v.shape)


def _level_exponents(g):
    n = len(g)
    r8 = lax.broadcasted_iota(jnp.int32, (SUBLANES, 1), 0)
    p = list(g)
    for s in (1, 2, 4):
        p = [pi + jnp.where(r8 >= s, pltpu.roll(pi, s, axis=0), 0.0) for pi in p]
    off = [None, _row_bcast(p[0], SUBLANES - 1)]
    for i in range(1, n):
        off.append(off[i] + _row_bcast(p[i], SUBLANES - 1))
    b = [p[0]] + [p[i] + off[i] for i in range(1, n)]

    exps = []
    for hs in LEVEL_SIZES:
        if hs >= SUBLANES:
            w = hs // SUBLANES
            lvl = []
            for i in range(n):
                ref = off[(i // (2 * w)) * 2 * w + w]
                lvl.append(b[i] - ref if (i // w) % 2 == 1 else ref - b[i])
        elif hs == 1:
            lvl = [jnp.where(r8 % 2 == 1, gi, 0.0) for gi in g]
        else:
            lvl = []
            for pi in p:
                ref = _row_bcast(pi, hs - 1)
                for blk in range(1, SUBLANES // (2 * hs)):
                    ref = jnp.where(r8 >= blk * 2 * hs, _row_bcast(pi, blk * 2 * hs + hs - 1), ref)
                lvl.append(-jnp.abs(pi - ref))
        exps.append(lvl)
    return exps, b, off[n]


def _scan_kernel(q_ref, k_ref, gl_ref, v_ref, gt_ref, gw_ref, o_ref,
                 st_scr, attn_scr, qd_scr, kd_scr, dec_scr):
    gw = gw_ref[...]
    nt = CHUNK // SUBLANES
    n_groups = SEQ // (CHUNK * SCAN_GROUP)
    n_levels = len(LEVEL_SIZES)
    ti = lax.broadcasted_iota(jnp.int32, (CHUNK, CHUNK), 0)
    si = lax.broadcasted_iota(jnp.int32, (CHUNK, CHUNK), 1)
    owner = jnp.where(ti == si, n_levels, -1)
    for idx, hs in enumerate(LEVEL_SIZES):
        owner = jnp.where(((ti // (2 * hs)) == (si // (2 * hs))) & ((ti // hs) > (si // hs)),
                          idx, owner)
    st_scr[...] = jnp.zeros_like(st_scr)

    def tiles(a):
        return [a[i * SUBLANES:(i + 1) * SUBLANES, :] for i in range(nt)]

    owner_t = tiles(owner)

    def scaled(ab, e_tiles):
        return ab * jnp.exp2(jnp.concatenate(e_tiles, axis=0)).astype(BF16)

    def rows_of(c):
        return pl.ds(pl.multiple_of(c * CHUNK, CHUNK), CHUNK)

    def cols_of(hd):
        return slice(hd * HEAD_DIM, (hd + 1) * HEAD_DIM)

    def scores(hd, c):
        rows, cols = rows_of(c), cols_of(hd)
        qb = q_ref[0, rows, cols]
        kb = k_ref[0, rows, cols]
        exps, b, b_last = _level_exponents(tiles(gl_ref[0, rows, cols] * LOG2E))

        attn = [jnp.where(o == n_levels, d, 0.0) for o, d in zip(owner_t, tiles(_dot_nt(qb, kb)))]
        for idx, (hs, e) in enumerate(zip(LEVEL_SIZES, exps)):
            lvl = tiles(_dot_nt(scaled(qb, e), scaled(kb, e)))
            for i in range(nt):
                if hs < SUBLANES or (i // (hs // SUBLANES)) % 2 == 1:
                    attn[i] = jnp.where(owner_t[i] == idx, lvl[i], attn[i])
        attn_scr[c] = jnp.concatenate(attn, axis=0).astype(BF16)
        qd_scr[rows, :] = scaled(qb, b)
        kd_scr[rows, :] = scaled(kb, [b_last - bi for bi in b])
        dec_scr[c] = jnp.exp2(b_last)

    def step(out_of, scores_of):
        prev, cur = [], []
        if out_of is not None:
            hp, gp = out_of
            prev = [gp * SCAN_GROUP + j for j in range(SCAN_GROUP)]
        if scores_of is not None:
            hc, gc = scores_of
            cur = [gc * SCAN_GROUP + j for j in range(SCAN_GROUP)]
        half = len(cur) // 2
        if prev:
            vbs = [v_ref[0, rows_of(c), cols_of(hp)] for c in prev]
            intra = [jnp.dot(attn_scr[c], vb, preferred_element_type=F32)
                     for c, vb in zip(prev, vbs)]
            kv = [_dot_tn(vb, kd_scr[rows_of(c), :]) for c, vb in zip(prev, vbs)]
        for c in cur[:half]:
            scores(hc, c)
        if prev:
            st = st_scr[...]
            inter = []
            for j, c in enumerate(prev):
                inter.append(_dot_nt(qd_scr[rows_of(c), :], st.astype(BF16)))
                st = dec_scr[c][0:1, :] * st + kv[j]
            last_of_head = isinstance(gp, int) and gp == n_groups - 1
            st_scr[...] = jnp.zeros_like(st) if last_of_head else st
        for c in cur[half:]:
            scores(hc, c)
        if prev:
            for j, c in enumerate(prev):
                o = intra[j] + inter[j]
                ms = jnp.mean(o * o, axis=-1, keepdims=True)
                gate = gt_ref[0, rows_of(c), cols_of(hp)].astype(F32)
                y = o * lax.rsqrt(ms + EPS) * gw * gate
                o_ref[0, rows_of(c), cols_of(hp)] = y.astype(BF16)

    n_heads = q_ref.shape[2] // HEAD_DIM
    step(None, (0, 0))
    for hd in range(n_heads):
        def body(g, carry, hd=hd):
            step((hd, g - 1), (hd, g))
            return carry

        lax.fori_loop(1, n_groups, body, 0)
        step((hd, n_groups - 1), (hd + 1, 0) if hd + 1 < n_heads else None)


def _scan(q3, k3, gl3, v3, gt3, gnorm_w):
    blk = pl.BlockSpec((1, SEQ, SCAN_HEADS * HEAD_DIM), lambda b, h: (b, 0, h))
    return pl.pallas_call(
        _scan_kernel,
        grid=(BATCH, HGRN_HEADS // SCAN_HEADS),
        in_specs=[blk, blk, blk, blk, blk, pl.BlockSpec((1, HEAD_DIM), lambda b, h: (0, 0))],
        out_specs=blk,
        out_shape=jax.ShapeDtypeStruct((BATCH, SEQ, HGRN_WIDTH), BF16),
        scratch_shapes=[
            pltpu.VMEM((HEAD_DIM, HEAD_DIM), F32),
            pltpu.VMEM((SEQ // CHUNK, CHUNK, CHUNK), BF16),
            pltpu.VMEM((SEQ, HEAD_DIM), BF16),
            pltpu.VMEM((SEQ, HEAD_DIM), BF16),
            pltpu.VMEM((SEQ // CHUNK, SUBLANES, HEAD_DIM), F32),
        ],
        compiler_params=pltpu.CompilerParams(
            dimension_semantics=("arbitrary", "arbitrary"), vmem_limit_bytes=VMEM_LIMIT),
        name="hgrn_scan",
    )(q3, k3, gl3, v3, gt3, gnorm_w)


def _wout_kernel(x_ref, yp_ref, yh_ref, wp_ref, wh_ref, gt_ref, o_ref):
    mix = (jnp.dot(yp_ref[...], wp_ref[...], preferred_element_type=F32)
           + jnp.dot(yh_ref[...], wh_ref[...], preferred_element_type=F32))
    o_ref[...] = x_ref[...] + gt_ref[0] * mix


def _wout(x2d, yp, yh, w_out, mod3):
    m = x2d.shape[0]
    tm = 512
    tiles_per_batch = SEQ // tm
    return pl.pallas_call(
        _wout_kernel,
        grid=(m // tm,),
        in_specs=[
            pl.BlockSpec((tm, D_MODEL), lambda i: (i, 0)),
            pl.BlockSpec((tm, POOL_WIDTH), lambda i: (i, 0)),
            pl.BlockSpec((tm, HGRN_WIDTH), lambda i: (i, 0)),
            pl.BlockSpec((POOL_WIDTH, D_MODEL), lambda i: (0, 0)),
            pl.BlockSpec((HGRN_WIDTH, D_MODEL), lambda i: (1, 0)),
            pl.BlockSpec((1, 1, D_MODEL), lambda i: ((i // tiles_per_batch) * N_MOD + 5, 0, 0)),
        ],
        out_specs=pl.BlockSpec((tm, D_MODEL), lambda i: (i, 0)),
        out_shape=jax.ShapeDtypeStruct((m, D_MODEL), F32),
        compiler_params=pltpu.CompilerParams(
            dimension_semantics=("arbitrary",), vmem_limit_bytes=VMEM_LIMIT),
        name="mix_out",
    )(x2d, yp, yh, w_out, w_out, mod3)


def kernel(x, c, w_ada, b_ada, norm1_w, ffn1_gate, ffn1_up, ffn1_down, norm2_w, w_in, pool_w,
           pool_scale, lb_logits, gnorm_w, w_out, norm3_w, ffn2_gate, ffn2_up, ffn2_down,
           final_norm_w):
    B, S, D = x.shape
    assert (B, S, D) == (BATCH, SEQ, D_MODEL) and w_ada.shape[0] == 1
    assert lb_logits.shape == (2, HGRN_WIDTH)

    c_pad = jnp.pad(c, ((0, 8 - B), (0, 0)))
    mod = _ada(c_pad, w_ada[0], b_ada)[:B]
    mod3 = mod.reshape(B * N_MOD, 1, D)

    x2d = x.reshape(B * S, D)
    fw = final_norm_w.reshape(1, D)

    x1, w_in_b, w_out_b, g2_b, u2_b, d2_b = _ffn(
        x2d, norm1_w, mod3, 0, ffn1_gate[0].astype(BF16), ffn1_up[0].astype(BF16),
        ffn1_down[0].astype(BF16), fw, final_norm=False,
        cast=(w_in[0], w_out[0], ffn2_gate[0], ffn2_up[0], ffn2_down[0]))

    y_pool, q, k, gl, v, gt = _win(x1, norm2_w, mod3, w_in_b, lb_logits,
                                   pool_w[0].astype(BF16), pool_scale)
    y_hgrn = _scan(*(a.reshape(B, S, HGRN_WIDTH) for a in (q, k, gl, v, gt)), gnorm_w)
    x2 = _wout(x1, y_pool, y_hgrn.reshape(B * S, HGRN_WIDTH), w_out_b, mod3)

    out, = _ffn(x2, norm3_w, mod3, 6, g2_b, u2_b, d2_b, fw, final_norm=True)
    return out.reshape(B, S, D)
```

```python
import functools

import jax
import jax.numpy as jnp
from jax import lax
from jax.experimental import pallas as pl
from jax.experimental.pallas import tpu as pltpu

F32 = jnp.float32
BF16 = jnp.bfloat16

D_MODEL = 2048
BATCH = 4
SEQ = 2048
POOL_WIDTH = 1024
POOL_WINDOWS = (2, 4, 8, 16)
POOL_GROUP_DIM = 256
HGRN_WIDTH = 1024
HGRN_HEADS = 8
HEAD_DIM = 128
IN_WIDTH = 5120
CHUNK = 64
SUBLANES = 8
BF16_SUBLANES = 16
WIN_SUB_ROWS = 256
FFN_SUB_ROWS = 512
FFN_MID_SUB_ROWS = 512
POOL_HALO = 16
D_FF = 5632
N_MOD = 9
EPS = 1e-6

VMEM_LIMIT = 60 * 1024 * 1024


def _silu(v):
    return v * jax.nn.sigmoid(v)


def _norm_modulate(x, nw, sh, sc):
    ms = jnp.mean(x * x, axis=-1, keepdims=True)
    y = x * lax.rsqrt(ms + EPS) * nw
    return y * (1.0 + sc) + sh


def _ada_kernel(c_ref, w_ref, b_ref, o_ref):
    ca = _silu(c_ref[...]).astype(BF16)
    o_ref[...] = jnp.dot(ca, w_ref[...].astype(BF16), preferred_element_type=F32) + b_ref[...]


def _ada(c_pad, w_ada, b_ada):
    n = w_ada.shape[1]
    tn = 1024
    return pl.pallas_call(
        _ada_kernel,
        grid=(n // tn,),
        in_specs=[
            pl.BlockSpec((8, D_MODEL), lambda j: (0, 0)),
            pl.BlockSpec((D_MODEL, tn), lambda j: (0, j)),
            pl.BlockSpec((1, tn), lambda j: (0, j)),
        ],
        out_specs=pl.BlockSpec((8, tn), lambda j: (0, j)),
        out_shape=jax.ShapeDtypeStruct((8, n), F32),
        compiler_params=pltpu.CompilerParams(
            dimension_semantics=("arbitrary",), vmem_limit_bytes=VMEM_LIMIT),
        name="ada_mod",
    )(c_pad, w_ada, b_ada)


def _ffn_kernel(*refs, nj, final_norm, n_cast):
    x_ref, nw_ref, sh_ref, sc_ref, gt_ref, wg_ref, wu_ref, wd_ref, fw_ref = refs[:9]
    cast_in = refs[9:9 + n_cast]
    o_ref = refs[9 + n_cast]
    cast_out = refs[10 + n_cast:10 + 2 * n_cast]
    h_scr, = refs[10 + 2 * n_cast:]
    j = pl.program_id(1)
    tm = x_ref.shape[0]
    subs = [pl.ds(r, FFN_SUB_ROWS) for r in range(0, tm, FFN_SUB_ROWS)]
    mid_subs = [pl.ds(r, FFN_MID_SUB_ROWS) for r in range(0, tm, FFN_MID_SUB_ROWS)]

    for w_ref, wb_ref in zip(cast_in, cast_out):
        wb_ref[...] = w_ref[...].astype(BF16)

    def partial_ffn(rows):
        h = h_scr[rows, :]
        g = jnp.dot(h, wg_ref[...], preferred_element_type=F32)
        u = jnp.dot(h, wu_ref[...], preferred_element_type=F32)
        a = (_silu(g) * u).astype(BF16)
        return jnp.dot(a, wd_ref[...], preferred_element_type=F32)

    @pl.when(j == 0)
    def _():
        for rows in subs:
            h = _norm_modulate(x_ref[rows, :], nw_ref[...], sh_ref[0], sc_ref[0])
            h_scr[rows, :] = h.astype(BF16)
            o_ref[rows, :] = partial_ffn(rows)

    @pl.when((j > 0) & (j < nj - 1))
    def _():
        for rows in mid_subs:
            o_ref[rows, :] += partial_ffn(rows)

    @pl.when(j == nj - 1)
    def _():
        for rows in subs:
            out = x_ref[rows, :] + 0.5 * gt_ref[0] * (o_ref[rows, :] + partial_ffn(rows))
            if final_norm:
                ms = jnp.mean(out * out, axis=-1, keepdims=True)
                out = out * lax.rsqrt(ms + EPS) * fw_ref[...]
            o_ref[rows, :] = out


def _ffn(x2d, nw, mod3, mod_base, wg, wu, wd, fw, *, final_norm, cast=()):
    m = x2d.shape[0]
    tm, tf = 1024, 512
    ni, nj = m // tm, D_FF // tf
    tiles_per_batch = SEQ // tm

    def mod_map(k):
        return lambda i, j: ((i // tiles_per_batch) * N_MOD + mod_base + k, 0, 0)

    cast_specs = []
    for w in cast:
        rows = w.shape[0]
        nblk = max(n for n in range(1, ni * nj + 1) if rows % (BF16_SUBLANES * n) == 0)
        cast_specs.append(pl.BlockSpec(
            (rows // nblk, w.shape[1]),
            lambda i, j, nblk=nblk: (jnp.minimum(i * nj + j, nblk - 1), 0)))

    kern = functools.partial(_ffn_kernel, nj=nj, final_norm=final_norm, n_cast=len(cast))
    res = pl.pallas_call(
        kern,
        grid=(ni, nj),
        in_specs=[
            pl.BlockSpec((tm, D_MODEL), lambda i, j: (i, 0)),
            pl.BlockSpec((1, D_MODEL), lambda i, j: (0, 0)),
            pl.BlockSpec((1, 1, D_MODEL), mod_map(0)),
            pl.BlockSpec((1, 1, D_MODEL), mod_map(1)),
            pl.BlockSpec((1, 1, D_MODEL), mod_map(2)),
            pl.BlockSpec((D_MODEL, tf), lambda i, j: (0, j)),
            pl.BlockSpec((D_MODEL, tf), lambda i, j: (0, j)),
            pl.BlockSpec((tf, D_MODEL), lambda i, j: (j, 0)),
            pl.BlockSpec((1, D_MODEL), lambda i, j: (0, 0)),
        ] + cast_specs,
        out_specs=[pl.BlockSpec((tm, D_MODEL), lambda i, j: (i, 0))] + cast_specs,
        out_shape=[jax.ShapeDtypeStruct((m, D_MODEL), F32)]
        + [jax.ShapeDtypeStruct(w.shape, BF16) for w in cast],
        scratch_shapes=[pltpu.VMEM((tm, D_MODEL), BF16)],
        compiler_params=pltpu.CompilerParams(
            dimension_semantics=("arbitrary", "arbitrary"), vmem_limit_bytes=VMEM_LIMIT),
        name="ffn_final" if final_norm else "ffn",
    )(x2d, nw, mod3, mod3, mod3, wg, wu, wd, fw, *cast)
    return res


def _pool_group(ext, u, t_abs, window, pw, ps):
    s, k = ext, 1
    while k < window:
        s = s + pltpu.roll(s, k, axis=0)
        k *= 2
    cnt = jnp.minimum(t_abs + 1, window).astype(F32)
    pooled = s[POOL_HALO:, :] * (1.0 / cnt) - u
    return jnp.dot(pooled.astype(BF16), pw, preferred_element_type=F32) * ps


def _win_kernel(x_ref, nw_ref, sh_ref, sc_ref, w_ref, lbl_ref, pw_ref, ps_ref,
                yp_ref, q_ref, k_ref, gl_ref, v_ref, gt_ref, tail_scr, *, tiles_per_batch):
    tm, tn = yp_ref.shape
    cg = POOL_GROUP_DIM
    lbl = lbl_ref[...]
    pe = jnp.exp(lbl - jnp.max(lbl, axis=0, keepdims=True))
    p = pe / jnp.sum(pe, axis=0, keepdims=True)
    lb = (p[0:1, :] + p[1:2, :]) - p[0:1, :]

    tile_in_seq = pl.program_id(0) % tiles_per_batch

    @pl.when(tile_in_seq == 0)
    def _():
        tail_scr[...] = jnp.zeros_like(tail_scr)

    tail = tail_scr[...]
    row = lax.broadcasted_iota(jnp.int32, (WIN_SUB_ROWS, 1), 0)

    for r in range(0, tm, WIN_SUB_ROWS):
        rows = pl.ds(r, WIN_SUB_ROWS)
        h = _norm_modulate(x_ref[rows, :], nw_ref[...], sh_ref[0], sc_ref[0]).astype(BF16)

        def project(seg):
            return jnp.dot(h, w_ref[:, seg * tn:(seg + 1) * tn], preferred_element_type=F32)

        u = project(0)
        ext = jnp.concatenate([tail, u], axis=0)
        tail = u[WIN_SUB_ROWS - POOL_HALO:, :]
        t_abs = tile_in_seq * tm + r + row
        for g, window in enumerate(POOL_WINDOWS):
            cols = slice(g * cg, (g + 1) * cg)
            y = _pool_group(ext[:, cols], u[:, cols], t_abs, window, pw_ref[g], ps_ref[:, cols])
            yp_ref[rows, cols] = y.astype(BF16)
        q_ref[rows, :] = _silu(project(1)).astype(BF16)
        forget = lb + (1.0 - lb) * jax.nn.sigmoid(project(2))
        k_ref[rows, :] = (1.0 - forget).astype(BF16)
        gl_ref[rows, :] = jnp.log(forget)
        v_ref[rows, :] = project(3).astype(BF16)
        gt_ref[rows, :] = _silu(project(4)).astype(BF16)
    tail_scr[...] = tail


def _win(x2d, nw, mod3, w_in, lb_logits, pool_w, pool_scale):
    m = x2d.shape[0]
    tm, tn = 512, 1024
    assert IN_WIDTH == 5 * tn and POOL_WIDTH == tn and HGRN_WIDTH == tn
    assert max(POOL_WINDOWS) <= POOL_HALO and tn == len(POOL_WINDOWS) * POOL_GROUP_DIM
    tiles_per_batch = SEQ // tm

    def mod_map(k):
        return lambda i: ((i // tiles_per_batch) * N_MOD + 3 + k, 0, 0)

    seg = pl.BlockSpec((tm, tn), lambda i: (i, 0))
    return pl.pallas_call(
        functools.partial(_win_kernel, tiles_per_batch=tiles_per_batch),
        grid=(m // tm,),
        in_specs=[
            pl.BlockSpec((tm, D_MODEL), lambda i: (i, 0)),
            pl.BlockSpec((1, D_MODEL), lambda i: (0, 0)),
            pl.BlockSpec((1, 1, D_MODEL), mod_map(0)),
            pl.BlockSpec((1, 1, D_MODEL), mod_map(1)),
            pl.BlockSpec((D_MODEL, IN_WIDTH), lambda i: (0, 0), pipeline_mode=pl.Buffered(1)),
            pl.BlockSpec((2, tn), lambda i: (0, 0)),
            pl.BlockSpec(pool_w.shape, lambda i: (0, 0, 0)),
            pl.BlockSpec((1, tn), lambda i: (0, 0)),
        ],
        out_specs=[seg] * 6,
        out_shape=[
            jax.ShapeDtypeStruct((m, tn), BF16),
            jax.ShapeDtypeStruct((m, tn), BF16),
            jax.ShapeDtypeStruct((m, tn), BF16),
            jax.ShapeDtypeStruct((m, tn), F32),
            jax.ShapeDtypeStruct((m, tn), BF16),
            jax.ShapeDtypeStruct((m, tn), BF16),
        ],
        scratch_shapes=[pltpu.VMEM((POOL_HALO, tn), F32)],
        compiler_params=pltpu.CompilerParams(
            dimension_semantics=("arbitrary",), vmem_limit_bytes=VMEM_LIMIT),
        name="mix_in",
    )(x2d, nw, mod3, mod3, w_in, lb_logits, pool_w, pool_scale)


def _dot_nt(a, b):
    return lax.dot_general(a, b, (((1,), (1,)), ((), ())), preferred_element_type=F32)


def _dot_tn(a, b):
    return lax.dot_general(a, b, (((0,), (0,)), ((), ())), preferred_element_type=F32)


LEVEL_SIZES = (32, 16, 8, 4, 2, 1)
SCAN_GROUP = 8
SCAN_HEADS = 4
LOG2E = 1.4426950408889634


def _row_bcast(v, r):
    return jnp.broadcast_to(v[r:r + 1, :], v.shape)


def _level_exponents(g):
    n = len(g)
    r8 = lax.broadcasted_iota(jnp.int32, (SUBLANES, 1), 0)
    p = list(g)
    for s in (1, 2, 4):
        p = [pi + jnp.where(r8 >= s, pltpu.roll(pi, s, axis=0), 0.0) for pi in p]
    off = [None, _row_bcast(p[0], SUBLANES - 1)]
    for i in range(1, n):
        off.append(off[i] + _row_bcast(p[i], SUBLANES - 1))
    b = [p[0]] + [p[i] + off[i] for i in range(1, n)]

    exps = []
    for hs in LEVEL_SIZES:
        if hs >= SUBLANES:
            w = hs // SUBLANES
            lvl = []
            for i in range(n):
                ref = off[(i // (2 * w)) * 2 * w + w]
                lvl.append(b[i] - ref if (i // w) % 2 == 1 else ref - b[i])
        elif hs == 1:
            lvl = [jnp.where(r8 % 2 == 1, gi, 0.0) for gi in g]
        else:
            lvl = []
            for pi in p:
                ref = _row_bcast(pi, hs - 1)
                for blk in range(1, SUBLANES // (2 * hs)):
                    ref = jnp.where(r8 >= blk * 2 * hs, _row_bcast(pi, blk * 2 * hs + hs - 1), ref)
                lvl.append(-jnp.abs(pi - ref))
        exps.append(lvl)
    return exps, b, off[n]


def _scan_kernel(q_ref, k_ref, gl_ref, v_ref, gt_ref, gw_ref, o_ref,
                 st_scr, attn_scr, qd_scr, kd_scr, dec_scr):
    gw = gw_ref[...]
    nt = CHUNK // SUBLANES
    n_groups = SEQ // (CHUNK * SCAN_GROUP)
    n_levels = len(LEVEL_SIZES)
    ti = lax.broadcasted_iota(jnp.int32, (CHUNK, CHUNK), 0)
    si = lax.broadcasted_iota(jnp.int32, (CHUNK, CHUNK), 1)
    owner = jnp.where(ti == si, n_levels, -1)
    for idx, hs in enumerate(LEVEL_SIZES):
        owner = jnp.where(((ti // (2 * hs)) == (si // (2 * hs))) & ((ti // hs) > (si // hs)),
                          idx, owner)
    st_scr[...] = jnp.zeros_like(st_scr)

    def tiles(a):
        return [a[i * SUBLANES:(i + 1) * SUBLANES, :] for i in range(nt)]

    owner_t = tiles(owner)

    def scaled(a_tiles, e_tiles, keep=None):
        zero = jnp.zeros_like(a_tiles[0])
        return jnp.concatenate(
            [a * jnp.exp2(e) if keep is None or keep[i] else zero
             for i, (a, e) in enumerate(zip(a_tiles, e_tiles))], axis=0).astype(BF16)

    def rows_of(c):
        return pl.ds(pl.multiple_of(c * CHUNK, CHUNK), CHUNK)

    def cols_of(hd):
        return slice(hd * HEAD_DIM, (hd + 1) * HEAD_DIM)

    def scores(hd, c):
        rows, cols = rows_of(c), cols_of(hd)
        qb = q_ref[0, rows, cols]
        kb = k_ref[0, rows, cols]
        q = tiles(qb.astype(F32))
        k = tiles(kb.astype(F32))
        exps, b, b_last = _level_exponents(tiles(gl_ref[0, rows, cols] * LOG2E))

        attn = [jnp.where(o == n_levels, d, 0.0) for o, d in zip(owner_t, tiles(_dot_nt(qb, kb)))]
        for idx, (hs, e) in enumerate(zip(LEVEL_SIZES, exps)):
            if hs >= SUBLANES:
                second = [(i // (hs // SUBLANES)) % 2 == 1 for i in range(nt)]
                first = [not s for s in second]
            else:
                second, first = [True] * nt, None
            lvl = tiles(_dot_nt(scaled(q, e, second), scaled(k, e, first)))
            for i in range(nt):
                if second[i]:
                    attn[i] = jnp.where(owner_t[i] == idx, lvl[i], attn[i])
        attn_scr[c] = jnp.concatenate(attn, axis=0).astype(BF16)
        qd_scr[rows, :] = scaled(q, b)
        kd_scr[rows, :] = scaled(k, [b_last - bi for bi in b])
        dec_scr[c] = jnp.exp2(b_last)

    def step(out_of, scores_of):
        prev, cur = [], []
        if out_of is not None:
            hp, gp = out_of
            prev = [gp * SCAN_GROUP + j for j in range(SCAN_GROUP)]
        if scores_of is not None:
            hc, gc = scores_of
            cur = [gc * SCAN_GROUP + j for j in range(SCAN_GROUP)]
        n = len(cur)
        cut = [0, n // 4, n // 2, n]
        if prev:
            vbs = [v_ref[0, rows_of(c), cols_of(hp)] for c in prev]
            intra = [jnp.dot(attn_scr[c], vb, preferred_element_type=F32)
                     for c, vb in zip(prev, vbs)]
            kv = [_dot_tn(vb, kd_scr[rows_of(c), :]) for c, vb in zip(prev, vbs)]
        for c in cur[cut[0]:cut[1]]:
            scores(hc, c)
        if prev:
            st = st_scr[...]
            inter = []
            for j, c in enumerate(prev):
                inter.append(_dot_nt(qd_scr[rows_of(c), :], st.astype(BF16)))
                st = dec_scr[c][0:1, :] * st + kv[j]
            last_of_head = isinstance(gp, int) and gp == n_groups - 1
            st_scr[...] = jnp.zeros_like(st) if last_of_head else st
        for c in cur[cut[1]:cut[2]]:
            scores(hc, c)
        if prev:
            ys = []
            for j, c in enumerate(prev):
                o = intra[j] + inter[j]
                ms = jnp.mean(o * o, axis=-1, keepdims=True)
                gate = gt_ref[0, rows_of(c), cols_of(hp)].astype(F32)
                ys.append((o * lax.rsqrt(ms + EPS) * gw * gate).astype(BF16))
        for c in cur[cut[2]:cut[3]]:
            scores(hc, c)
        if prev:
            for y, c in zip(ys, prev):
                o_ref[0, rows_of(c), cols_of(hp)] = y

    n_heads = q_ref.shape[2] // HEAD_DIM
    step(None, (0, 0))
    for hd in range(n_heads):
        def body(g, carry, hd=hd):
            step((hd, g - 1), (hd, g))
            return carry

        lax.fori_loop(1, n_groups, body, 0)
        step((hd, n_groups - 1), (hd + 1, 0) if hd + 1 < n_heads else None)


def _scan(q3, k3, gl3, v3, gt3, gnorm_w):
    blk = pl.BlockSpec((1, SEQ, SCAN_HEADS * HEAD_DIM), lambda b, h: (b, 0, h))
    return pl.pallas_call(
        _scan_kernel,
        grid=(BATCH, HGRN_HEADS // SCAN_HEADS),
        in_specs=[blk, blk, blk, blk, blk, pl.BlockSpec((1, HEAD_DIM), lambda b, h: (0, 0))],
        out_specs=blk,
        out_shape=jax.ShapeDtypeStruct((BATCH, SEQ, HGRN_WIDTH), BF16),
        scratch_shapes=[
            pltpu.VMEM((HEAD_DIM, HEAD_DIM), F32),
            pltpu.VMEM((SEQ // CHUNK, CHUNK, CHUNK), BF16),
            pltpu.VMEM((SEQ, HEAD_DIM), BF16),
            pltpu.VMEM((SEQ, HEAD_DIM), BF16),
            pltpu.VMEM((SEQ // CHUNK, SUBLANES, HEAD_DIM), F32),
        ],
        compiler_params=pltpu.CompilerParams(
            dimension_semantics=("arbitrary", "arbitrary"), vmem_limit_bytes=VMEM_LIMIT),
        name="hgrn_scan",
    )(q3, k3, gl3, v3, gt3, gnorm_w)


def _wout_kernel(x_ref, yp_ref, yh_ref, wp_ref, wh_ref, gt_ref, o_ref):
    mix = (jnp.dot(yp_ref[...], wp_ref[...], preferred_element_type=F32)
           + jnp.dot(yh_ref[...], wh_ref[...], preferred_element_type=F32))
    o_ref[...] = x_ref[...] + gt_ref[0] * mix


def _wout(x2d, yp, yh, w_out, mod3):
    m = x2d.shape[0]
    tm = 512
    tiles_per_batch = SEQ // tm
    return pl.pallas_call(
        _wout_kernel,
        grid=(m // tm,),
        in_specs=[
            pl.BlockSpec((tm, D_MODEL), lambda i: (i, 0)),
            pl.BlockSpec((tm, POOL_WIDTH), lambda i: (i, 0)),
            pl.BlockSpec((tm, HGRN_WIDTH), lambda i: (i, 0)),
            pl.BlockSpec((POOL_WIDTH, D_MODEL), lambda i: (0, 0)),
            pl.BlockSpec((HGRN_WIDTH, D_MODEL), lambda i: (1, 0)),
            pl.BlockSpec((1, 1, D_MODEL), lambda i: ((i // tiles_per_batch) * N_MOD + 5, 0, 0)),
        ],
        out_specs=pl.BlockSpec((tm, D_MODEL), lambda i: (i, 0)),
        out_shape=jax.ShapeDtypeStruct((m, D_MODEL), F32),
        compiler_params=pltpu.CompilerParams(
            dimension_semantics=("arbitrary",), vmem_limit_bytes=VMEM_LIMIT),
        name="mix_out",
    )(x2d, yp, yh, w_out, w_out, mod3)


def kernel(x, c, w_ada, b_ada, norm1_w, ffn1_gate, ffn1_up, ffn1_down, norm2_w, w_in, pool_w,
           pool_scale, lb_logits, gnorm_w, w_out, norm3_w, ffn2_gate, ffn2_up, ffn2_down,
           final_norm_w):
    B, S, D = x.shape
    assert (B, S, D) == (BATCH, SEQ, D_MODEL) and w_ada.shape[0] == 1
    assert lb_logits.shape == (2, HGRN_WIDTH)

    c_pad = jnp.pad(c, ((0, 8 - B), (0, 0)))
    mod = _ada(c_pad, w_ada[0], b_ada)[:B]
    mod3 = mod.reshape(B * N_MOD, 1, D)

    x2d = x.reshape(B * S, D)
    fw = final_norm_w.reshape(1, D)

    x1, w_in_b, w_out_b, g2_b, u2_b, d2_b = _ffn(
        x2d, norm1_w, mod3, 0, ffn1_gate[0].astype(BF16), ffn1_up[0].astype(BF16),
        ffn1_down[0].astype(BF16), fw, final_norm=False,
        cast=(w_in[0], w_out[0], ffn2_gate[0], ffn2_up[0], ffn2_down[0]))

    y_pool, q, k, gl, v, gt = _win(x1, norm2_w, mod3, w_in_b, lb_logits,
                                   pool_w[0].astype(BF16), pool_scale)
    y_hgrn = _scan(*(a.reshape(B, S, HGRN_WIDTH) for a in (q, k, gl, v, gt)), gnorm_w)
    x2 = _wout(x1, y_pool, y_hgrn.reshape(B * S, HGRN_WIDTH), w_out_b, mod3)

    out, = _ffn(x2, norm3_w, mod3, 6, g2_b, u2_b, d2_b, fw, final_norm=True)
    return out.reshape(B, S, D)
```

```python
import functools

import jax
import jax.numpy as jnp
from jax import lax
from jax.experimental import pallas as pl
from jax.experimental.pallas import tpu as pltpu

F32 = jnp.float32
BF16 = jnp.bfloat16

D_MODEL = 2048
BATCH = 4
SEQ = 2048
POOL_WIDTH = 1024
POOL_WINDOWS = (2, 4, 8, 16)
POOL_GROUP_DIM = 256
HGRN_WIDTH = 1024
HGRN_HEADS = 8
HEAD_DIM = 128
IN_WIDTH = 5120
CHUNK = 64
SUBLANES = 8
BF16_SUBLANES = 16
WIN_SUB_ROWS = 256
FFN_SUB_ROWS = 512
FFN_MID_SUB_ROWS = 512
POOL_HALO = 16
D_FF = 5632
N_MOD = 9
MOD_EARLY = 5
ADA_COLS = 1024
EPS = 1e-6

VMEM_LIMIT = 60 * 1024 * 1024


def _silu(v):
    return v * jax.nn.sigmoid(v)


def _norm_modulate(x, nw, sh, sc):
    ms = jnp.mean(x * x, axis=-1, keepdims=True)
    y = x * lax.rsqrt(ms + EPS) * nw
    return y * (1.0 + sc) + sh


def _ada_block(c_ref, w_ref, b_ref):
    ca = _silu(c_ref[...]).astype(BF16)
    return jnp.dot(ca, w_ref[...].astype(BF16), preferred_element_type=F32) + b_ref[...]


def _ada_kernel(c_ref, w_ref, b_ref, o_ref):
    o_ref[...] = _ada_block(c_ref, w_ref, b_ref)


def _ada(c_pad, w_ada, b_ada, n):
    tn = ADA_COLS
    return pl.pallas_call(
        _ada_kernel,
        grid=(n // tn,),
        in_specs=[
            pl.BlockSpec((8, D_MODEL), lambda j: (0, 0)),
            pl.BlockSpec((D_MODEL, tn), lambda j: (0, j)),
            pl.BlockSpec((1, tn), lambda j: (0, j)),
        ],
        out_specs=pl.BlockSpec((8, tn), lambda j: (0, j)),
        out_shape=jax.ShapeDtypeStruct((8, n), F32),
        compiler_params=pltpu.CompilerParams(
            dimension_semantics=("arbitrary",), vmem_limit_bytes=VMEM_LIMIT),
        name="ada_mod",
    )(c_pad, w_ada, b_ada)


def _ffn_kernel(*refs, nj, final_norm, n_cast):
    x_ref, nw_ref, sh_ref, sc_ref, gt_ref, wg_ref, wu_ref, wd_ref, fw_ref = refs[:9]
    cast_in = refs[9:9 + n_cast]
    o_ref = refs[9 + n_cast]
    cast_out = refs[10 + n_cast:10 + 2 * n_cast]
    h_scr, = refs[10 + 2 * n_cast:]
    j = pl.program_id(1)
    tm = x_ref.shape[0]
    subs = [pl.ds(r, FFN_SUB_ROWS) for r in range(0, tm, FFN_SUB_ROWS)]
    mid_subs = [pl.ds(r, FFN_MID_SUB_ROWS) for r in range(0, tm, FFN_MID_SUB_ROWS)]

    for w_ref, wb_ref in zip(cast_in, cast_out):
        wb_ref[...] = w_ref[...].astype(BF16)

    def partial_ffn(rows):
        h = h_scr[rows, :]
        g = jnp.dot(h, wg_ref[...], preferred_element_type=F32)
        u = jnp.dot(h, wu_ref[...], preferred_element_type=F32)
        a = (_silu(g) * u).astype(BF16)
        return jnp.dot(a, wd_ref[...], preferred_element_type=F32)

    @pl.when(j == 0)
    def _():
        for rows in subs:
            h = _norm_modulate(x_ref[rows, :], nw_ref[...], sh_ref[0], sc_ref[0])
            h_scr[rows, :] = h.astype(BF16)
            o_ref[rows, :] = partial_ffn(rows)

    @pl.when((j > 0) & (j < nj - 1))
    def _():
        for rows in mid_subs:
            o_ref[rows, :] += partial_ffn(rows)

    @pl.when(j == nj - 1)
    def _():
        for rows in subs:
            out = x_ref[rows, :] + 0.5 * gt_ref[0] * (o_ref[rows, :] + partial_ffn(rows))
            if final_norm:
                ms = jnp.mean(out * out, axis=-1, keepdims=True)
                out = out * lax.rsqrt(ms + EPS) * fw_ref[...]
            o_ref[rows, :] = out


def _ffn(x2d, nw, mod3, mod_base, wg, wu, wd, fw, *, final_norm, cast=()):
    m = x2d.shape[0]
    tm, tf = 1024, 512
    ni, nj = m // tm, D_FF // tf
    tiles_per_batch = SEQ // tm
    mod_rows = mod3.shape[0] // BATCH

    def mod_map(k):
        return lambda i, j: ((i // tiles_per_batch) * mod_rows + mod_base + k, 0, 0)

    cast_specs = []
    for w in cast:
        rows = w.shape[0]
        nblk = max(n for n in range(1, ni * nj + 1) if rows % (BF16_SUBLANES * n) == 0)
        cast_specs.append(pl.BlockSpec(
            (rows // nblk, w.shape[1]),
            lambda i, j, nblk=nblk: (jnp.minimum(i * nj + j, nblk - 1), 0)))

    kern = functools.partial(_ffn_kernel, nj=nj, final_norm=final_norm, n_cast=len(cast))
    res = pl.pallas_call(
        kern,
        grid=(ni, nj),
        in_specs=[
            pl.BlockSpec((tm, D_MODEL), lambda i, j: (i, 0)),
            pl.BlockSpec((1, D_MODEL), lambda i, j: (0, 0)),
            pl.BlockSpec((1, 1, D_MODEL), mod_map(0)),
            pl.BlockSpec((1, 1, D_MODEL), mod_map(1)),
            pl.BlockSpec((1, 1, D_MODEL), mod_map(2)),
            pl.BlockSpec((D_MODEL, tf), lambda i, j: (0, j)),
            pl.BlockSpec((D_MODEL, tf), lambda i, j: (0, j)),
            pl.BlockSpec((tf, D_MODEL), lambda i, j: (j, 0)),
            pl.BlockSpec((1, D_MODEL), lambda i, j: (0, 0)),
        ] + cast_specs,
        out_specs=[pl.BlockSpec((tm, D_MODEL), lambda i, j: (i, 0))] + cast_specs,
        out_shape=[jax.ShapeDtypeStruct((m, D_MODEL), F32)]
        + [jax.ShapeDtypeStruct(w.shape, BF16) for w in cast],
        scratch_shapes=[pltpu.VMEM((tm, D_MODEL), BF16)],
        compiler_params=pltpu.CompilerParams(
            dimension_semantics=("arbitrary", "arbitrary"), vmem_limit_bytes=VMEM_LIMIT),
        name="ffn_final" if final_norm else "ffn",
    )(x2d, nw, mod3, mod3, mod3, wg, wu, wd, fw, *cast)
    return res


def _pool_group(ext, u, t_abs, window, pw, ps):
    s, k = ext, 1
    while k < window:
        s = s + pltpu.roll(s, k, axis=0)
        k *= 2
    cnt = jnp.minimum(t_abs + 1, window).astype(F32)
    pooled = s[POOL_HALO:, :] * (1.0 / cnt) - u
    return jnp.dot(pooled.astype(BF16), pw, preferred_element_type=F32) * ps


def _win_kernel(x_ref, nw_ref, sh_ref, sc_ref, w_ref, lbl_ref, pw_ref, ps_ref,
                yp_ref, q_ref, k_ref, gl_ref, v_ref, gt_ref, tail_scr, *, tiles_per_batch):
    tm, tn = yp_ref.shape
    cg = POOL_GROUP_DIM
    lbl = lbl_ref[...]
    pe = jnp.exp(lbl - jnp.max(lbl, axis=0, keepdims=True))
    p = pe / jnp.sum(pe, axis=0, keepdims=True)
    lb = (p[0:1, :] + p[1:2, :]) - p[0:1, :]

    tile_in_seq = pl.program_id(0) % tiles_per_batch

    @pl.when(tile_in_seq == 0)
    def _():
        tail_scr[...] = jnp.zeros_like(tail_scr)

    tail = tail_scr[...]
    row = lax.broadcasted_iota(jnp.int32, (WIN_SUB_ROWS, 1), 0)

    for r in range(0, tm, WIN_SUB_ROWS):
        rows = pl.ds(r, WIN_SUB_ROWS)
        h = _norm_modulate(x_ref[rows, :], nw_ref[...], sh_ref[0], sc_ref[0]).astype(BF16)

        def project(seg):
            return jnp.dot(h, w_ref[:, seg * tn:(seg + 1) * tn], preferred_element_type=F32)

        u = project(0)
        ext = jnp.concatenate([tail, u], axis=0)
        tail = u[WIN_SUB_ROWS - POOL_HALO:, :]
        t_abs = tile_in_seq * tm + r + row
        for g, window in enumerate(POOL_WINDOWS):
            cols = slice(g * cg, (g + 1) * cg)
            y = _pool_group(ext[:, cols], u[:, cols], t_abs, window, pw_ref[g], ps_ref[:, cols])
            yp_ref[rows, cols] = y.astype(BF16)
        q_ref[rows, :] = _silu(project(1)).astype(BF16)
        forget = lb + (1.0 - lb) * jax.nn.sigmoid(project(2))
        k_ref[rows, :] = (1.0 - forget).astype(BF16)
        gl_ref[rows, :] = jnp.log(forget)
        v_ref[rows, :] = project(3).astype(BF16)
        gt_ref[rows, :] = _silu(project(4)).astype(BF16)
    tail_scr[...] = tail


def _win(x2d, nw, mod3, mod_base, w_in, lb_logits, pool_w, pool_scale):
    m = x2d.shape[0]
    tm, tn = 512, 1024
    assert IN_WIDTH == 5 * tn and POOL_WIDTH == tn and HGRN_WIDTH == tn
    assert max(POOL_WINDOWS) <= POOL_HALO and tn == len(POOL_WINDOWS) * POOL_GROUP_DIM
    tiles_per_batch = SEQ // tm
    mod_rows = mod3.shape[0] // BATCH

    def mod_map(k):
        return lambda i: ((i // tiles_per_batch) * mod_rows + mod_base + k, 0, 0)

    seg = pl.BlockSpec((tm, tn), lambda i: (i, 0))
    return pl.pallas_call(
        functools.partial(_win_kernel, tiles_per_batch=tiles_per_batch),
        grid=(m // tm,),
        in_specs=[
            pl.BlockSpec((tm, D_MODEL), lambda i: (i, 0)),
            pl.BlockSpec((1, D_MODEL), lambda i: (0, 0)),
            pl.BlockSpec((1, 1, D_MODEL), mod_map(0)),
            pl.BlockSpec((1, 1, D_MODEL), mod_map(1)),
            pl.BlockSpec((D_MODEL, IN_WIDTH), lambda i: (0, 0), pipeline_mode=pl.Buffered(1)),
            pl.BlockSpec((2, tn), lambda i: (0, 0)),
            pl.BlockSpec(pool_w.shape, lambda i: (0, 0, 0)),
            pl.BlockSpec((1, tn), lambda i: (0, 0)),
        ],
        out_specs=[seg] * 6,
        out_shape=[
            jax.ShapeDtypeStruct((m, tn), BF16),
            jax.ShapeDtypeStruct((m, tn), BF16),
            jax.ShapeDtypeStruct((m, tn), BF16),
            jax.ShapeDtypeStruct((m, tn), F32),
            jax.ShapeDtypeStruct((m, tn), BF16),
            jax.ShapeDtypeStruct((m, tn), BF16),
        ],
        scratch_shapes=[pltpu.VMEM((POOL_HALO, tn), F32)],
        compiler_params=pltpu.CompilerParams(
            dimension_semantics=("arbitrary",), vmem_limit_bytes=VMEM_LIMIT),
        name="mix_in",
    )(x2d, nw, mod3, mod3, w_in, lb_logits, pool_w, pool_scale)


def _dot_nt(a, b):
    return lax.dot_general(a, b, (((1,), (1,)), ((), ())), preferred_element_type=F32)


def _dot_tn(a, b):
    return lax.dot_general(a, b, (((0,), (0,)), ((), ())), preferred_element_type=F32)


LEVEL_SIZES = (32, 16, 8, 4, 2, 1)
SCAN_GROUP = 8
SCAN_HEADS = 4
LOG2E = 1.4426950408889634


def _row_bcast(v, r):
    return jnp.broadcast_to(v[r:r + 1, :], v.shape)


def _level_exponents(g):
    n = len(g)
    r8 = lax.broadcasted_iota(jnp.int32, (SUBLANES, 1), 0)
    p = list(g)
    for s in (1, 2, 4):
        p = [pi + jnp.where(r8 >= s, pltpu.roll(pi, s, axis=0), 0.0) for pi in p]
    off = [None, _row_bcast(p[0], SUBLANES - 1)]
    for i in range(1, n):
        off.append(off[i] + _row_bcast(p[i], SUBLANES - 1))
    b = [p[0]] + [p[i] + off[i] for i in range(1, n)]

    exps = []
    for hs in LEVEL_SIZES:
        if hs >= SUBLANES:
            w = hs // SUBLANES
            lvl = []
            for i in range(n):
                ref = off[(i // (2 * w)) * 2 * w + w]
                lvl.append(b[i] - ref if (i // w) % 2 == 1 else ref - b[i])
        elif hs == 1:
            lvl = [jnp.where(r8 % 2 == 1, gi, 0.0) for gi in g]
        else:
            lvl = []
            for pi in p:
                ref = _row_bcast(pi, hs - 1)
                for blk in range(1, SUBLANES // (2 * hs)):
                    ref = jnp.where(r8 >= blk * 2 * hs, _row_bcast(pi, blk * 2 * hs + hs - 1), ref)
                lvl.append(-jnp.abs(pi - ref))
        exps.append(lvl)
    return exps, b, off[n]


def _scan_kernel(q_ref, k_ref, gl_ref, v_ref, gt_ref, gw_ref, c_ref, wa_ref, ba_ref,
                 o_ref, mod_ref, st_scr, attn_scr, qd_scr, kd_scr, dec_scr):
    mod_ref[...] = _ada_block(c_ref, wa_ref, ba_ref)
    gw = gw_ref[...]
    nt = CHUNK // SUBLANES
    n_groups = SEQ // (CHUNK * SCAN_GROUP)
    n_levels = len(LEVEL_SIZES)
    ti = lax.broadcasted_iota(jnp.int32, (CHUNK, CHUNK), 0)
    si = lax.broadcasted_iota(jnp.int32, (CHUNK, CHUNK), 1)
    owner = jnp.where(ti == si, n_levels, -1)
    for idx, hs in enumerate(LEVEL_SIZES):
        owner = jnp.where(((ti // (2 * hs)) == (si // (2 * hs))) & ((ti // hs) > (si // hs)),
                          idx, owner)
    st_scr[...] = jnp.zeros_like(st_scr)

    def tiles(a):
        return [a[i * SUBLANES:(i + 1) * SUBLANES, :] for i in range(nt)]

    owner_t = tiles(owner)

    def scaled(a_tiles, e_tiles, keep=None):
        zero = jnp.zeros_like(a_tiles[0])
        return jnp.concatenate(
            [a * jnp.exp2(e) if keep is None or keep[i] else zero
             for i, (a, e) in enumerate(zip(a_tiles, e_tiles))], axis=0).astype(BF16)

    def rows_of(c):
        return pl.ds(pl.multiple_of(c * CHUNK, CHUNK), CHUNK)

    def cols_of(hd):
        return slice(hd * HEAD_DIM, (hd + 1) * HEAD_DIM)

    def scores(hd, c):
        rows, cols = rows_of(c), cols_of(hd)
        qb = q_ref[0, rows, cols]
        kb = k_ref[0, rows, cols]
        q = tiles(qb.astype(F32))
        k = tiles(kb.astype(F32))
        exps, b, b_last = _level_exponents(tiles(gl_ref[0, rows, cols] * LOG2E))

        attn = [jnp.where(o == n_levels, d, 0.0) for o, d in zip(owner_t, tiles(_dot_nt(qb, kb)))]
        for idx, (hs, e) in enumerate(zip(LEVEL_SIZES, exps)):
            if hs >= SUBLANES:
                second = [(i // (hs // SUBLANES)) % 2 == 1 for i in range(nt)]
                first = [not s for s in second]
            else:
                second, first = [True] * nt, None
            lvl = tiles(_dot_nt(scaled(q, e, second), scaled(k, e, first)))
            for i in range(nt):
                if second[i]:
                    attn[i] = jnp.where(owner_t[i] == idx, lvl[i], attn[i])
        attn_scr[c] = jnp.concatenate(attn, axis=0).astype(BF16)
        qd_scr[rows, :] = scaled(q, b)
        kd_scr[rows, :] = scaled(k, [b_last - bi for bi in b])
        dec_scr[c] = jnp.exp2(b_last)

    def step(out_of, scores_of):
        prev, cur = [], []
        if out_of is not None:
            hp, gp = out_of
            prev = [gp * SCAN_GROUP + j for j in range(SCAN_GROUP)]
        if scores_of is not None:
            hc, gc = scores_of
            cur = [gc * SCAN_GROUP + j for j in range(SCAN_GROUP)]
        n = len(cur)
        cut = [0, n // 4, n // 2, n]
        if prev:
            vbs = [v_ref[0, rows_of(c), cols_of(hp)] for c in prev]
            intra = [jnp.dot(attn_scr[c], vb, preferred_element_type=F32)
                     for c, vb in zip(prev, vbs)]
            kv = [_dot_tn(vb, kd_scr[rows_of(c), :]) for c, vb in zip(prev, vbs)]
        for c in cur[cut[0]:cut[1]]:
            scores(hc, c)
        if prev:
            st = st_scr[...]
            inter = []
            for j, c in enumerate(prev):
                inter.append(_dot_nt(qd_scr[rows_of(c), :], st.astype(BF16)))
                st = dec_scr[c][0:1, :] * st + kv[j]
            last_of_head = isinstance(gp, int) and gp == n_groups - 1
            st_scr[...] = jnp.zeros_like(st) if last_of_head else st
        for c in cur[cut[1]:cut[2]]:
            scores(hc, c)
        if prev:
            ys = []
            for j, c in enumerate(prev):
                o = intra[j] + inter[j]
                ms = jnp.mean(o * o, axis=-1, keepdims=True)
                gate = gt_ref[0, rows_of(c), cols_of(hp)].astype(F32)
                ys.append((o * lax.rsqrt(ms + EPS) * gw * gate).astype(BF16))
        for c in cur[cut[2]:cut[3]]:
            scores(hc, c)
        if prev:
            for y, c in zip(ys, prev):
                o_ref[0, rows_of(c), cols_of(hp)] = y

    n_heads = q_ref.shape[2] // HEAD_DIM
    step(None, (0, 0))
    for hd in range(n_heads):
        def body(g, carry, hd=hd):
            step((hd, g - 1), (hd, g))
            return carry

        lax.fori_loop(1, n_groups, body, 0)
        step((hd, n_groups - 1), (hd + 1, 0) if hd + 1 < n_heads else None)


def _scan(q3, k3, gl3, v3, gt3, gnorm_w, c_pad, w_ada, b_ada, ada_start):
    blk = pl.BlockSpec((1, SEQ, SCAN_HEADS * HEAD_DIM), lambda b, h: (b, 0, h))
    hsteps = HGRN_HEADS // SCAN_HEADS
    n_ada = w_ada.shape[1] - ada_start
    assert n_ada == BATCH * hsteps * ADA_COLS and ada_start % ADA_COLS == 0

    def ada_map(b, h):
        return (0, ada_start // ADA_COLS + b * hsteps + h)

    return pl.pallas_call(
        _scan_kernel,
        grid=(BATCH, hsteps),
        in_specs=[blk, blk, blk, blk, blk, pl.BlockSpec((1, HEAD_DIM), lambda b, h: (0, 0)),
                  pl.BlockSpec((8, D_MODEL), lambda b, h: (0, 0)),
                  pl.BlockSpec((D_MODEL, ADA_COLS), ada_map),
                  pl.BlockSpec((1, ADA_COLS), ada_map)],
        out_specs=[blk, pl.BlockSpec((8, ADA_COLS), lambda b, h: (0, b * hsteps + h))],
        out_shape=[jax.ShapeDtypeStruct((BATCH, SEQ, HGRN_WIDTH), BF16),
                   jax.ShapeDtypeStruct((8, n_ada), F32)],
        scratch_shapes=[
            pltpu.VMEM((HEAD_DIM, HEAD_DIM), F32),
            pltpu.VMEM((SEQ // CHUNK, CHUNK, CHUNK), BF16),
            pltpu.VMEM((SEQ, HEAD_DIM), BF16),
            pltpu.VMEM((SEQ, HEAD_DIM), BF16),
            pltpu.VMEM((SEQ // CHUNK, SUBLANES, HEAD_DIM), F32),
        ],
        compiler_params=pltpu.CompilerParams(
            dimension_semantics=("arbitrary", "arbitrary"), vmem_limit_bytes=VMEM_LIMIT),
        name="hgrn_scan",
    )(q3, k3, gl3, v3, gt3, gnorm_w, c_pad, w_ada, b_ada)


def _wout_kernel(x_ref, yp_ref, yh_ref, wp_ref, wh_ref, gt_ref, o_ref):
    mix = (jnp.dot(yp_ref[...], wp_ref[...], preferred_element_type=F32)
           + jnp.dot(yh_ref[...], wh_ref[...], preferred_element_type=F32))
    o_ref[...] = x_ref[...] + gt_ref[0] * mix


def _wout(x2d, yp, yh, w_out, mod3, mod_base):
    m = x2d.shape[0]
    tm = 512
    tiles_per_batch = SEQ // tm
    mod_rows = mod3.shape[0] // BATCH
    return pl.pallas_call(
        _wout_kernel,
        grid=(m // tm,),
        in_specs=[
            pl.BlockSpec((tm, D_MODEL), lambda i: (i, 0)),
            pl.BlockSpec((tm, POOL_WIDTH), lambda i: (i, 0)),
            pl.BlockSpec((tm, HGRN_WIDTH), lambda i: (i, 0)),
            pl.BlockSpec((POOL_WIDTH, D_MODEL), lambda i: (0, 0)),
            pl.BlockSpec((HGRN_WIDTH, D_MODEL), lambda i: (1, 0)),
            pl.BlockSpec((1, 1, D_MODEL),
                         lambda i: ((i // tiles_per_batch) * mod_rows + mod_base, 0, 0)),
        ],
        out_specs=pl.BlockSpec((tm, D_MODEL), lambda i: (i, 0)),
        out_shape=jax.ShapeDtypeStruct((m, D_MODEL), F32),
        compiler_params=pltpu.CompilerParams(
            dimension_semantics=("arbitrary",), vmem_limit_bytes=VMEM_LIMIT),
        name="mix_out",
    )(x2d, yp, yh, w_out, w_out, mod3)


def kernel(x, c, w_ada, b_ada, norm1_w, ffn1_gate, ffn1_up, ffn1_down, norm2_w, w_in, pool_w,
           pool_scale, lb_logits, gnorm_w, w_out, norm3_w, ffn2_gate, ffn2_up, ffn2_down,
           final_norm_w):
    B, S, D = x.shape
    assert (B, S, D) == (BATCH, SEQ, D_MODEL) and w_ada.shape[0] == 1
    assert lb_logits.shape == (2, HGRN_WIDTH)

    c_pad = jnp.pad(c, ((0, 8 - B), (0, 0)))
    mod_a = _ada(c_pad, w_ada[0], b_ada, MOD_EARLY * D)[:B]
    mod3a = mod_a.reshape(B * MOD_EARLY, 1, D)

    x2d = x.reshape(B * S, D)
    fw = final_norm_w.reshape(1, D)

    x1, w_in_b, w_out_b, g2_b, u2_b, d2_b = _ffn(
        x2d, norm1_w, mod3a, 0, ffn1_gate[0].astype(BF16), ffn1_up[0].astype(BF16),
        ffn1_down[0].astype(BF16), fw, final_norm=False,
        cast=(w_in[0], w_out[0], ffn2_gate[0], ffn2_up[0], ffn2_down[0]))

    y_pool, q, k, gl, v, gt = _win(x1, norm2_w, mod3a, 3, w_in_b, lb_logits,
                                   pool_w[0].astype(BF16), pool_scale)
    y_hgrn, mod_b = _scan(*(a.reshape(B, S, HGRN_WIDTH) for a in (q, k, gl, v, gt)), gnorm_w,
                          c_pad, w_ada[0], b_ada, MOD_EARLY * D)
    mod3b = mod_b[:B].reshape(B * (N_MOD - MOD_EARLY), 1, D)
    x2 = _wout(x1, y_pool, y_hgrn.reshape(B * S, HGRN_WIDTH), w_out_b, mod3b, 0)

    out, = _ffn(x2, norm3_w, mod3b, 1, g2_b, u2_b, d2_b, fw, final_norm=True)
    return out.reshape(B, S, D)
```

```python
import functools

import jax
import jax.numpy as jnp
from jax import lax
from jax.experimental import pallas as pl
from jax.experimental.pallas import tpu as pltpu

F32 = jnp.float32
BF16 = jnp.bfloat16

D_MODEL = 2048
BATCH = 4
SEQ = 2048
POOL_WIDTH = 1024
POOL_WINDOWS = (2, 4, 8, 16)
POOL_GROUP_DIM = 256
HGRN_WIDTH = 1024
HGRN_HEADS = 8
HEAD_DIM = 128
IN_WIDTH = 5120
CHUNK = 64
SUBLANES = 8
BF16_SUBLANES = 16
WIN_SUB_ROWS = 256
FFN_SUB_ROWS = 512
FFN_MID_SUB_ROWS = 512
POOL_HALO = 16
D_FF = 5632
N_MOD = 9
MOD_EARLY = 5
ADA_COLS = 1024
EPS = 1e-6

VMEM_LIMIT = 60 * 1024 * 1024


def _silu(v):
    return v * jax.nn.sigmoid(v)


def _norm_modulate(x, nw, sh, sc):
    ms = jnp.mean(x * x, axis=-1, keepdims=True)
    y = x * lax.rsqrt(ms + EPS) * nw
    return y * (1.0 + sc) + sh


def _ada_block(c_ref, w_ref, b_ref):
    ca = _silu(c_ref[...]).astype(BF16)
    return jnp.dot(ca, w_ref[...].astype(BF16), preferred_element_type=F32) + b_ref[...]


def _ada_kernel(c_ref, w_ref, b_ref, o_ref):
    o_ref[...] = _ada_block(c_ref, w_ref, b_ref)


def _ada(c_pad, w_ada, b_ada, n):
    tn = ADA_COLS
    return pl.pallas_call(
        _ada_kernel,
        grid=(n // tn,),
        in_specs=[
            pl.BlockSpec((8, D_MODEL), lambda j: (0, 0)),
            pl.BlockSpec((D_MODEL, tn), lambda j: (0, j)),
            pl.BlockSpec((1, tn), lambda j: (0, j)),
        ],
        out_specs=pl.BlockSpec((8, tn), lambda j: (0, j)),
        out_shape=jax.ShapeDtypeStruct((8, n), F32),
        compiler_params=pltpu.CompilerParams(
            dimension_semantics=("arbitrary",), vmem_limit_bytes=VMEM_LIMIT),
        name="ada_mod",
    )(c_pad, w_ada, b_ada)


def _ffn_kernel(*refs, nj, final_norm, n_cast):
    x_ref, nw_ref, sh_ref, sc_ref, gt_ref, wg_ref, wu_ref, wd_ref, fw_ref = refs[:9]
    cast_in = refs[9:9 + n_cast]
    o_ref = refs[9 + n_cast]
    cast_out = refs[10 + n_cast:10 + 2 * n_cast]
    h_scr, = refs[10 + 2 * n_cast:]
    j = pl.program_id(1)
    tm = x_ref.shape[0]
    subs = [pl.ds(r, FFN_SUB_ROWS) for r in range(0, tm, FFN_SUB_ROWS)]
    mid_rows = FFN_MID_SUB_ROWS if n_cast else tm
    mid_subs = [pl.ds(r, mid_rows) for r in range(0, tm, mid_rows)]

    for w_ref, wb_ref in zip(cast_in, cast_out):
        wb_ref[...] = w_ref[...].astype(BF16)

    def partial_ffn(rows):
        h = h_scr[rows, :]
        g = jnp.dot(h, wg_ref[...], preferred_element_type=F32)
        u = jnp.dot(h, wu_ref[...], preferred_element_type=F32)
        a = (_silu(g) * u).astype(BF16)
        return jnp.dot(a, wd_ref[...], preferred_element_type=F32)

    @pl.when(j == 0)
    def _():
        for rows in subs:
            h = _norm_modulate(x_ref[rows, :], nw_ref[...], sh_ref[0], sc_ref[0])
            h_scr[rows, :] = h.astype(BF16)
            o_ref[rows, :] = partial_ffn(rows)

    @pl.when((j > 0) & (j < nj - 1))
    def _():
        for rows in mid_subs:
            o_ref[rows, :] += partial_ffn(rows)

    @pl.when(j == nj - 1)
    def _():
        for rows in subs:
            out = x_ref[rows, :] + 0.5 * gt_ref[0] * (o_ref[rows, :] + partial_ffn(rows))
            if final_norm:
                ms = jnp.mean(out * out, axis=-1, keepdims=True)
                out = out * lax.rsqrt(ms + EPS) * fw_ref[...]
            o_ref[rows, :] = out


def _ffn(x2d, nw, mod3, mod_base, wg, wu, wd, fw, *, final_norm, cast=()):
    m = x2d.shape[0]
    tm, tf = 1024, 512
    ni, nj = m // tm, D_FF // tf
    tiles_per_batch = SEQ // tm
    mod_rows = mod3.shape[0] // BATCH

    def mod_map(k):
        return lambda i, j: ((i // tiles_per_batch) * mod_rows + mod_base + k, 0, 0)

    cast_specs = []
    for w in cast:
        rows = w.shape[0]
        nblk = max(n for n in range(1, ni * nj + 1) if rows % (BF16_SUBLANES * n) == 0)
        cast_specs.append(pl.BlockSpec(
            (rows // nblk, w.shape[1]),
            lambda i, j, nblk=nblk: (jnp.minimum(i * nj + j, nblk - 1), 0)))

    kern = functools.partial(_ffn_kernel, nj=nj, final_norm=final_norm, n_cast=len(cast))
    res = pl.pallas_call(
        kern,
        grid=(ni, nj),
        in_specs=[
            pl.BlockSpec((tm, D_MODEL), lambda i, j: (i, 0)),
            pl.BlockSpec((1, D_MODEL), lambda i, j: (0, 0)),
            pl.BlockSpec((1, 1, D_MODEL), mod_map(0)),
            pl.BlockSpec((1, 1, D_MODEL), mod_map(1)),
            pl.BlockSpec((1, 1, D_MODEL), mod_map(2)),
            pl.BlockSpec((D_MODEL, tf), lambda i, j: (0, j)),
            pl.BlockSpec((D_MODEL, tf), lambda i, j: (0, j)),
            pl.BlockSpec((tf, D_MODEL), lambda i, j: (j, 0)),
            pl.BlockSpec((1, D_MODEL), lambda i, j: (0, 0)),
        ] + cast_specs,
        out_specs=[pl.BlockSpec((tm, D_MODEL), lambda i, j: (i, 0))] + cast_specs,
        out_shape=[jax.ShapeDtypeStruct((m, D_MODEL), F32)]
        + [jax.ShapeDtypeStruct(w.shape, BF16) for w in cast],
        scratch_shapes=[pltpu.VMEM((tm, D_MODEL), BF16)],
        compiler_params=pltpu.CompilerParams(
            dimension_semantics=("arbitrary", "arbitrary"), vmem_limit_bytes=VMEM_LIMIT),
        name="ffn_final" if final_norm else "ffn",
    )(x2d, nw, mod3, mod3, mod3, wg, wu, wd, fw, *cast)
    return res


def _pool_group(ext, u, t_abs, window, pw, ps):
    s, k = ext, 1
    while k < window:
        s = s + pltpu.roll(s, k, axis=0)
        k *= 2
    cnt = jnp.minimum(t_abs + 1, window).astype(F32)
    pooled = s[POOL_HALO:, :] * (1.0 / cnt) - u
    return jnp.dot(pooled.astype(BF16), pw, preferred_element_type=F32) * ps


def _win_kernel(x_ref, nw_ref, sh_ref, sc_ref, w_ref, lbl_ref, pw_ref, ps_ref,
                yp_ref, q_ref, k_ref, gl_ref, v_ref, gt_ref, tail_scr, *, tiles_per_batch):
    tm, tn = yp_ref.shape
    cg = POOL_GROUP_DIM
    lbl = lbl_ref[...]
    pe = jnp.exp(lbl - jnp.max(lbl, axis=0, keepdims=True))
    p = pe / jnp.sum(pe, axis=0, keepdims=True)
    lb = (p[0:1, :] + p[1:2, :]) - p[0:1, :]

    tile_in_seq = pl.program_id(0) % tiles_per_batch

    @pl.when(tile_in_seq == 0)
    def _():
        tail_scr[...] = jnp.zeros_like(tail_scr)

    tail = tail_scr[...]
    row = lax.broadcasted_iota(jnp.int32, (WIN_SUB_ROWS, 1), 0)

    for r in range(0, tm, WIN_SUB_ROWS):
        rows = pl.ds(r, WIN_SUB_ROWS)
        h = _norm_modulate(x_ref[rows, :], nw_ref[...], sh_ref[0], sc_ref[0]).astype(BF16)

        def project(seg):
            return jnp.dot(h, w_ref[:, seg * tn:(seg + 1) * tn], preferred_element_type=F32)

        u = project(0)
        ext = jnp.concatenate([tail, u], axis=0)
        tail = u[WIN_SUB_ROWS - POOL_HALO:, :]
        t_abs = tile_in_seq * tm + r + row
        for g, window in enumerate(POOL_WINDOWS):
            cols = slice(g * cg, (g + 1) * cg)
            y = _pool_group(ext[:, cols], u[:, cols], t_abs, window, pw_ref[g], ps_ref[:, cols])
            yp_ref[rows, cols] = y.astype(BF16)
        q_ref[rows, :] = _silu(project(1)).astype(BF16)
        forget = lb + (1.0 - lb) * jax.nn.sigmoid(project(2))
        k_ref[rows, :] = (1.0 - forget).astype(BF16)
        gl_ref[rows, :] = jnp.log(forget)
        v_ref[rows, :] = project(3).astype(BF16)
        gt_ref[rows, :] = _silu(project(4)).astype(BF16)
    tail_scr[...] = tail


def _win(x2d, nw, mod3, mod_base, w_in, lb_logits, pool_w, pool_scale):
    m = x2d.shape[0]
    tm, tn = 512, 1024
    assert IN_WIDTH == 5 * tn and POOL_WIDTH == tn and HGRN_WIDTH == tn
    assert max(POOL_WINDOWS) <= POOL_HALO and tn == len(POOL_WINDOWS) * POOL_GROUP_DIM
    tiles_per_batch = SEQ // tm
    mod_rows = mod3.shape[0] // BATCH

    def mod_map(k):
        return lambda i: ((i // tiles_per_batch) * mod_rows + mod_base + k, 0, 0)

    seg = pl.BlockSpec((tm, tn), lambda i: (i, 0))
    return pl.pallas_call(
        functools.partial(_win_kernel, tiles_per_batch=tiles_per_batch),
        grid=(m // tm,),
        in_specs=[
            pl.BlockSpec((tm, D_MODEL), lambda i: (i, 0)),
            pl.BlockSpec((1, D_MODEL), lambda i: (0, 0)),
            pl.BlockSpec((1, 1, D_MODEL), mod_map(0)),
            pl.BlockSpec((1, 1, D_MODEL), mod_map(1)),
            pl.BlockSpec((D_MODEL, IN_WIDTH), lambda i: (0, 0), pipeline_mode=pl.Buffered(1)),
            pl.BlockSpec((2, tn), lambda i: (0, 0)),
            pl.BlockSpec(pool_w.shape, lambda i: (0, 0, 0)),
            pl.BlockSpec((1, tn), lambda i: (0, 0)),
        ],
        out_specs=[seg] * 6,
        out_shape=[
            jax.ShapeDtypeStruct((m, tn), BF16),
            jax.ShapeDtypeStruct((m, tn), BF16),
            jax.ShapeDtypeStruct((m, tn), BF16),
            jax.ShapeDtypeStruct((m, tn), F32),
            jax.ShapeDtypeStruct((m, tn), BF16),
            jax.ShapeDtypeStruct((m, tn), BF16),
        ],
        scratch_shapes=[pltpu.VMEM((POOL_HALO, tn), F32)],
        compiler_params=pltpu.CompilerParams(
            dimension_semantics=("arbitrary",), vmem_limit_bytes=VMEM_LIMIT),
        name="mix_in",
    )(x2d, nw, mod3, mod3, w_in, lb_logits, pool_w, pool_scale)


def _dot_nt(a, b):
    return lax.dot_general(a, b, (((1,), (1,)), ((), ())), preferred_element_type=F32)


def _dot_tn(a, b):
    return lax.dot_general(a, b, (((0,), (0,)), ((), ())), preferred_element_type=F32)


LEVEL_SIZES = (32, 16, 8, 4, 2, 1)
SCAN_GROUP = 8
SCAN_HEADS = 4
LOG2E = 1.4426950408889634


def _row_bcast(v, r):
    return jnp.broadcast_to(v[r:r + 1, :], v.shape)


def _level_exponents(g):
    n = len(g)
    r8 = lax.broadcasted_iota(jnp.int32, (SUBLANES, 1), 0)
    p = list(g)
    for s in (1, 2, 4):
        p = [pi + jnp.where(r8 >= s, pltpu.roll(pi, s, axis=0), 0.0) for pi in p]
    off = [None, _row_bcast(p[0], SUBLANES - 1)]
    for i in range(1, n):
        off.append(off[i] + _row_bcast(p[i], SUBLANES - 1))
    b = [p[0]] + [p[i] + off[i] for i in range(1, n)]

    exps = []
    for hs in LEVEL_SIZES:
        if hs >= SUBLANES:
            w = hs // SUBLANES
            lvl = []
            for i in range(n):
                ref = off[(i // (2 * w)) * 2 * w + w]
                lvl.append(b[i] - ref if (i // w) % 2 == 1 else ref - b[i])
        elif hs == 1:
            lvl = [jnp.where(r8 % 2 == 1, gi, 0.0) for gi in g]
        else:
            lvl = []
            for pi in p:
                ref = _row_bcast(pi, hs - 1)
                for blk in range(1, SUBLANES // (2 * hs)):
                    ref = jnp.where(r8 >= blk * 2 * hs, _row_bcast(pi, blk * 2 * hs + hs - 1), ref)
                lvl.append(-jnp.abs(pi - ref))
        exps.append(lvl)
    return exps, b, off[n]


def _scan_kernel(q_ref, k_ref, gl_ref, v_ref, gt_ref, gw_ref, c_ref, wa_ref, ba_ref,
                 o_ref, mod_ref, st_scr, attn_scr, qd_scr, kd_scr, dec_scr):
    gw = gw_ref[...]
    nt = CHUNK // SUBLANES
    n_groups = SEQ // (CHUNK * SCAN_GROUP)
    n_levels = len(LEVEL_SIZES)
    ti = lax.broadcasted_iota(jnp.int32, (CHUNK, CHUNK), 0)
    si = lax.broadcasted_iota(jnp.int32, (CHUNK, CHUNK), 1)
    owner = jnp.where(ti == si, n_levels, -1)
    for idx, hs in enumerate(LEVEL_SIZES):
        owner = jnp.where(((ti // (2 * hs)) == (si // (2 * hs))) & ((ti // hs) > (si // hs)),
                          idx, owner)
    st_scr[...] = jnp.zeros_like(st_scr)

    def tiles(a):
        return [a[i * SUBLANES:(i + 1) * SUBLANES, :] for i in range(nt)]

    owner_t = tiles(owner)

    def scaled(a_tiles, e_tiles, keep=None):
        zero = jnp.zeros_like(a_tiles[0])
        return jnp.concatenate(
            [a * jnp.exp2(e) if keep is None or keep[i] else zero
             for i, (a, e) in enumerate(zip(a_tiles, e_tiles))], axis=0).astype(BF16)

    def rows_of(c):
        return pl.ds(pl.multiple_of(c * CHUNK, CHUNK), CHUNK)

    def cols_of(hd):
        return slice(hd * HEAD_DIM, (hd + 1) * HEAD_DIM)

    def scores(hd, c):
        rows, cols = rows_of(c), cols_of(hd)
        qb = q_ref[0, rows, cols]
        kb = k_ref[0, rows, cols]
        q = tiles(qb.astype(F32))
        k = tiles(kb.astype(F32))
        exps, b, b_last = _level_exponents(tiles(gl_ref[0, rows, cols] * LOG2E))

        attn = [jnp.where(o == n_levels, d, 0.0) for o, d in zip(owner_t, tiles(_dot_nt(qb, kb)))]
        for idx, (hs, e) in enumerate(zip(LEVEL_SIZES, exps)):
            if hs >= SUBLANES:
                second = [(i // (hs // SUBLANES)) % 2 == 1 for i in range(nt)]
                first = [not s for s in second]
            else:
                second, first = [True] * nt, None
            lvl = tiles(_dot_nt(scaled(q, e, second), scaled(k, e, first)))
            for i in range(nt):
                if second[i]:
                    attn[i] = jnp.where(owner_t[i] == idx, lvl[i], attn[i])
        attn_scr[c] = jnp.concatenate(attn, axis=0).astype(BF16)
        qd_scr[rows, :] = scaled(q, b)
        kd_scr[rows, :] = scaled(k, [b_last - bi for bi in b])
        dec_scr[c] = jnp.exp2(b_last)

    def step(out_of, scores_of, extra=None):
        prev, cur = [], []
        if out_of is not None:
            hp, gp = out_of
            prev = [gp * SCAN_GROUP + j for j in range(SCAN_GROUP)]
        if scores_of is not None:
            hc, gc = scores_of
            cur = [gc * SCAN_GROUP + j for j in range(SCAN_GROUP)]
        n = len(cur)
        cut = [0, n // 4, n // 2, n]
        if prev:
            vbs = [v_ref[0, rows_of(c), cols_of(hp)] for c in prev]
            intra = [jnp.dot(attn_scr[c], vb, preferred_element_type=F32)
                     for c, vb in zip(prev, vbs)]
            kv = [_dot_tn(vb, kd_scr[rows_of(c), :]) for c, vb in zip(prev, vbs)]
        for c in cur[cut[0]:cut[1]]:
            scores(hc, c)
        if extra is not None:
            extra()
        if prev:
            st = st_scr[...]
            inter = []
            for j, c in enumerate(prev):
                inter.append(_dot_nt(qd_scr[rows_of(c), :], st.astype(BF16)))
                st = dec_scr[c][0:1, :] * st + kv[j]
            last_of_head = isinstance(gp, int) and gp == n_groups - 1
            st_scr[...] = jnp.zeros_like(st) if last_of_head else st
        for c in cur[cut[1]:cut[2]]:
            scores(hc, c)
        if prev:
            ys = []
            for j, c in enumerate(prev):
                o = intra[j] + inter[j]
                ms = jnp.mean(o * o, axis=-1, keepdims=True)
                gate = gt_ref[0, rows_of(c), cols_of(hp)].astype(F32)
                ys.append((o * lax.rsqrt(ms + EPS) * gw * gate).astype(BF16))
        for c in cur[cut[2]:cut[3]]:
            scores(hc, c)
        if prev:
            for y, c in zip(ys, prev):
                o_ref[0, rows_of(c), cols_of(hp)] = y

    def ada_block():
        mod_ref[...] = _ada_block(c_ref, wa_ref, ba_ref)

    n_heads = q_ref.shape[2] // HEAD_DIM
    step(None, (0, 0))
    for hd in range(n_heads):
        def body(g, carry, hd=hd):
            step((hd, g - 1), (hd, g))
            return carry

        lax.fori_loop(1, n_groups, body, 0)
        if hd + 1 < n_heads:
            step((hd, n_groups - 1), (hd + 1, 0))
        else:
            step((hd, n_groups - 1), None, extra=ada_block)


def _scan(q3, k3, gl3, v3, gt3, gnorm_w, c_pad, w_ada, b_ada, ada_start):
    blk = pl.BlockSpec((1, SEQ, SCAN_HEADS * HEAD_DIM), lambda b, h: (b, 0, h))
    hsteps = HGRN_HEADS // SCAN_HEADS
    n_ada = w_ada.shape[1] - ada_start
    assert n_ada == BATCH * hsteps * ADA_COLS and ada_start % ADA_COLS == 0

    def ada_map(b, h):
        return (0, ada_start // ADA_COLS + b * hsteps + h)

    return pl.pallas_call(
        _scan_kernel,
        grid=(BATCH, hsteps),
        in_specs=[blk, blk, blk, blk, blk, pl.BlockSpec((1, HEAD_DIM), lambda b, h: (0, 0)),
                  pl.BlockSpec((8, D_MODEL), lambda b, h: (0, 0)),
                  pl.BlockSpec((D_MODEL, ADA_COLS), ada_map),
                  pl.BlockSpec((1, ADA_COLS), ada_map)],
        out_specs=[blk, pl.BlockSpec((8, ADA_COLS), lambda b, h: (0, b * hsteps + h))],
        out_shape=[jax.ShapeDtypeStruct((BATCH, SEQ, HGRN_WIDTH), BF16),
                   jax.ShapeDtypeStruct((8, n_ada), F32)],
        scratch_shapes=[
            pltpu.VMEM((HEAD_DIM, HEAD_DIM), F32),
            pltpu.VMEM((SEQ // CHUNK, CHUNK, CHUNK), BF16),
            pltpu.VMEM((SEQ, HEAD_DIM), BF16),
            pltpu.VMEM((SEQ, HEAD_DIM), BF16),
            pltpu.VMEM((SEQ // CHUNK, SUBLANES, HEAD_DIM), F32),
        ],
        compiler_params=pltpu.CompilerParams(
            dimension_semantics=("arbitrary", "arbitrary"), vmem_limit_bytes=VMEM_LIMIT),
        name="hgrn_scan",
    )(q3, k3, gl3, v3, gt3, gnorm_w, c_pad, w_ada, b_ada)


def _wout_kernel(x_ref, yp_ref, yh_ref, wp_ref, wh_ref, gt_ref, o_ref):
    mix = (jnp.dot(yp_ref[...], wp_ref[...], preferred_element_type=F32)
           + jnp.dot(yh_ref[...], wh_ref[...], preferred_element_type=F32))
    o_ref[...] = x_ref[...] + gt_ref[0] * mix


def _wout(x2d, yp, yh, w_out, mod3, mod_base):
    m = x2d.shape[0]
    tm = 512
    tiles_per_batch = SEQ // tm
    mod_rows = mod3.shape[0] // BATCH
    return pl.pallas_call(
        _wout_kernel,
        grid=(m // tm,),
        in_specs=[
            pl.BlockSpec((tm, D_MODEL), lambda i: (i, 0)),
            pl.BlockSpec((tm, POOL_WIDTH), lambda i: (i, 0)),
            pl.BlockSpec((tm, HGRN_WIDTH), lambda i: (i, 0)),
            pl.BlockSpec((POOL_WIDTH, D_MODEL), lambda i: (0, 0)),
            pl.BlockSpec((HGRN_WIDTH, D_MODEL), lambda i: (1, 0)),
            pl.BlockSpec((1, 1, D_MODEL),
                         lambda i: ((i // tiles_per_batch) * mod_rows + mod_base, 0, 0)),
        ],
        out_specs=pl.BlockSpec((tm, D_MODEL), lambda i: (i, 0)),
        out_shape=jax.ShapeDtypeStruct((m, D_MODEL), F32),
        compiler_params=pltpu.CompilerParams(
            dimension_semantics=("arbitrary",), vmem_limit_bytes=VMEM_LIMIT),
        name="mix_out",
    )(x2d, yp, yh, w_out, w_out, mod3)


def kernel(x, c, w_ada, b_ada, norm1_w, ffn1_gate, ffn1_up, ffn1_down, norm2_w, w_in, pool_w,
           pool_scale, lb_logits, gnorm_w, w_out, norm3_w, ffn2_gate, ffn2_up, ffn2_down,
           final_norm_w):
    B, S, D = x.shape
    assert (B, S, D) == (BATCH, SEQ, D_MODEL) and w_ada.shape[0] == 1
    assert lb_logits.shape == (2, HGRN_WIDTH)

    c_pad = jnp.pad(c, ((0, 8 - B), (0, 0)))
    mod_a = _ada(c_pad, w_ada[0], b_ada, MOD_EARLY * D)[:B]
    mod3a = mod_a.reshape(B * MOD_EARLY, 1, D)

    x2d = x.reshape(B * S, D)
    fw = final_norm_w.reshape(1, D)

    x1, w_in_b, w_out_b, g2_b, u2_b, d2_b = _ffn(
        x2d, norm1_w, mod3a, 0, ffn1_gate[0].astype(BF16), ffn1_up[0].astype(BF16),
        ffn1_down[0].astype(BF16), fw, final_norm=False,
        cast=(w_in[0], w_out[0], ffn2_gate[0], ffn2_up[0], ffn2_down[0]))

    y_pool, q, k, gl, v, gt = _win(x1, norm2_w, mod3a, 3, w_in_b, lb_logits,
                                   pool_w[0].astype(BF16), pool_scale)
    y_hgrn, mod_b = _scan(*(a.reshape(B, S, HGRN_WIDTH) for a in (q, k, gl, v, gt)), gnorm_w,
                          c_pad, w_ada[0], b_ada, MOD_EARLY * D)
    mod3b = mod_b[:B].reshape(B * (N_MOD - MOD_EARLY), 1, D)
    x2 = _wout(x1, y_pool, y_hgrn.reshape(B * S, HGRN_WIDTH), w_out_b, mod3b, 0)

    out, = _ffn(x2, norm3_w, mod3b, 1, g2_b, u2_b, d2_b, fw, final_norm=True)
    return out.reshape(B, S, D)
```

```python
import functools

import jax
import jax.numpy as jnp
from jax import lax
from jax.experimental import pallas as pl
from jax.experimental.pallas import tpu as pltpu

F32 = jnp.float32
BF16 = jnp.bfloat16

D_MODEL = 2048
BATCH = 4
SEQ = 2048
POOL_WIDTH = 1024
POOL_WINDOWS = (2, 4, 8, 16)
POOL_GROUP_DIM = 256
HGRN_WIDTH = 1024
HGRN_HEADS = 8
HEAD_DIM = 128
IN_WIDTH = 5120
CHUNK = 64
SUBLANES = 8
BF16_SUBLANES = 16
WIN_SUB_ROWS = 256
FFN_SUB_ROWS = 512
FFN_MID_SUB_ROWS = 512
POOL_HALO = 16
D_FF = 5632
N_MOD = 9
MOD_EARLY = 5
ADA_COLS = 1024
EPS = 1e-6

VMEM_LIMIT = 60 * 1024 * 1024


def _sigmoid(v):
    return 0.5 * jnp.tanh(0.5 * v) + 0.5


def _silu(v):
    return v * _sigmoid(v)


def _norm_modulate(x, nw, sh, sc):
    ms = jnp.mean(x * x, axis=-1, keepdims=True)
    y = x * lax.rsqrt(ms + EPS) * nw
    return y * (1.0 + sc) + sh


def _ada_block(c_ref, w_ref, b_ref):
    ca = _silu(c_ref[...]).astype(BF16)
    return jnp.dot(ca, w_ref[...].astype(BF16), preferred_element_type=F32) + b_ref[...]


def _ada_kernel(c_ref, w_ref, b_ref, o_ref):
    o_ref[...] = _ada_block(c_ref, w_ref, b_ref)


def _ada(c_pad, w_ada, b_ada, n):
    tn = ADA_COLS
    return pl.pallas_call(
        _ada_kernel,
        grid=(n // tn,),
        in_specs=[
            pl.BlockSpec((8, D_MODEL), lambda j: (0, 0)),
            pl.BlockSpec((D_MODEL, tn), lambda j: (0, j)),
            pl.BlockSpec((1, tn), lambda j: (0, j)),
        ],
        out_specs=pl.BlockSpec((8, tn), lambda j: (0, j)),
        out_shape=jax.ShapeDtypeStruct((8, n), F32),
        compiler_params=pltpu.CompilerParams(
            dimension_semantics=("arbitrary",), vmem_limit_bytes=VMEM_LIMIT),
        name="ada_mod",
    )(c_pad, w_ada, b_ada)


def _ffn_kernel(*refs, nj, final_norm, n_cast):
    x_ref, nw_ref, sh_ref, sc_ref, gt_ref, wg_ref, wu_ref, wd_ref, fw_ref = refs[:9]
    cast_in = refs[9:9 + n_cast]
    o_ref = refs[9 + n_cast]
    cast_out = refs[10 + n_cast:10 + 2 * n_cast]
    h_scr, = refs[10 + 2 * n_cast:]
    j = pl.program_id(1)
    tm = x_ref.shape[0]
    subs = [pl.ds(r, FFN_SUB_ROWS) for r in range(0, tm, FFN_SUB_ROWS)]
    mid_rows = FFN_MID_SUB_ROWS if n_cast else tm
    mid_subs = [pl.ds(r, mid_rows) for r in range(0, tm, mid_rows)]

    for w_ref, wb_ref in zip(cast_in, cast_out):
        wb_ref[...] = w_ref[...].astype(BF16)

    def partial_ffn(rows):
        h = h_scr[rows, :]
        g = jnp.dot(h, wg_ref[...], preferred_element_type=F32)
        u = jnp.dot(h, wu_ref[...], preferred_element_type=F32)
        a = (_silu(g) * u).astype(BF16)
        return jnp.dot(a, wd_ref[...], preferred_element_type=F32)

    @pl.when(j == 0)
    def _():
        for rows in subs:
            h = _norm_modulate(x_ref[rows, :], nw_ref[...], sh_ref[0], sc_ref[0])
            h_scr[rows, :] = h.astype(BF16)
            o_ref[rows, :] = partial_ffn(rows)

    @pl.when((j > 0) & (j < nj - 1))
    def _():
        for rows in mid_subs:
            o_ref[rows, :] += partial_ffn(rows)

    @pl.when(j == nj - 1)
    def _():
        for rows in subs:
            out = x_ref[rows, :] + 0.5 * gt_ref[0] * (o_ref[rows, :] + partial_ffn(rows))
            if final_norm:
                ms = jnp.mean(out * out, axis=-1, keepdims=True)
                out = out * lax.rsqrt(ms + EPS) * fw_ref[...]
            o_ref[rows, :] = out


def _ffn(x2d, nw, mod3, mod_base, wg, wu, wd, fw, *, final_norm, cast=()):
    m = x2d.shape[0]
    tm, tf = 1024, 512
    ni, nj = m // tm, D_FF // tf
    tiles_per_batch = SEQ // tm
    mod_rows = mod3.shape[0] // BATCH

    def mod_map(k):
        return lambda i, j: ((i // tiles_per_batch) * mod_rows + mod_base + k, 0, 0)

    cast_specs = []
    for w in cast:
        rows = w.shape[0]
        nblk = max(n for n in range(1, ni * nj + 1) if rows % (BF16_SUBLANES * n) == 0)
        cast_specs.append(pl.BlockSpec(
            (rows // nblk, w.shape[1]),
            lambda i, j, nblk=nblk: (jnp.minimum(i * nj + j, nblk - 1), 0)))

    kern = functools.partial(_ffn_kernel, nj=nj, final_norm=final_norm, n_cast=len(cast))
    res = pl.pallas_call(
        kern,
        grid=(ni, nj),
        in_specs=[
            pl.BlockSpec((tm, D_MODEL), lambda i, j: (i, 0)),
            pl.BlockSpec((1, D_MODEL), lambda i, j: (0, 0)),
            pl.BlockSpec((1, 1, D_MODEL), mod_map(0)),
            pl.BlockSpec((1, 1, D_MODEL), mod_map(1)),
            pl.BlockSpec((1, 1, D_MODEL), mod_map(2)),
            pl.BlockSpec((D_MODEL, tf), lambda i, j: (0, j)),
            pl.BlockSpec((D_MODEL, tf), lambda i, j: (0, j)),
            pl.BlockSpec((tf, D_MODEL), lambda i, j: (j, 0)),
            pl.BlockSpec((1, D_MODEL), lambda i, j: (0, 0)),
        ] + cast_specs,
        out_specs=[pl.BlockSpec((tm, D_MODEL), lambda i, j: (i, 0))] + cast_specs,
        out_shape=[jax.ShapeDtypeStruct((m, D_MODEL), F32)]
        + [jax.ShapeDtypeStruct(w.shape, BF16) for w in cast],
        scratch_shapes=[pltpu.VMEM((tm, D_MODEL), BF16)],
        compiler_params=pltpu.CompilerParams(
            dimension_semantics=("arbitrary", "arbitrary"), vmem_limit_bytes=VMEM_LIMIT),
        name="ffn_final" if final_norm else "ffn",
    )(x2d, nw, mod3, mod3, mod3, wg, wu, wd, fw, *cast)
    return res


def _pool_group(ext, u, t_abs, window, pw, ps):
    s, k = ext, 1
    while k < window:
        s = s + pltpu.roll(s, k, axis=0)
        k *= 2
    cnt = jnp.minimum(t_abs + 1, window).astype(F32)
    pooled = s[POOL_HALO:, :] * (1.0 / cnt) - u
    return jnp.dot(pooled.astype(BF16), pw, preferred_element_type=F32) * ps


def _win_kernel(x_ref, nw_ref, sh_ref, sc_ref, w_ref, lbl_ref, pw_ref, ps_ref,
                yp_ref, q_ref, k_ref, gl_ref, v_ref, gt_ref, tail_scr, *, tiles_per_batch):
    tm, tn = yp_ref.shape
    cg = POOL_GROUP_DIM
    lbl = lbl_ref[...]
    pe = jnp.exp(lbl - jnp.max(lbl, axis=0, keepdims=True))
    p = pe / jnp.sum(pe, axis=0, keepdims=True)
    lb = (p[0:1, :] + p[1:2, :]) - p[0:1, :]

    tile_in_seq = pl.program_id(0) % tiles_per_batch

    @pl.when(tile_in_seq == 0)
    def _():
        tail_scr[...] = jnp.zeros_like(tail_scr)

    tail = tail_scr[...]
    row = lax.broadcasted_iota(jnp.int32, (WIN_SUB_ROWS, 1), 0)

    for r in range(0, tm, WIN_SUB_ROWS):
        rows = pl.ds(r, WIN_SUB_ROWS)
        h = _norm_modulate(x_ref[rows, :], nw_ref[...], sh_ref[0], sc_ref[0]).astype(BF16)

        def project(seg):
            return jnp.dot(h, w_ref[:, seg * tn:(seg + 1) * tn], preferred_element_type=F32)

        u = project(0)
        ext = jnp.concatenate([tail, u], axis=0)
        tail = u[WIN_SUB_ROWS - POOL_HALO:, :]
        t_abs = tile_in_seq * tm + r + row
        for g, window in enumerate(POOL_WINDOWS):
            cols = slice(g * cg, (g + 1) * cg)
            y = _pool_group(ext[:, cols], u[:, cols], t_abs, window, pw_ref[g], ps_ref[:, cols])
            yp_ref[rows, cols] = y.astype(BF16)
        q_ref[rows, :] = _silu(project(1)).astype(BF16)
        forget = lb + (1.0 - lb) * _sigmoid(project(2))
        k_ref[rows, :] = (1.0 - forget).astype(BF16)
        gl_ref[rows, :] = jnp.log(forget)
        v_ref[rows, :] = project(3).astype(BF16)
        gt_ref[rows, :] = _silu(project(4)).astype(BF16)
    tail_scr[...] = tail


def _win(x2d, nw, mod3, mod_base, w_in, lb_logits, pool_w, pool_scale):
    m = x2d.shape[0]
    tm, tn = 512, 1024
    assert IN_WIDTH == 5 * tn and POOL_WIDTH == tn and HGRN_WIDTH == tn
    assert max(POOL_WINDOWS) <= POOL_HALO and tn == len(POOL_WINDOWS) * POOL_GROUP_DIM
    tiles_per_batch = SEQ // tm
    mod_rows = mod3.shape[0] // BATCH

    def mod_map(k):
        return lambda i: ((i // tiles_per_batch) * mod_rows + mod_base + k, 0, 0)

    seg = pl.BlockSpec((tm, tn), lambda i: (i, 0))
    return pl.pallas_call(
        functools.partial(_win_kernel, tiles_per_batch=tiles_per_batch),
        grid=(m // tm,),
        in_specs=[
            pl.BlockSpec((tm, D_MODEL), lambda i: (i, 0)),
            pl.BlockSpec((1, D_MODEL), lambda i: (0, 0)),
            pl.BlockSpec((1, 1, D_MODEL), mod_map(0)),
            pl.BlockSpec((1, 1, D_MODEL), mod_map(1)),
            pl.BlockSpec((D_MODEL, IN_WIDTH), lambda i: (0, 0), pipeline_mode=pl.Buffered(1)),
            pl.BlockSpec((2, tn), lambda i: (0, 0)),
            pl.BlockSpec(pool_w.shape, lambda i: (0, 0, 0)),
            pl.BlockSpec((1, tn), lambda i: (0, 0)),
        ],
        out_specs=[seg] * 6,
        out_shape=[
            jax.ShapeDtypeStruct((m, tn), BF16),
            jax.ShapeDtypeStruct((m, tn), BF16),
            jax.ShapeDtypeStruct((m, tn), BF16),
            jax.ShapeDtypeStruct((m, tn), F32),
            jax.ShapeDtypeStruct((m, tn), BF16),
            jax.ShapeDtypeStruct((m, tn), BF16),
        ],
        scratch_shapes=[pltpu.VMEM((POOL_HALO, tn), F32)],
        compiler_params=pltpu.CompilerParams(
            dimension_semantics=("arbitrary",), vmem_limit_bytes=VMEM_LIMIT),
        name="mix_in",
    )(x2d, nw, mod3, mod3, w_in, lb_logits, pool_w, pool_scale)


def _dot_nt(a, b):
    return lax.dot_general(a, b, (((1,), (1,)), ((), ())), preferred_element_type=F32)


def _dot_tn(a, b):
    return lax.dot_general(a, b, (((0,), (0,)), ((), ())), preferred_element_type=F32)


LEVEL_SIZES = (32, 16, 8, 4, 2, 1)
SCAN_GROUP = 8
SCAN_HEADS = 4
LOG2E = 1.4426950408889634


def _row_bcast(v, r):
    return jnp.broadcast_to(v[r:r + 1, :], v.shape)


def _level_exponents(g):
    n = len(g)
    r8 = lax.broadcasted_iota(jnp.int32, (SUBLANES, 1), 0)
    p = list(g)
    for s in (1, 2, 4):
        p = [pi + jnp.where(r8 >= s, pltpu.roll(pi, s, axis=0), 0.0) for pi in p]
    off = [None, _row_bcast(p[0], SUBLANES - 1)]
    for i in range(1, n):
        off.append(off[i] + _row_bcast(p[i], SUBLANES - 1))
    b = [p[0]] + [p[i] + off[i] for i in range(1, n)]

    exps = []
    for hs in LEVEL_SIZES:
        if hs >= SUBLANES:
            w = hs // SUBLANES
            lvl = []
            for i in range(n):
                ref = off[(i // (2 * w)) * 2 * w + w]
                lvl.append(b[i] - ref if (i // w) % 2 == 1 else ref - b[i])
        elif hs == 1:
            lvl = [jnp.where(r8 % 2 == 1, gi, 0.0) for gi in g]
        else:
            lvl = []
            for pi in p:
                ref = _row_bcast(pi, hs - 1)
                for blk in range(1, SUBLANES // (2 * hs)):
                    ref = jnp.where(r8 >= blk * 2 * hs, _row_bcast(pi, blk * 2 * hs + hs - 1), ref)
                lvl.append(-jnp.abs(pi - ref))
        exps.append(lvl)
    return exps, b, off[n]


def _scan_kernel(q_ref, k_ref, gl_ref, v_ref, gt_ref, gw_ref, c_ref, wa_ref, ba_ref,
                 o_ref, mod_ref, st_scr, attn_scr, qd_scr, kd_scr, dec_scr):
    gw = gw_ref[...]
    nt = CHUNK // SUBLANES
    n_groups = SEQ // (CHUNK * SCAN_GROUP)
    n_levels = len(LEVEL_SIZES)
    ti = lax.broadcasted_iota(jnp.int32, (CHUNK, CHUNK), 0)
    si = lax.broadcasted_iota(jnp.int32, (CHUNK, CHUNK), 1)
    owner = jnp.where(ti == si, n_levels, -1)
    for idx, hs in enumerate(LEVEL_SIZES):
        owner = jnp.where(((ti // (2 * hs)) == (si // (2 * hs))) & ((ti // hs) > (si // hs)),
                          idx, owner)
    st_scr[...] = jnp.zeros_like(st_scr)

    def tiles(a):
        return [a[i * SUBLANES:(i + 1) * SUBLANES, :] for i in range(nt)]

    owner_t = tiles(owner)

    def scaled(a_tiles, e_tiles, keep=None):
        zero = jnp.zeros_like(a_tiles[0])
        return jnp.concatenate(
            [a * jnp.exp2(e) if keep is None or keep[i] else zero
             for i, (a, e) in enumerate(zip(a_tiles, e_tiles))], axis=0).astype(BF16)

    def rows_of(c):
        return pl.ds(pl.multiple_of(c * CHUNK, CHUNK), CHUNK)

    def cols_of(hd):
        return slice(hd * HEAD_DIM, (hd + 1) * HEAD_DIM)

    def scores(hd, c):
        rows, cols = rows_of(c), cols_of(hd)
        qb = q_ref[0, rows, cols]
        kb = k_ref[0, rows, cols]
        q = tiles(qb.astype(F32))
        k = tiles(kb.astype(F32))
        exps, b, b_last = _level_exponents(tiles(gl_ref[0, rows, cols] * LOG2E))

        attn = [jnp.where(o == n_levels, d, 0.0) for o, d in zip(owner_t, tiles(_dot_nt(qb, kb)))]
        for idx, (hs, e) in enumerate(zip(LEVEL_SIZES, exps)):
            if hs >= SUBLANES:
                second = [(i // (hs // SUBLANES)) % 2 == 1 for i in range(nt)]
                first = [not s for s in second]
            else:
                second, first = [True] * nt, None
            lvl = tiles(_dot_nt(scaled(q, e, second), scaled(k, e, first)))
            for i in range(nt):
                if second[i]:
                    attn[i] = jnp.where(owner_t[i] == idx, lvl[i], attn[i])
        attn_scr[c] = jnp.concatenate(attn, axis=0).astype(BF16)
        qd_scr[rows, :] = scaled(q, b)
        kd_scr[rows, :] = scaled(k, [b_last - bi for bi in b])
        dec_scr[c] = jnp.exp2(b_last)

    def step(out_of, scores_of, extra=None):
        prev, cur = [], []
        if out_of is not None:
            hp, gp = out_of
            prev = [gp * SCAN_GROUP + j for j in range(SCAN_GROUP)]
        if scores_of is not None:
            hc, gc = scores_of
            cur = [gc * SCAN_GROUP + j for j in range(SCAN_GROUP)]
        n = len(cur)
        cut = [0, n // 4, n // 2, n]
        if prev:
            vbs = [v_ref[0, rows_of(c), cols_of(hp)] for c in prev]
            intra = [jnp.dot(attn_scr[c], vb, preferred_element_type=F32)
                     for c, vb in zip(prev, vbs)]
            kv = [_dot_tn(vb, kd_scr[rows_of(c), :]) for c, vb in zip(prev, vbs)]
        for c in cur[cut[0]:cut[1]]:
            scores(hc, c)
        if extra is not None:
            extra()
        if prev:
            st = st_scr[...]
            inter = []
            for j, c in enumerate(prev):
                inter.append(_dot_nt(qd_scr[rows_of(c), :], st.astype(BF16)))
                st = dec_scr[c][0:1, :] * st + kv[j]
            last_of_head = isinstance(gp, int) and gp == n_groups - 1
            st_scr[...] = jnp.zeros_like(st) if last_of_head else st
        for c in cur[cut[1]:cut[2]]:
            scores(hc, c)
        if prev:
            ys = []
            for j, c in enumerate(prev):
                o = intra[j] + inter[j]
                ms = jnp.mean(o * o, axis=-1, keepdims=True)
                gate = gt_ref[0, rows_of(c), cols_of(hp)].astype(F32)
                ys.append((o * lax.rsqrt(ms + EPS) * gw * gate).astype(BF16))
        for c in cur[cut[2]:cut[3]]:
            scores(hc, c)
        if prev:
            for y, c in zip(ys, prev):
                o_ref[0, rows_of(c), cols_of(hp)] = y

    def ada_block():
        mod_ref[...] = _ada_block(c_ref, wa_ref, ba_ref)

    n_heads = q_ref.shape[2] // HEAD_DIM
    step(None, (0, 0))
    for hd in range(n_heads):
        def body(g, carry, hd=hd):
            step((hd, g - 1), (hd, g))
            return carry

        lax.fori_loop(1, n_groups, body, 0)
        if hd + 1 < n_heads:
            step((hd, n_groups - 1), (hd + 1, 0))
        else:
            step((hd, n_groups - 1), None, extra=ada_block)


def _scan(q3, k3, gl3, v3, gt3, gnorm_w, c_pad, w_ada, b_ada, ada_start):
    blk = pl.BlockSpec((1, SEQ, SCAN_HEADS * HEAD_DIM), lambda b, h: (b, 0, h))
    hsteps = HGRN_HEADS // SCAN_HEADS
    n_ada = w_ada.shape[1] - ada_start
    assert n_ada == BATCH * hsteps * ADA_COLS and ada_start % ADA_COLS == 0

    def ada_map(b, h):
        return (0, ada_start // ADA_COLS + b * hsteps + h)

    return pl.pallas_call(
        _scan_kernel,
        grid=(BATCH, hsteps),
        in_specs=[blk, blk, blk, blk, blk, pl.BlockSpec((1, HEAD_DIM), lambda b, h: (0, 0)),
                  pl.BlockSpec((8, D_MODEL), lambda b, h: (0, 0)),
                  pl.BlockSpec((D_MODEL, ADA_COLS), ada_map),
                  pl.BlockSpec((1, ADA_COLS), ada_map)],
        out_specs=[blk, pl.BlockSpec((8, ADA_COLS), lambda b, h: (0, b * hsteps + h))],
        out_shape=[jax.ShapeDtypeStruct((BATCH, SEQ, HGRN_WIDTH), BF16),
                   jax.ShapeDtypeStruct((8, n_ada), F32)],
        scratch_shapes=[
            pltpu.VMEM((HEAD_DIM, HEAD_DIM), F32),
            pltpu.VMEM((SEQ // CHUNK, CHUNK, CHUNK), BF16),
            pltpu.VMEM((SEQ, HEAD_DIM), BF16),
            pltpu.VMEM((SEQ, HEAD_DIM), BF16),
            pltpu.VMEM((SEQ // CHUNK, SUBLANES, HEAD_DIM), F32),
        ],
        compiler_params=pltpu.CompilerParams(
            dimension_semantics=("arbitrary", "arbitrary"), vmem_limit_bytes=VMEM_LIMIT),
        name="hgrn_scan",
    )(q3, k3, gl3, v3, gt3, gnorm_w, c_pad, w_ada, b_ada)


def _wout_kernel(x_ref, yp_ref, yh_ref, wp_ref, wh_ref, gt_ref, o_ref):
    mix = (jnp.dot(yp_ref[...], wp_ref[...], preferred_element_type=F32)
           + jnp.dot(yh_ref[...], wh_ref[...], preferred_element_type=F32))
    o_ref[...] = x_ref[...] + gt_ref[0] * mix


def _wout(x2d, yp, yh, w_out, mod3, mod_base):
    m = x2d.shape[0]
    tm = 512
    tiles_per_batch = SEQ // tm
    mod_rows = mod3.shape[0] // BATCH
    return pl.pallas_call(
        _wout_kernel,
        grid=(m // tm,),
        in_specs=[
            pl.BlockSpec((tm, D_MODEL), lambda i: (i, 0)),
            pl.BlockSpec((tm, POOL_WIDTH), lambda i: (i, 0)),
            pl.BlockSpec((tm, HGRN_WIDTH), lambda i: (i, 0)),
            pl.BlockSpec((POOL_WIDTH, D_MODEL), lambda i: (0, 0)),
            pl.BlockSpec((HGRN_WIDTH, D_MODEL), lambda i: (1, 0)),
            pl.BlockSpec((1, 1, D_MODEL),
                         lambda i: ((i // tiles_per_batch) * mod_rows + mod_base, 0, 0)),
        ],
        out_specs=pl.BlockSpec((tm, D_MODEL), lambda i: (i, 0)),
        out_shape=jax.ShapeDtypeStruct((m, D_MODEL), F32),
        compiler_params=pltpu.CompilerParams(
            dimension_semantics=("arbitrary",), vmem_limit_bytes=VMEM_LIMIT),
        name="mix_out",
    )(x2d, yp, yh, w_out, w_out, mod3)


def kernel(x, c, w_ada, b_ada, norm1_w, ffn1_gate, ffn1_up, ffn1_down, norm2_w, w_in, pool_w,
           pool_scale, lb_logits, gnorm_w, w_out, norm3_w, ffn2_gate, ffn2_up, ffn2_down,
           final_norm_w):
    B, S, D = x.shape
    assert (B, S, D) == (BATCH, SEQ, D_MODEL) and w_ada.shape[0] == 1
    assert lb_logits.shape == (2, HGRN_WIDTH)

    c_pad = jnp.pad(c, ((0, 8 - B), (0, 0)))
    mod_a = _ada(c_pad, w_ada[0], b_ada, MOD_EARLY * D)[:B]
    mod3a = mod_a.reshape(B * MOD_EARLY, 1, D)

    x2d = x.reshape(B * S, D)
    fw = final_norm_w.reshape(1, D)

    x1, w_in_b, w_out_b, g2_b, u2_b, d2_b = _ffn(
        x2d, norm1_w, mod3a, 0, ffn1_gate[0].astype(BF16), ffn1_up[0].astype(BF16),
        ffn1_down[0].astype(BF16), fw, final_norm=False,
        cast=(w_in[0], w_out[0], ffn2_gate[0], ffn2_up[0], ffn2_down[0]))

    y_pool, q, k, gl, v, gt = _win(x1, norm2_w, mod3a, 3, w_in_b, lb_logits,
                                   pool_w[0].astype(BF16), pool_scale)
    y_hgrn, mod_b = _scan(*(a.reshape(B, S, HGRN_WIDTH) for a in (q, k, gl, v, gt)), gnorm_w,
                          c_pad, w_ada[0], b_ada, MOD_EARLY * D)
    mod3b = mod_b[:B].reshape(B * (N_MOD - MOD_EARLY), 1, D)
    x2 = _wout(x1, y_pool, y_hgrn.reshape(B * S, HGRN_WIDTH), w_out_b, mod3b, 0)

    out, = _ffn(x2, norm3_w, mod3b, 1, g2_b, u2_b, d2_b, fw, final_norm=True)
    return out.reshape(B, S, D)
```

```python
import functools

import jax
import jax.numpy as jnp
from jax import lax
from jax.experimental import pallas as pl
from jax.experimental.pallas import tpu as pltpu

F32 = jnp.float32
BF16 = jnp.bfloat16

D_MODEL = 2048
BATCH = 4
SEQ = 2048
POOL_WIDTH = 1024
POOL_WINDOWS = (2, 4, 8, 16)
POOL_GROUP_DIM = 256
HGRN_WIDTH = 1024
HGRN_HEADS = 8
HEAD_DIM = 128
IN_WIDTH = 5120
CHUNK = 64
SUBLANES = 8
BF16_SUBLANES = 16
WIN_SUB_ROWS = 256
FFN_SUB_ROWS = 512
FFN_MID_SUB_ROWS = 512
POOL_HALO = 16
D_FF = 5632
N_MOD = 9
MOD_EARLY = 5
ADA_COLS = 1024
EPS = 1e-6

VMEM_LIMIT = 60 * 1024 * 1024


def _silu(v):
    return v * jax.nn.sigmoid(v)


def _silu_tanh(v):
    return v * (0.5 * jnp.tanh(0.5 * v) + 0.5)


def _norm_modulate(x, nw, sh, sc):
    ms = jnp.mean(x * x, axis=-1, keepdims=True)
    y = x * lax.rsqrt(ms + EPS) * nw
    return y * (1.0 + sc) + sh


def _ada_block(c_ref, w_ref, b_ref):
    ca = _silu(c_ref[...]).astype(BF16)
    return jnp.dot(ca, w_ref[...].astype(BF16), preferred_element_type=F32) + b_ref[...]


def _ada_kernel(c_ref, w_ref, b_ref, o_ref):
    o_ref[...] = _ada_block(c_ref, w_ref, b_ref)


def _ada(c_pad, w_ada, b_ada, n):
    tn = ADA_COLS
    return pl.pallas_call(
        _ada_kernel,
        grid=(n // tn,),
        in_specs=[
            pl.BlockSpec((8, D_MODEL), lambda j: (0, 0)),
            pl.BlockSpec((D_MODEL, tn), lambda j: (0, j)),
            pl.BlockSpec((1, tn), lambda j: (0, j)),
        ],
        out_specs=pl.BlockSpec((8, tn), lambda j: (0, j)),
        out_shape=jax.ShapeDtypeStruct((8, n), F32),
        compiler_params=pltpu.CompilerParams(
            dimension_semantics=("arbitrary",), vmem_limit_bytes=VMEM_LIMIT),
        name="ada_mod",
    )(c_pad, w_ada, b_ada)


def _ffn_kernel(*refs, nj, final_norm, n_cast):
    x_ref, nw_ref, sh_ref, sc_ref, gt_ref, wg_ref, wu_ref, wd_ref, fw_ref = refs[:9]
    cast_in = refs[9:9 + n_cast]
    o_ref = refs[9 + n_cast]
    cast_out = refs[10 + n_cast:10 + 2 * n_cast]
    h_scr, = refs[10 + 2 * n_cast:]
    j = pl.program_id(1)
    tm = x_ref.shape[0]
    subs = [pl.ds(r, FFN_SUB_ROWS) for r in range(0, tm, FFN_SUB_ROWS)]
    mid_rows = FFN_MID_SUB_ROWS if n_cast else tm
    mid_subs = [pl.ds(r, mid_rows) for r in range(0, tm, mid_rows)]

    for w_ref, wb_ref in zip(cast_in, cast_out):
        wb_ref[...] = w_ref[...].astype(BF16)

    def partial_ffn(rows):
        h = h_scr[rows, :]
        g = jnp.dot(h, wg_ref[...], preferred_element_type=F32)
        u = jnp.dot(h, wu_ref[...], preferred_element_type=F32)
        a = (_silu_tanh(g) * u).astype(BF16)
        return jnp.dot(a, wd_ref[...], preferred_element_type=F32)

    @pl.when(j == 0)
    def _():
        for rows in subs:
            h = _norm_modulate(x_ref[rows, :], nw_ref[...], sh_ref[0], sc_ref[0])
            h_scr[rows, :] = h.astype(BF16)
            o_ref[rows, :] = partial_ffn(rows)

    @pl.when((j > 0) & (j < nj - 1))
    def _():
        for rows in mid_subs:
            o_ref[rows, :] += partial_ffn(rows)

    @pl.when(j == nj - 1)
    def _():
        for rows in subs:
            out = x_ref[rows, :] + 0.5 * gt_ref[0] * (o_ref[rows, :] + partial_ffn(rows))
            if final_norm:
                ms = jnp.mean(out * out, axis=-1, keepdims=True)
                out = out * lax.rsqrt(ms + EPS) * fw_ref[...]
            o_ref[rows, :] = out


def _ffn(x2d, nw, mod3, mod_base, wg, wu, wd, fw, *, final_norm, cast=()):
    m = x2d.shape[0]
    tm, tf = 1024, 512
    ni, nj = m // tm, D_FF // tf
    tiles_per_batch = SEQ // tm
    mod_rows = mod3.shape[0] // BATCH

    def mod_map(k):
        return lambda i, j: ((i // tiles_per_batch) * mod_rows + mod_base + k, 0, 0)

    cast_specs = []
    for w in cast:
        rows = w.shape[0]
        nblk = max(n for n in range(1, ni * nj + 1) if rows % (BF16_SUBLANES * n) == 0)
        cast_specs.append(pl.BlockSpec(
            (rows // nblk, w.shape[1]),
            lambda i, j, nblk=nblk: (jnp.minimum(i * nj + j, nblk - 1), 0)))

    kern = functools.partial(_ffn_kernel, nj=nj, final_norm=final_norm, n_cast=len(cast))
    res = pl.pallas_call(
        kern,
        grid=(ni, nj),
        in_specs=[
            pl.BlockSpec((tm, D_MODEL), lambda i, j: (i, 0)),
            pl.BlockSpec((1, D_MODEL), lambda i, j: (0, 0)),
            pl.BlockSpec((1, 1, D_MODEL), mod_map(0)),
            pl.BlockSpec((1, 1, D_MODEL), mod_map(1)),
            pl.BlockSpec((1, 1, D_MODEL), mod_map(2)),
            pl.BlockSpec((D_MODEL, tf), lambda i, j: (0, j)),
            pl.BlockSpec((D_MODEL, tf), lambda i, j: (0, j)),
            pl.BlockSpec((tf, D_MODEL), lambda i, j: (j, 0)),
            pl.BlockSpec((1, D_MODEL), lambda i, j: (0, 0)),
        ] + cast_specs,
        out_specs=[pl.BlockSpec((tm, D_MODEL), lambda i, j: (i, 0))] + cast_specs,
        out_shape=[jax.ShapeDtypeStruct((m, D_MODEL), F32)]
        + [jax.ShapeDtypeStruct(w.shape, BF16) for w in cast],
        scratch_shapes=[pltpu.VMEM((tm, D_MODEL), BF16)],
        compiler_params=pltpu.CompilerParams(
            dimension_semantics=("arbitrary", "arbitrary"), vmem_limit_bytes=VMEM_LIMIT),
        name="ffn_final" if final_norm else "ffn",
    )(x2d, nw, mod3, mod3, mod3, wg, wu, wd, fw, *cast)
    return res


def _pool_group(ext, u, t_abs, window, pw, ps):
    s, k = ext, 1
    while k < window:
        s = s + pltpu.roll(s, k, axis=0)
        k *= 2
    cnt = jnp.minimum(t_abs + 1, window).astype(F32)
    pooled = s[POOL_HALO:, :] * (1.0 / cnt) - u
    return jnp.dot(pooled.astype(BF16), pw, preferred_element_type=F32) * ps


def _win_kernel(x_ref, nw_ref, sh_ref, sc_ref, w_ref, lbl_ref, pw_ref, ps_ref,
                yp_ref, q_ref, k_ref, gl_ref, v_ref, gt_ref, tail_scr, *, tiles_per_batch):
    tm, tn = yp_ref.shape
    cg = POOL_GROUP_DIM
    lbl = lbl_ref[...]
    pe = jnp.exp(lbl - jnp.max(lbl, axis=0, keepdims=True))
    p = pe / jnp.sum(pe, axis=0, keepdims=True)
    lb = (p[0:1, :] + p[1:2, :]) - p[0:1, :]

    tile_in_seq = pl.program_id(0) % tiles_per_batch

    @pl.when(tile_in_seq == 0)
    def _():
        tail_scr[...] = jnp.zeros_like(tail_scr)

    tail = tail_scr[...]
    row = lax.broadcasted_iota(jnp.int32, (WIN_SUB_ROWS, 1), 0)

    for r in range(0, tm, WIN_SUB_ROWS):
        rows = pl.ds(r, WIN_SUB_ROWS)
        h = _norm_modulate(x_ref[rows, :], nw_ref[...], sh_ref[0], sc_ref[0]).astype(BF16)

        def project(seg):
            return jnp.dot(h, w_ref[:, seg * tn:(seg + 1) * tn], preferred_element_type=F32)

        u = project(0)
        zq = project(1)
        ext = jnp.concatenate([tail, u], axis=0)
        tail = u[WIN_SUB_ROWS - POOL_HALO:, :]
        t_abs = tile_in_seq * tm + r + row
        for g, window in enumerate(POOL_WINDOWS):
            cols = slice(g * cg, (g + 1) * cg)
            y = _pool_group(ext[:, cols], u[:, cols], t_abs, window, pw_ref[g], ps_ref[:, cols])
            yp_ref[rows, cols] = y.astype(BF16)
        zf = project(2)
        q_ref[rows, :] = _silu(zq).astype(BF16)
        zv = project(3)
        forget = lb + (1.0 - lb) * jax.nn.sigmoid(zf)
        k_ref[rows, :] = (1.0 - forget).astype(BF16)
        gl_ref[rows, :] = jnp.log(forget)
        zg = project(4)
        v_ref[rows, :] = zv.astype(BF16)
        gt_ref[rows, :] = _silu(zg).astype(BF16)
    tail_scr[...] = tail


def _win(x2d, nw, mod3, mod_base, w_in, lb_logits, pool_w, pool_scale):
    m = x2d.shape[0]
    tm, tn = 512, 1024
    assert IN_WIDTH == 5 * tn and POOL_WIDTH == tn and HGRN_WIDTH == tn
    assert max(POOL_WINDOWS) <= POOL_HALO and tn == len(POOL_WINDOWS) * POOL_GROUP_DIM
    tiles_per_batch = SEQ // tm
    mod_rows = mod3.shape[0] // BATCH

    def mod_map(k):
        return lambda i: ((i // tiles_per_batch) * mod_rows + mod_base + k, 0, 0)

    seg = pl.BlockSpec((tm, tn), lambda i: (i, 0))
    return pl.pallas_call(
        functools.partial(_win_kernel, tiles_per_batch=tiles_per_batch),
        grid=(m // tm,),
        in_specs=[
            pl.BlockSpec((tm, D_MODEL), lambda i: (i, 0)),
            pl.BlockSpec((1, D_MODEL), lambda i: (0, 0)),
            pl.BlockSpec((1, 1, D_MODEL), mod_map(0)),
            pl.BlockSpec((1, 1, D_MODEL), mod_map(1)),
            pl.BlockSpec((D_MODEL, IN_WIDTH), lambda i: (0, 0), pipeline_mode=pl.Buffered(1)),
            pl.BlockSpec((2, tn), lambda i: (0, 0)),
            pl.BlockSpec(pool_w.shape, lambda i: (0, 0, 0)),
            pl.BlockSpec((1, tn), lambda i: (0, 0)),
        ],
        out_specs=[seg] * 6,
        out_shape=[
            jax.ShapeDtypeStruct((m, tn), BF16),
            jax.ShapeDtypeStruct((m, tn), BF16),
            jax.ShapeDtypeStruct((m, tn), BF16),
            jax.ShapeDtypeStruct((m, tn), F32),
            jax.ShapeDtypeStruct((m, tn), BF16),
            jax.ShapeDtypeStruct((m, tn), BF16),
        ],
        scratch_shapes=[pltpu.VMEM((POOL_HALO, tn), F32)],
        compiler_params=pltpu.CompilerParams(
            dimension_semantics=("arbitrary",), vmem_limit_bytes=VMEM_LIMIT),
        name="mix_in",
    )(x2d, nw, mod3, mod3, w_in, lb_logits, pool_w, pool_scale)


def _dot_nt(a, b):
    return lax.dot_general(a, b, (((1,), (1,)), ((), ())), preferred_element_type=F32)


def _dot_tn(a, b):
    return lax.dot_general(a, b, (((0,), (0,)), ((), ())), preferred_element_type=F32)


LEVEL_SIZES = (32, 16, 8, 4, 2, 1)
SCAN_GROUP = 8
SCAN_HEADS = 4
LOG2E = 1.4426950408889634


def _row_bcast(v, r):
    return jnp.broadcast_to(v[r:r + 1, :], v.shape)


def _level_exponents(g):
    n = len(g)
    r8 = lax.broadcasted_iota(jnp.int32, (SUBLANES, 1), 0)
    p = list(g)
    for s in (1, 2, 4):
        p = [pi + jnp.where(r8 >= s, pltpu.roll(pi, s, axis=0), 0.0) for pi in p]
    off = [None, _row_bcast(p[0], SUBLANES - 1)]
    for i in range(1, n):
        off.append(off[i] + _row_bcast(p[i], SUBLANES - 1))
    b = [p[0]] + [p[i] + off[i] for i in range(1, n)]

    exps = []
    for hs in LEVEL_SIZES:
        if hs >= SUBLANES:
            w = hs // SUBLANES
            lvl = []
            for i in range(n):
                ref = off[(i // (2 * w)) * 2 * w + w]
                lvl.append(b[i] - ref if (i // w) % 2 == 1 else ref - b[i])
        elif hs == 1:
            lvl = [jnp.where(r8 % 2 == 1, gi, 0.0) for gi in g]
        else:
            lvl = []
            for pi in p:
                ref = _row_bcast(pi, hs - 1)
                for blk in range(1, SUBLANES // (2 * hs)):
                    ref = jnp.where(r8 >= blk * 2 * hs, _row_bcast(pi, blk * 2 * hs + hs - 1), ref)
                lvl.append(-jnp.abs(pi - ref))
        exps.append(lvl)
    return exps, b, off[n]


def _scan_kernel(q_ref, k_ref, gl_ref, v_ref, gt_ref, gw_ref, c_ref, wa_ref, ba_ref,
                 o_ref, mod_ref, st_scr, attn_scr, qd_scr, kd_scr, dec_scr):
    gw = gw_ref[...]
    nt = CHUNK // SUBLANES
    n_groups = SEQ // (CHUNK * SCAN_GROUP)
    n_levels = len(LEVEL_SIZES)
    ti = lax.broadcasted_iota(jnp.int32, (CHUNK, CHUNK), 0)
    si = lax.broadcasted_iota(jnp.int32, (CHUNK, CHUNK), 1)
    owner = jnp.where(ti == si, n_levels, -1)
    for idx, hs in enumerate(LEVEL_SIZES):
        owner = jnp.where(((ti // (2 * hs)) == (si // (2 * hs))) & ((ti // hs) > (si // hs)),
                          idx, owner)
    st_scr[...] = jnp.zeros_like(st_scr)

    def tiles(a):
        return [a[i * SUBLANES:(i + 1) * SUBLANES, :] for i in range(nt)]

    owner_t = tiles(owner)

    def scaled(a_tiles, e_tiles, keep=None):
        zero = jnp.zeros_like(a_tiles[0])
        return jnp.concatenate(
            [a * jnp.exp2(e) if keep is None or keep[i] else zero
             for i, (a, e) in enumerate(zip(a_tiles, e_tiles))], axis=0).astype(BF16)

    def rows_of(c):
        return pl.ds(pl.multiple_of(c * CHUNK, CHUNK), CHUNK)

    def cols_of(hd):
        return slice(hd * HEAD_DIM, (hd + 1) * HEAD_DIM)

    def scores(hd, c):
        rows, cols = rows_of(c), cols_of(hd)
        qb = q_ref[0, rows, cols]
        kb = k_ref[0, rows, cols]
        q = tiles(qb.astype(F32))
        k = tiles(kb.astype(F32))
        exps, b, b_last = _level_exponents(tiles(gl_ref[0, rows, cols] * LOG2E))

        attn = [jnp.where(o == n_levels, d, 0.0) for o, d in zip(owner_t, tiles(_dot_nt(qb, kb)))]
        for idx, (hs, e) in enumerate(zip(LEVEL_SIZES, exps)):
            if hs >= SUBLANES:
                second = [(i // (hs // SUBLANES)) % 2 == 1 for i in range(nt)]
                first = [not s for s in second]
            else:
                second, first = [True] * nt, None
            lvl = tiles(_dot_nt(scaled(q, e, second), scaled(k, e, first)))
            for i in range(nt):
                if second[i]:
                    attn[i] = jnp.where(owner_t[i] == idx, lvl[i], attn[i])
        attn_scr[c] = jnp.concatenate(attn, axis=0).astype(BF16)
        qd_scr[rows, :] = scaled(q, b)
        kd_scr[rows, :] = scaled(k, [b_last - bi for bi in b])
        dec_scr[c] = jnp.exp2(b_last)

    def step(out_of, scores_of, extra=None):
        prev, cur = [], []
        if out_of is not None:
            hp, gp = out_of
            prev = [gp * SCAN_GROUP + j for j in range(SCAN_GROUP)]
        if scores_of is not None:
            hc, gc = scores_of
            cur = [gc * SCAN_GROUP + j for j in range(SCAN_GROUP)]
        n = len(cur)
        cut = [0, n // 4, n // 2, n]
        if prev:
            vbs = [v_ref[0, rows_of(c), cols_of(hp)] for c in prev]
            intra = [jnp.dot(attn_scr[c], vb, preferred_element_type=F32)
                     for c, vb in zip(prev, vbs)]
            kv = [_dot_tn(vb, kd_scr[rows_of(c), :]) for c, vb in zip(prev, vbs)]
        for c in cur[cut[0]:cut[1]]:
            scores(hc, c)
        if extra is not None:
            extra()
        if prev:
            st = st_scr[...]
            inter = []
            for j, c in enumerate(prev):
                inter.append(_dot_nt(qd_scr[rows_of(c), :], st.astype(BF16)))
                st = dec_scr[c][0:1, :] * st + kv[j]
            last_of_head = isinstance(gp, int) and gp == n_groups - 1
            st_scr[...] = jnp.zeros_like(st) if last_of_head else st
        for c in cur[cut[1]:cut[2]]:
            scores(hc, c)
        if prev:
            ys = []
            for j, c in enumerate(prev):
                o = intra[j] + inter[j]
                ms = jnp.mean(o * o, axis=-1, keepdims=True)
                gate = gt_ref[0, rows_of(c), cols_of(hp)].astype(F32)
                ys.append((o * lax.rsqrt(ms + EPS) * gw * gate).astype(BF16))
        for c in cur[cut[2]:cut[3]]:
            scores(hc, c)
        if prev:
            for y, c in zip(ys, prev):
                o_ref[0, rows_of(c), cols_of(hp)] = y

    def ada_block():
        mod_ref[...] = _ada_block(c_ref, wa_ref, ba_ref)

    n_heads = q_ref.shape[2] // HEAD_DIM
    step(None, (0, 0))
    for hd in range(n_heads):
        def body(g, carry, hd=hd):
            step((hd, g - 1), (hd, g))
            return carry

        lax.fori_loop(1, n_groups, body, 0)
        if hd + 1 < n_heads:
            step((hd, n_groups - 1), (hd + 1, 0))
        else:
            step((hd, n_groups - 1), None, extra=ada_block)


def _scan(q3, k3, gl3, v3, gt3, gnorm_w, c_pad, w_ada, b_ada, ada_start):
    blk = pl.BlockSpec((1, SEQ, SCAN_HEADS * HEAD_DIM), lambda b, h: (b, 0, h))
    hsteps = HGRN_HEADS // SCAN_HEADS
    n_ada = w_ada.shape[1] - ada_start
    assert n_ada == BATCH * hsteps * ADA_COLS and ada_start % ADA_COLS == 0

    def ada_map(b, h):
        return (0, ada_start // ADA_COLS + b * hsteps + h)

    return pl.pallas_call(
        _scan_kernel,
        grid=(BATCH, hsteps),
        in_specs=[blk, blk, blk, blk, blk, pl.BlockSpec((1, HEAD_DIM), lambda b, h: (0, 0)),
                  pl.BlockSpec((8, D_MODEL), lambda b, h: (0, 0)),
                  pl.BlockSpec((D_MODEL, ADA_COLS), ada_map),
                  pl.BlockSpec((1, ADA_COLS), ada_map)],
        out_specs=[blk, pl.BlockSpec((8, ADA_COLS), lambda b, h: (0, b * hsteps + h))],
        out_shape=[jax.ShapeDtypeStruct((BATCH, SEQ, HGRN_WIDTH), BF16),
                   jax.ShapeDtypeStruct((8, n_ada), F32)],
        scratch_shapes=[
            pltpu.VMEM((HEAD_DIM, HEAD_DIM), F32),
            pltpu.VMEM((SEQ // CHUNK, CHUNK, CHUNK), BF16),
            pltpu.VMEM((SEQ, HEAD_DIM), BF16),
            pltpu.VMEM((SEQ, HEAD_DIM), BF16),
            pltpu.VMEM((SEQ // CHUNK, SUBLANES, HEAD_DIM), F32),
        ],
        compiler_params=pltpu.CompilerParams(
            dimension_semantics=("arbitrary", "arbitrary"), vmem_limit_bytes=VMEM_LIMIT),
        name="hgrn_scan",
    )(q3, k3, gl3, v3, gt3, gnorm_w, c_pad, w_ada, b_ada)


def _wout_kernel(x_ref, yp_ref, yh_ref, wp_ref, wh_ref, gt_ref, o_ref):
    mix = (jnp.dot(yp_ref[...], wp_ref[...], preferred_element_type=F32)
           + jnp.dot(yh_ref[...], wh_ref[...], preferred_element_type=F32))
    o_ref[...] = x_ref[...] + gt_ref[0] * mix


def _wout(x2d, yp, yh, w_out, mod3, mod_base):
    m = x2d.shape[0]
    tm = 512
    tiles_per_batch = SEQ // tm
    mod_rows = mod3.shape[0] // BATCH
    return pl.pallas_call(
        _wout_kernel,
        grid=(m // tm,),
        in_specs=[
            pl.BlockSpec((tm, D_MODEL), lambda i: (i, 0)),
            pl.BlockSpec((tm, POOL_WIDTH), lambda i: (i, 0)),
            pl.BlockSpec((tm, HGRN_WIDTH), lambda i: (i, 0)),
            pl.BlockSpec((POOL_WIDTH, D_MODEL), lambda i: (0, 0)),
            pl.BlockSpec((HGRN_WIDTH, D_MODEL), lambda i: (1, 0)),
            pl.BlockSpec((1, 1, D_MODEL),
                         lambda i: ((i // tiles_per_batch) * mod_rows + mod_base, 0, 0)),
        ],
        out_specs=pl.BlockSpec((tm, D_MODEL), lambda i: (i, 0)),
        out_shape=jax.ShapeDtypeStruct((m, D_MODEL), F32),
        compiler_params=pltpu.CompilerParams(
            dimension_semantics=("arbitrary",), vmem_limit_bytes=VMEM_LIMIT),
        name="mix_out",
    )(x2d, yp, yh, w_out, w_out, mod3)


def kernel(x, c, w_ada, b_ada, norm1_w, ffn1_gate, ffn1_up, ffn1_down, norm2_w, w_in, pool_w,
           pool_scale, lb_logits, gnorm_w, w_out, norm3_w, ffn2_gate, ffn2_up, ffn2_down,
           final_norm_w):
    B, S, D = x.shape
    assert (B, S, D) == (BATCH, SEQ, D_MODEL) and w_ada.shape[0] == 1
    assert lb_logits.shape == (2, HGRN_WIDTH)

    c_pad = jnp.pad(c, ((0, 8 - B), (0, 0)))
    mod_a = _ada(c_pad, w_ada[0], b_ada, MOD_EARLY * D)[:B]
    mod3a = mod_a.reshape(B * MOD_EARLY, 1, D)

    x2d = x.reshape(B * S, D)
    fw = final_norm_w.reshape(1, D)

    x1, w_in_b, w_out_b, g2_b, u2_b, d2_b = _ffn(
        x2d, norm1_w, mod3a, 0, ffn1_gate[0].astype(BF16), ffn1_up[0].astype(BF16),
        ffn1_down[0].astype(BF16), fw, final_norm=False,
        cast=(w_in[0], w_out[0], ffn2_gate[0], ffn2_up[0], ffn2_down[0]))

    y_pool, q, k, gl, v, gt = _win(x1, norm2_w, mod3a, 3, w_in_b, lb_logits,
                                   pool_w[0].astype(BF16), pool_scale)
    y_hgrn, mod_b = _scan(*(a.reshape(B, S, HGRN_WIDTH) for a in (q, k, gl, v, gt)), gnorm_w,
                          c_pad, w_ada[0], b_ada, MOD_EARLY * D)
    mod3b = mod_b[:B].reshape(B * (N_MOD - MOD_EARLY), 1, D)
    x2 = _wout(x1, y_pool, y_hgrn.reshape(B * S, HGRN_WIDTH), w_out_b, mod3b, 0)

    out, = _ffn(x2, norm3_w, mod3b, 1, g2_b, u2_b, d2_b, fw, final_norm=True)
    return out.reshape(B, S, D)
```

```python
import functools

import jax
import jax.numpy as jnp
from jax import lax
from jax.experimental import pallas as pl
from jax.experimental.pallas import tpu as pltpu

F32 = jnp.float32
BF16 = jnp.bfloat16

D_MODEL = 2048
BATCH = 4
SEQ = 2048
POOL_WIDTH = 1024
POOL_WINDOWS = (2, 4, 8, 16)
POOL_GROUP_DIM = 256
HGRN_WIDTH = 1024
HGRN_HEADS = 8
HEAD_DIM = 128
IN_WIDTH = 5120
CHUNK = 64
SUBLANES = 8
BF16_SUBLANES = 16
WIN_SUB_ROWS = 256
FFN_SUB_ROWS = 512
FFN_MID_SUB_ROWS = 512
WOUT_SUB_ROWS = 512
POOL_HALO = 16
D_FF = 5632
N_MOD = 9
MOD_EARLY = 5
ADA_COLS = 1024
EPS = 1e-6

VMEM_LIMIT = 60 * 1024 * 1024


def _silu(v):
    return v * jax.nn.sigmoid(v)


def _silu_tanh(v):
    return v * (0.5 * jnp.tanh(0.5 * v) + 0.5)


def _norm_modulate(x, nw, sh, sc):
    ms = jnp.mean(x * x, axis=-1, keepdims=True)
    y = x * lax.rsqrt(ms + EPS) * nw
    return y * (1.0 + sc) + sh


def _ada_block(c_ref, w_ref, b_ref):
    ca = _silu(c_ref[...]).astype(BF16)
    return jnp.dot(ca, w_ref[...].astype(BF16), preferred_element_type=F32) + b_ref[...]


def _ada_kernel(c_ref, w_ref, b_ref, o_ref):
    o_ref[...] = _ada_block(c_ref, w_ref, b_ref)


def _ada(c_pad, w_ada, b_ada, n):
    tn = ADA_COLS
    return pl.pallas_call(
        _ada_kernel,
        grid=(n // tn,),
        in_specs=[
            pl.BlockSpec((8, D_MODEL), lambda j: (0, 0)),
            pl.BlockSpec((D_MODEL, tn), lambda j: (0, j)),
            pl.BlockSpec((1, tn), lambda j: (0, j)),
        ],
        out_specs=pl.BlockSpec((8, tn), lambda j: (0, j)),
        out_shape=jax.ShapeDtypeStruct((8, n), F32),
        compiler_params=pltpu.CompilerParams(
            dimension_semantics=("arbitrary",), vmem_limit_bytes=VMEM_LIMIT),
        name="ada_mod",
    )(c_pad, w_ada, b_ada)


def _ffn_kernel(*refs, nj, final_norm, n_cast):
    x_ref, nw_ref, sh_ref, sc_ref, gt_ref, wg_ref, wu_ref, wd_ref, fw_ref = refs[:9]
    cast_in = refs[9:9 + n_cast]
    o_ref = refs[9 + n_cast]
    cast_out = refs[10 + n_cast:10 + 2 * n_cast]
    h_scr, = refs[10 + 2 * n_cast:]
    j = pl.program_id(1)
    tm = x_ref.shape[0]
    subs = [pl.ds(r, FFN_SUB_ROWS) for r in range(0, tm, FFN_SUB_ROWS)]
    mid_rows = FFN_MID_SUB_ROWS if n_cast else tm
    mid_subs = [pl.ds(r, mid_rows) for r in range(0, tm, mid_rows)]

    for w_ref, wb_ref in zip(cast_in, cast_out):
        wb_ref[...] = w_ref[...].astype(BF16)

    def partial_ffn(rows):
        h = h_scr[rows, :]
        g = jnp.dot(h, wg_ref[...], preferred_element_type=F32)
        u = jnp.dot(h, wu_ref[...], preferred_element_type=F32)
        a = (_silu_tanh(g) * u).astype(BF16)
        return jnp.dot(a, wd_ref[...], preferred_element_type=F32)

    @pl.when(j == 0)
    def _():
        for rows in subs:
            h = _norm_modulate(x_ref[rows, :], nw_ref[...], sh_ref[0], sc_ref[0])
            h_scr[rows, :] = h.astype(BF16)
            o_ref[rows, :] = partial_ffn(rows)

    @pl.when((j > 0) & (j < nj - 1))
    def _():
        for rows in mid_subs:
            o_ref[rows, :] += partial_ffn(rows)

    @pl.when(j == nj - 1)
    def _():
        for rows in subs:
            out = x_ref[rows, :] + 0.5 * gt_ref[0] * (o_ref[rows, :] + partial_ffn(rows))
            if final_norm:
                ms = jnp.mean(out * out, axis=-1, keepdims=True)
                out = out * lax.rsqrt(ms + EPS) * fw_ref[...]
            o_ref[rows, :] = out


def _ffn(x2d, nw, mod3, mod_base, wg, wu, wd, fw, *, final_norm, cast=()):
    m = x2d.shape[0]
    tm, tf = 1024, 512
    ni, nj = m // tm, D_FF // tf
    tiles_per_batch = SEQ // tm
    mod_rows = mod3.shape[0] // BATCH

    def mod_map(k):
        return lambda i, j: ((i // tiles_per_batch) * mod_rows + mod_base + k, 0, 0)

    cast_specs = []
    for w in cast:
        rows = w.shape[0]
        nblk = max(n for n in range(1, ni * nj + 1) if rows % (BF16_SUBLANES * n) == 0)
        cast_specs.append(pl.BlockSpec(
            (rows // nblk, w.shape[1]),
            lambda i, j, nblk=nblk: (jnp.minimum(i * nj + j, nblk - 1), 0)))

    kern = functools.partial(_ffn_kernel, nj=nj, final_norm=final_norm, n_cast=len(cast))
    res = pl.pallas_call(
        kern,
        grid=(ni, nj),
        in_specs=[
            pl.BlockSpec((tm, D_MODEL), lambda i, j: (i, 0)),
            pl.BlockSpec((1, D_MODEL), lambda i, j: (0, 0)),
            pl.BlockSpec((1, 1, D_MODEL), mod_map(0)),
            pl.BlockSpec((1, 1, D_MODEL), mod_map(1)),
            pl.BlockSpec((1, 1, D_MODEL), mod_map(2)),
            pl.BlockSpec((D_MODEL, tf), lambda i, j: (0, j)),
            pl.BlockSpec((D_MODEL, tf), lambda i, j: (0, j)),
            pl.BlockSpec((tf, D_MODEL), lambda i, j: (j, 0)),
            pl.BlockSpec((1, D_MODEL), lambda i, j: (0, 0)),
        ] + cast_specs,
        out_specs=[pl.BlockSpec((tm, D_MODEL), lambda i, j: (i, 0))] + cast_specs,
        out_shape=[jax.ShapeDtypeStruct((m, D_MODEL), F32)]
        + [jax.ShapeDtypeStruct(w.shape, BF16) for w in cast],
        scratch_shapes=[pltpu.VMEM((tm, D_MODEL), BF16)],
        compiler_params=pltpu.CompilerParams(
            dimension_semantics=("arbitrary", "arbitrary"), vmem_limit_bytes=VMEM_LIMIT),
        name="ffn_final" if final_norm else "ffn",
    )(x2d, nw, mod3, mod3, mod3, wg, wu, wd, fw, *cast)
    return res


def _pool_group(ext, u, t_abs, window, pw, ps):
    s, k = ext, 1
    while k < window:
        s = s + pltpu.roll(s, k, axis=0)
        k *= 2
    cnt = jnp.minimum(t_abs + 1, window).astype(F32)
    pooled = s[POOL_HALO:, :] * (1.0 / cnt) - u
    return jnp.dot(pooled.astype(BF16), pw, preferred_element_type=F32) * ps


def _win_kernel(x_ref, nw_ref, sh_ref, sc_ref, w_ref, lbl_ref, pw_ref, ps_ref,
                yp_ref, q_ref, k_ref, gl_ref, v_ref, gt_ref, tail_scr, *, tiles_per_batch):
    tm, tn = yp_ref.shape
    cg = POOL_GROUP_DIM
    lbl = lbl_ref[...]
    pe = jnp.exp(lbl - jnp.max(lbl, axis=0, keepdims=True))
    p = pe / jnp.sum(pe, axis=0, keepdims=True)
    lb = (p[0:1, :] + p[1:2, :]) - p[0:1, :]

    tile_in_seq = pl.program_id(0) % tiles_per_batch

    @pl.when(tile_in_seq == 0)
    def _():
        tail_scr[...] = jnp.zeros_like(tail_scr)

    tail = tail_scr[...]
    row = lax.broadcasted_iota(jnp.int32, (WIN_SUB_ROWS, 1), 0)

    for r in range(0, tm, WIN_SUB_ROWS):
        rows = pl.ds(r, WIN_SUB_ROWS)
        h = _norm_modulate(x_ref[rows, :], nw_ref[...], sh_ref[0], sc_ref[0]).astype(BF16)

        def project(seg):
            return jnp.dot(h, w_ref[:, seg * tn:(seg + 1) * tn], preferred_element_type=F32)

        u = project(0)
        zq = project(1)
        ext = jnp.concatenate([tail, u], axis=0)
        tail = u[WIN_SUB_ROWS - POOL_HALO:, :]
        t_abs = tile_in_seq * tm + r + row
        for g, window in enumerate(POOL_WINDOWS):
            cols = slice(g * cg, (g + 1) * cg)
            y = _pool_group(ext[:, cols], u[:, cols], t_abs, window, pw_ref[g], ps_ref[:, cols])
            yp_ref[rows, cols] = y.astype(BF16)
        zf = project(2)
        q_ref[rows, :] = _silu(zq).astype(BF16)
        zv = project(3)
        forget = lb + (1.0 - lb) * jax.nn.sigmoid(zf)
        k_ref[rows, :] = (1.0 - forget).astype(BF16)
        gl_ref[rows, :] = jnp.log(forget)
        zg = project(4)
        v_ref[rows, :] = zv.astype(BF16)
        gt_ref[rows, :] = _silu(zg).astype(BF16)
    tail_scr[...] = tail


def _win(x2d, nw, mod3, mod_base, w_in, lb_logits, pool_w, pool_scale):
    m = x2d.shape[0]
    tm, tn = 512, 1024
    assert IN_WIDTH == 5 * tn and POOL_WIDTH == tn and HGRN_WIDTH == tn
    assert max(POOL_WINDOWS) <= POOL_HALO and tn == len(POOL_WINDOWS) * POOL_GROUP_DIM
    tiles_per_batch = SEQ // tm
    mod_rows = mod3.shape[0] // BATCH

    def mod_map(k):
        return lambda i: ((i // tiles_per_batch) * mod_rows + mod_base + k, 0, 0)

    seg = pl.BlockSpec((tm, tn), lambda i: (i, 0))
    return pl.pallas_call(
        functools.partial(_win_kernel, tiles_per_batch=tiles_per_batch),
        grid=(m // tm,),
        in_specs=[
            pl.BlockSpec((tm, D_MODEL), lambda i: (i, 0)),
            pl.BlockSpec((1, D_MODEL), lambda i: (0, 0)),
            pl.BlockSpec((1, 1, D_MODEL), mod_map(0)),
            pl.BlockSpec((1, 1, D_MODEL), mod_map(1)),
            pl.BlockSpec((D_MODEL, IN_WIDTH), lambda i: (0, 0), pipeline_mode=pl.Buffered(1)),
            pl.BlockSpec((2, tn), lambda i: (0, 0)),
            pl.BlockSpec(pool_w.shape, lambda i: (0, 0, 0)),
            pl.BlockSpec((1, tn), lambda i: (0, 0)),
        ],
        out_specs=[seg] * 6,
        out_shape=[
            jax.ShapeDtypeStruct((m, tn), BF16),
            jax.ShapeDtypeStruct((m, tn), BF16),
            jax.ShapeDtypeStruct((m, tn), BF16),
            jax.ShapeDtypeStruct((m, tn), F32),
            jax.ShapeDtypeStruct((m, tn), BF16),
            jax.ShapeDtypeStruct((m, tn), BF16),
        ],
        scratch_shapes=[pltpu.VMEM((POOL_HALO, tn), F32)],
        compiler_params=pltpu.CompilerParams(
            dimension_semantics=("arbitrary",), vmem_limit_bytes=VMEM_LIMIT),
        name="mix_in",
    )(x2d, nw, mod3, mod3, w_in, lb_logits, pool_w, pool_scale)


def _dot_nt(a, b):
    return lax.dot_general(a, b, (((1,), (1,)), ((), ())), preferred_element_type=F32)


def _dot_tn(a, b):
    return lax.dot_general(a, b, (((0,), (0,)), ((), ())), preferred_element_type=F32)


LEVEL_SIZES = (32, 16, 8, 4, 2, 1)
SCAN_GROUP = 8
SCAN_HEADS = 4
LOG2E = 1.4426950408889634


def _row_bcast(v, r):
    return jnp.broadcast_to(v[r:r + 1, :], v.shape)


def _level_exponents(g):
    n = len(g)
    r8 = lax.broadcasted_iota(jnp.int32, (SUBLANES, 1), 0)
    p = list(g)
    for s in (1, 2, 4):
        p = [pi + jnp.where(r8 >= s, pltpu.roll(pi, s, axis=0), 0.0) for pi in p]
    off = [None, _row_bcast(p[0], SUBLANES - 1)]
    for i in range(1, n):
        off.append(off[i] + _row_bcast(p[i], SUBLANES - 1))
    b = [p[0]] + [p[i] + off[i] for i in range(1, n)]

    exps = []
    for hs in LEVEL_SIZES:
        if hs >= SUBLANES:
            w = hs // SUBLANES
            lvl = []
            for i in range(n):
                ref = off[(i // (2 * w)) * 2 * w + w]
                lvl.append(b[i] - ref if (i // w) % 2 == 1 else ref - b[i])
        elif hs == 1:
            lvl = [jnp.where(r8 % 2 == 1, gi, 0.0) for gi in g]
        else:
            lvl = []
            for pi in p:
                ref = _row_bcast(pi, hs - 1)
                for blk in range(1, SUBLANES // (2 * hs)):
                    ref = jnp.where(r8 >= blk * 2 * hs, _row_bcast(pi, blk * 2 * hs + hs - 1), ref)
                lvl.append(-jnp.abs(pi - ref))
        exps.append(lvl)
    return exps, b, off[n]


def _scan_kernel(q_ref, k_ref, gl_ref, v_ref, gt_ref, gw_ref, c_ref, wa_ref, ba_ref,
                 o_ref, mod_ref, st_scr, attn_scr, qd_scr, kd_scr, dec_scr):
    gw = gw_ref[...]
    nt = CHUNK // SUBLANES
    n_groups = SEQ // (CHUNK * SCAN_GROUP)
    n_levels = len(LEVEL_SIZES)
    ti = lax.broadcasted_iota(jnp.int32, (CHUNK, CHUNK), 0)
    si = lax.broadcasted_iota(jnp.int32, (CHUNK, CHUNK), 1)
    owner = jnp.where(ti == si, n_levels, -1)
    for idx, hs in enumerate(LEVEL_SIZES):
        owner = jnp.where(((ti // (2 * hs)) == (si // (2 * hs))) & ((ti // hs) > (si // hs)),
                          idx, owner)
    st_scr[...] = jnp.zeros_like(st_scr)

    def tiles(a):
        return [a[i * SUBLANES:(i + 1) * SUBLANES, :] for i in range(nt)]

    owner_t = tiles(owner)

    def scaled(a_tiles, e_tiles, keep=None):
        zero = jnp.zeros_like(a_tiles[0])
        return jnp.concatenate(
            [a * jnp.exp2(e) if keep is None or keep[i] else zero
             for i, (a, e) in enumerate(zip(a_tiles, e_tiles))], axis=0).astype(BF16)

    def rows_of(c):
        return pl.ds(pl.multiple_of(c * CHUNK, CHUNK), CHUNK)

    def cols_of(hd):
        return slice(hd * HEAD_DIM, (hd + 1) * HEAD_DIM)

    def scores(hd, c):
        rows, cols = rows_of(c), cols_of(hd)
        qb = q_ref[0, rows, cols]
        kb = k_ref[0, rows, cols]
        q = tiles(qb.astype(F32))
        k = tiles(kb.astype(F32))
        exps, b, b_last = _level_exponents(tiles(gl_ref[0, rows, cols] * LOG2E))

        attn = [jnp.where(o == n_levels, d, 0.0) for o, d in zip(owner_t, tiles(_dot_nt(qb, kb)))]
        for idx, (hs, e) in enumerate(zip(LEVEL_SIZES, exps)):
            if hs >= SUBLANES:
                second = [(i // (hs // SUBLANES)) % 2 == 1 for i in range(nt)]
                first = [not s for s in second]
            else:
                second, first = [True] * nt, None
            lvl = tiles(_dot_nt(scaled(q, e, second), scaled(k, e, first)))
            for i in range(nt):
                if second[i]:
                    attn[i] = jnp.where(owner_t[i] == idx, lvl[i], attn[i])
        attn_scr[c] = jnp.concatenate(attn, axis=0).astype(BF16)
        qd_scr[rows, :] = scaled(q, b)
        kd_scr[rows, :] = scaled(k, [b_last - bi for bi in b])
        dec_scr[c] = jnp.exp2(b_last)

    def step(out_of, scores_of, extra=None):
        prev, cur = [], []
        if out_of is not None:
            hp, gp = out_of
            prev = [gp * SCAN_GROUP + j for j in range(SCAN_GROUP)]
        if scores_of is not None:
            hc, gc = scores_of
            cur = [gc * SCAN_GROUP + j for j in range(SCAN_GROUP)]
        n = len(cur)
        cut = [0, n // 4, (3 * n) // 4, n]
        if prev:
            vbs = [v_ref[0, rows_of(c), cols_of(hp)] for c in prev]
            intra = [jnp.dot(attn_scr[c], vb, preferred_element_type=F32)
                     for c, vb in zip(prev, vbs)]
            kv = [_dot_tn(vb, kd_scr[rows_of(c), :]) for c, vb in zip(prev, vbs)]
        for c in cur[cut[0]:cut[1]]:
            scores(hc, c)
        if extra is not None:
            extra()
        if prev:
            st = st_scr[...]
            inter = []
            for j, c in enumerate(prev):
                inter.append(_dot_nt(qd_scr[rows_of(c), :], st.astype(BF16)))
                st = dec_scr[c][0:1, :] * st + kv[j]
            last_of_head = isinstance(gp, int) and gp == n_groups - 1
            st_scr[...] = jnp.zeros_like(st) if last_of_head else st
        for c in cur[cut[1]:cut[2]]:
            scores(hc, c)
        if prev:
            ys = []
            for j, c in enumerate(prev):
                o = intra[j] + inter[j]
                ms = jnp.mean(o * o, axis=-1, keepdims=True)
                gate = gt_ref[0, rows_of(c), cols_of(hp)].astype(F32)
                ys.append((o * lax.rsqrt(ms + EPS) * gw * gate).astype(BF16))
        for c in cur[cut[2]:cut[3]]:
            scores(hc, c)
        if prev:
            for y, c in zip(ys, prev):
                o_ref[0, rows_of(c), cols_of(hp)] = y

    def ada_block():
        mod_ref[...] = _ada_block(c_ref, wa_ref, ba_ref)

    n_heads = q_ref.shape[2] // HEAD_DIM
    step(None, (0, 0))
    for hd in range(n_heads):
        def body(g, carry, hd=hd):
            step((hd, g - 1), (hd, g))
            return carry

        lax.fori_loop(1, n_groups, body, 0)
        if hd + 1 < n_heads:
            step((hd, n_groups - 1), (hd + 1, 0))
        else:
            step((hd, n_groups - 1), None, extra=ada_block)


def _scan(q3, k3, gl3, v3, gt3, gnorm_w, c_pad, w_ada, b_ada, ada_start):
    blk = pl.BlockSpec((1, SEQ, SCAN_HEADS * HEAD_DIM), lambda b, h: (b, 0, h))
    hsteps = HGRN_HEADS // SCAN_HEADS
    n_ada = w_ada.shape[1] - ada_start
    assert n_ada == BATCH * hsteps * ADA_COLS and ada_start % ADA_COLS == 0

    def ada_map(b, h):
        return (0, ada_start // ADA_COLS + b * hsteps + h)

    return pl.pallas_call(
        _scan_kernel,
        grid=(BATCH, hsteps),
        in_specs=[blk, blk, blk, blk, blk, pl.BlockSpec((1, HEAD_DIM), lambda b, h: (0, 0)),
                  pl.BlockSpec((8, D_MODEL), lambda b, h: (0, 0)),
                  pl.BlockSpec((D_MODEL, ADA_COLS), ada_map),
                  pl.BlockSpec((1, ADA_COLS), ada_map)],
        out_specs=[blk, pl.BlockSpec((8, ADA_COLS), lambda b, h: (0, b * hsteps + h))],
        out_shape=[jax.ShapeDtypeStruct((BATCH, SEQ, HGRN_WIDTH), BF16),
                   jax.ShapeDtypeStruct((8, n_ada), F32)],
        scratch_shapes=[
            pltpu.VMEM((HEAD_DIM, HEAD_DIM), F32),
            pltpu.VMEM((SEQ // CHUNK, CHUNK, CHUNK), BF16),
            pltpu.VMEM((SEQ, HEAD_DIM), BF16),
            pltpu.VMEM((SEQ, HEAD_DIM), BF16),
            pltpu.VMEM((SEQ // CHUNK, SUBLANES, HEAD_DIM), F32),
        ],
        compiler_params=pltpu.CompilerParams(
            dimension_semantics=("arbitrary", "arbitrary"), vmem_limit_bytes=VMEM_LIMIT),
        name="hgrn_scan",
    )(q3, k3, gl3, v3, gt3, gnorm_w, c_pad, w_ada, b_ada)


def _wout_kernel(x_ref, yp_ref, yh_ref, wp_ref, wh_ref, gt_ref, o_ref):
    for r in range(0, x_ref.shape[0], WOUT_SUB_ROWS):
        rows = pl.ds(r, WOUT_SUB_ROWS)
        mix = (jnp.dot(yp_ref[rows, :], wp_ref[...], preferred_element_type=F32)
               + jnp.dot(yh_ref[rows, :], wh_ref[...], preferred_element_type=F32))
        o_ref[rows, :] = x_ref[rows, :] + gt_ref[0] * mix


def _wout(x2d, yp, yh, w_out, mod3, mod_base):
    m = x2d.shape[0]
    tm = 1024
    tiles_per_batch = SEQ // tm
    mod_rows = mod3.shape[0] // BATCH
    return pl.pallas_call(
        _wout_kernel,
        grid=(m // tm,),
        in_specs=[
            pl.BlockSpec((tm, D_MODEL), lambda i: (i, 0)),
            pl.BlockSpec((tm, POOL_WIDTH), lambda i: (i, 0)),
            pl.BlockSpec((tm, HGRN_WIDTH), lambda i: (i, 0)),
            pl.BlockSpec((POOL_WIDTH, D_MODEL), lambda i: (0, 0), pipeline_mode=pl.Buffered(1)),
            pl.BlockSpec((HGRN_WIDTH, D_MODEL), lambda i: (1, 0), pipeline_mode=pl.Buffered(1)),
            pl.BlockSpec((1, 1, D_MODEL),
                         lambda i: ((i // tiles_per_batch) * mod_rows + mod_base, 0, 0)),
        ],
        out_specs=pl.BlockSpec((tm, D_MODEL), lambda i: (i, 0)),
        out_shape=jax.ShapeDtypeStruct((m, D_MODEL), F32),
        compiler_params=pltpu.CompilerParams(
            dimension_semantics=("arbitrary",), vmem_limit_bytes=VMEM_LIMIT),
        name="mix_out",
    )(x2d, yp, yh, w_out, w_out, mod3)


def kernel(x, c, w_ada, b_ada, norm1_w, ffn1_gate, ffn1_up, ffn1_down, norm2_w, w_in, pool_w,
           pool_scale, lb_logits, gnorm_w, w_out, norm3_w, ffn2_gate, ffn2_up, ffn2_down,
           final_norm_w):
    B, S, D = x.shape
    assert (B, S, D) == (BATCH, SEQ, D_MODEL) and w_ada.shape[0] == 1
    assert lb_logits.shape == (2, HGRN_WIDTH)

    c_pad = jnp.pad(c, ((0, 8 - B), (0, 0)))
    mod_a = _ada(c_pad, w_ada[0], b_ada, MOD_EARLY * D)[:B]
    mod3a = mod_a.reshape(B * MOD_EARLY, 1, D)

    x2d = x.reshape(B * S, D)
    fw = final_norm_w.reshape(1, D)

    x1, w_in_b, w_out_b, g2_b, u2_b, d2_b = _ffn(
        x2d, norm1_w, mod3a, 0, ffn1_gate[0].astype(BF16), ffn1_up[0].astype(BF16),
        ffn1_down[0].astype(BF16), fw, final_norm=False,
        cast=(w_in[0], w_out[0], ffn2_gate[0], ffn2_up[0], ffn2_down[0]))

    y_pool, q, k, gl, v, gt = _win(x1, norm2_w, mod3a, 3, w_in_b, lb_logits,
                                   pool_w[0].astype(BF16), pool_scale)
    y_hgrn, mod_b = _scan(*(a.reshape(B, S, HGRN_WIDTH) for a in (q, k, gl, v, gt)), gnorm_w,
                          c_pad, w_ada[0], b_ada, MOD_EARLY * D)
    mod3b = mod_b[:B].reshape(B * (N_MOD - MOD_EARLY), 1, D)
    x2 = _wout(x1, y_pool, y_hgrn.reshape(B * S, HGRN_WIDTH), w_out_b, mod3b, 0)

    out, = _ffn(x2, norm3_w, mod3b, 1, g2_b, u2_b, d2_b, fw, final_norm=True)
    return out.reshape(B, S, D)
```

```python
import functools

import jax
import jax.numpy as jnp
from jax import lax
from jax.experimental import pallas as pl
from jax.experimental.pallas import tpu as pltpu

F32 = jnp.float32
BF16 = jnp.bfloat16

D_MODEL = 2048
BATCH = 4
SEQ = 2048
POOL_WIDTH = 1024
POOL_WINDOWS = (2, 4, 8, 16)
POOL_GROUP_DIM = 256
HGRN_WIDTH = 1024
HGRN_HEADS = 8
HEAD_DIM = 128
IN_WIDTH = 5120
CHUNK = 64
SUBLANES = 8
BF16_SUBLANES = 16
WIN_SUB_ROWS = 256
W_WINDOW_COLS = 512
FFN_SUB_ROWS = 512
FFN_MID_SUB_ROWS = 512
WOUT_SUB_ROWS = 512
POOL_HALO = 16
D_FF = 5632
N_MOD = 9
MOD_EARLY = 5
ADA_COLS = 1024
EPS = 1e-6

VMEM_LIMIT = 60 * 1024 * 1024


def _silu(v):
    return v * jax.nn.sigmoid(v)


def _silu_tanh(v):
    return v * (0.5 * jnp.tanh(0.5 * v) + 0.5)


def _norm_modulate(x, nw, sh, sc):
    ms = jnp.mean(x * x, axis=-1, keepdims=True)
    y = x * lax.rsqrt(ms + EPS) * nw
    return y * (1.0 + sc) + sh


def _ada_block(c_ref, w_ref, b_ref):
    ca = _silu(c_ref[...]).astype(BF16)
    return jnp.dot(ca, w_ref[...].astype(BF16), preferred_element_type=F32) + b_ref[...]


def _ada_kernel(c_ref, w_ref, b_ref, o_ref):
    o_ref[...] = _ada_block(c_ref, w_ref, b_ref)


def _ada(c_pad, w_ada, b_ada, n):
    tn = ADA_COLS
    return pl.pallas_call(
        _ada_kernel,
        grid=(n // tn,),
        in_specs=[
            pl.BlockSpec((8, D_MODEL), lambda j: (0, 0)),
            pl.BlockSpec((D_MODEL, tn), lambda j: (0, j)),
            pl.BlockSpec((1, tn), lambda j: (0, j)),
        ],
        out_specs=pl.BlockSpec((8, tn), lambda j: (0, j)),
        out_shape=jax.ShapeDtypeStruct((8, n), F32),
        compiler_params=pltpu.CompilerParams(
            dimension_semantics=("arbitrary",), vmem_limit_bytes=VMEM_LIMIT),
        name="ada_mod",
    )(c_pad, w_ada, b_ada)


def _ffn_kernel(*refs, nj, final_norm, n_cast):
    n_wd = D_MODEL // W_WINDOW_COLS
    x_ref, nw_ref, sh_ref, sc_ref, gt_ref, wg_ref, wu_ref = refs[:7]
    wd_refs = refs[7:7 + n_wd]
    refs = refs[7 + n_wd:]
    fw_ref = refs[0]
    cast_in = refs[1:1 + n_cast]
    o_ref = refs[1 + n_cast]
    cast_out = refs[2 + n_cast:2 + 2 * n_cast]
    h_scr, = refs[2 + 2 * n_cast:]
    j = pl.program_id(1)
    tm = x_ref.shape[0]
    subs = [pl.ds(r, FFN_SUB_ROWS) for r in range(0, tm, FFN_SUB_ROWS)]
    mid_rows = FFN_MID_SUB_ROWS if n_cast else tm
    mid_subs = [pl.ds(r, mid_rows) for r in range(0, tm, mid_rows)]

    for w_ref, wb_ref in zip(cast_in, cast_out):
        wb_ref[...] = w_ref[...].astype(BF16)

    def partial_ffn(rows):
        h = h_scr[rows, :]
        g = jnp.dot(h, wg_ref[...], preferred_element_type=F32)
        u = jnp.dot(h, wu_ref[...], preferred_element_type=F32)
        a = (_silu_tanh(g) * u).astype(BF16)
        return jnp.concatenate(
            [jnp.dot(a, wd[...], preferred_element_type=F32) for wd in wd_refs], axis=1)

    @pl.when(j == 0)
    def _():
        for rows in subs:
            h = _norm_modulate(x_ref[rows, :], nw_ref[...], sh_ref[0], sc_ref[0])
            h_scr[rows, :] = h.astype(BF16)
            o_ref[rows, :] = partial_ffn(rows)

    @pl.when((j > 0) & (j < nj - 1))
    def _():
        for rows in mid_subs:
            o_ref[rows, :] += partial_ffn(rows)

    @pl.when(j == nj - 1)
    def _():
        for rows in subs:
            out = x_ref[rows, :] + 0.5 * gt_ref[0] * (o_ref[rows, :] + partial_ffn(rows))
            if final_norm:
                ms = jnp.mean(out * out, axis=-1, keepdims=True)
                out = out * lax.rsqrt(ms + EPS) * fw_ref[...]
            o_ref[rows, :] = out


def _ffn(x2d, nw, mod3, mod_base, wg, wu, wd, fw, *, final_norm, cast=()):
    m = x2d.shape[0]
    tm, tf = 1024, 512
    ni, nj = m // tm, D_FF // tf
    tiles_per_batch = SEQ // tm
    mod_rows = mod3.shape[0] // BATCH

    def mod_map(k):
        return lambda i, j: ((i // tiles_per_batch) * mod_rows + mod_base + k, 0, 0)

    cast_specs = []
    for w in cast:
        rows = w.shape[0]
        nblk = max(n for n in range(1, ni * nj + 1) if rows % (BF16_SUBLANES * n) == 0)
        cast_specs.append(pl.BlockSpec(
            (rows // nblk, w.shape[1]),
            lambda i, j, nblk=nblk: (jnp.minimum(i * nj + j, nblk - 1), 0)))

    kern = functools.partial(_ffn_kernel, nj=nj, final_norm=final_norm, n_cast=len(cast))
    res = pl.pallas_call(
        kern,
        grid=(ni, nj),
        in_specs=[
            pl.BlockSpec((tm, D_MODEL), lambda i, j: (i, 0)),
            pl.BlockSpec((1, D_MODEL), lambda i, j: (0, 0)),
            pl.BlockSpec((1, 1, D_MODEL), mod_map(0)),
            pl.BlockSpec((1, 1, D_MODEL), mod_map(1)),
            pl.BlockSpec((1, 1, D_MODEL), mod_map(2)),
            pl.BlockSpec((D_MODEL, tf), lambda i, j: (0, j)),
            pl.BlockSpec((D_MODEL, tf), lambda i, j: (0, j)),
        ] + [
            pl.BlockSpec((tf, W_WINDOW_COLS), lambda i, j, s=s: (j, s))
            for s in range(D_MODEL // W_WINDOW_COLS)
        ] + [
            pl.BlockSpec((1, D_MODEL), lambda i, j: (0, 0)),
        ] + cast_specs,
        out_specs=[pl.BlockSpec((tm, D_MODEL), lambda i, j: (i, 0))] + cast_specs,
        out_shape=[jax.ShapeDtypeStruct((m, D_MODEL), F32)]
        + [jax.ShapeDtypeStruct(w.shape, BF16) for w in cast],
        scratch_shapes=[pltpu.VMEM((tm, D_MODEL), BF16)],
        compiler_params=pltpu.CompilerParams(
            dimension_semantics=("arbitrary", "arbitrary"), vmem_limit_bytes=VMEM_LIMIT),
        name="ffn_final" if final_norm else "ffn",
    )(x2d, nw, mod3, mod3, mod3, wg, wu, *([wd] * (D_MODEL // W_WINDOW_COLS)), fw, *cast)
    return res


def _pool_group(ext, u, t_abs, window, pw, ps):
    s, k = ext, 1
    while k < window:
        s = s + pltpu.roll(s, k, axis=0)
        k *= 2
    cnt = jnp.minimum(t_abs + 1, window).astype(F32)
    pooled = s[POOL_HALO:, :] * (1.0 / cnt) - u
    return jnp.dot(pooled.astype(BF16), pw, preferred_element_type=F32) * ps


def _win_kernel(x_ref, nw_ref, sh_ref, sc_ref, *rest, tiles_per_batch):
    n_win = IN_WIDTH // W_WINDOW_COLS
    w_refs = rest[:n_win]
    (lbl_ref, pw_ref, ps_ref, yp_ref, q_ref, k_ref, gl_ref, v_ref, gt_ref,
     tail_scr) = rest[n_win:]
    tm, tn = yp_ref.shape
    cg = POOL_GROUP_DIM
    lbl = lbl_ref[...]
    pe = jnp.exp(lbl - jnp.max(lbl, axis=0, keepdims=True))
    p = pe / jnp.sum(pe, axis=0, keepdims=True)
    lb = (p[0:1, :] + p[1:2, :]) - p[0:1, :]

    tile_in_seq = pl.program_id(0) % tiles_per_batch

    @pl.when(tile_in_seq == 0)
    def _():
        tail_scr[...] = jnp.zeros_like(tail_scr)

    tail = tail_scr[...]
    row = lax.broadcasted_iota(jnp.int32, (WIN_SUB_ROWS, 1), 0)

    for r in range(0, tm, WIN_SUB_ROWS):
        rows = pl.ds(r, WIN_SUB_ROWS)
        h = _norm_modulate(x_ref[rows, :], nw_ref[...], sh_ref[0], sc_ref[0]).astype(BF16)

        def project(seg):
            per_seg = tn // W_WINDOW_COLS
            return jnp.concatenate(
                [jnp.dot(h, w_refs[seg * per_seg + c][...], preferred_element_type=F32)
                 for c in range(per_seg)], axis=1)

        u = project(0)
        zq = project(1)
        ext = jnp.concatenate([tail, u], axis=0)
        tail = u[WIN_SUB_ROWS - POOL_HALO:, :]
        t_abs = tile_in_seq * tm + r + row
        for g, window in enumerate(POOL_WINDOWS):
            cols = slice(g * cg, (g + 1) * cg)
            y = _pool_group(ext[:, cols], u[:, cols], t_abs, window, pw_ref[g], ps_ref[:, cols])
            yp_ref[rows, cols] = y.astype(BF16)
        zf = project(2)
        q_ref[rows, :] = _silu(zq).astype(BF16)
        zv = project(3)
        forget = lb + (1.0 - lb) * jax.nn.sigmoid(zf)
        k_ref[rows, :] = (1.0 - forget).astype(BF16)
        gl_ref[rows, :] = jnp.log(forget)
        zg = project(4)
        v_ref[rows, :] = zv.astype(BF16)
        gt_ref[rows, :] = _silu(zg).astype(BF16)
    tail_scr[...] = tail


def _win(x2d, nw, mod3, mod_base, w_in, lb_logits, pool_w, pool_scale):
    m = x2d.shape[0]
    tm, tn = 512, 1024
    assert IN_WIDTH == 5 * tn and POOL_WIDTH == tn and HGRN_WIDTH == tn
    assert max(POOL_WINDOWS) <= POOL_HALO and tn == len(POOL_WINDOWS) * POOL_GROUP_DIM
    tiles_per_batch = SEQ // tm
    mod_rows = mod3.shape[0] // BATCH

    def mod_map(k):
        return lambda i: ((i // tiles_per_batch) * mod_rows + mod_base + k, 0, 0)

    seg = pl.BlockSpec((tm, tn), lambda i: (i, 0))
    return pl.pallas_call(
        functools.partial(_win_kernel, tiles_per_batch=tiles_per_batch),
        grid=(m // tm,),
        in_specs=[
            pl.BlockSpec((tm, D_MODEL), lambda i: (i, 0)),
            pl.BlockSpec((1, D_MODEL), lambda i: (0, 0)),
            pl.BlockSpec((1, 1, D_MODEL), mod_map(0)),
            pl.BlockSpec((1, 1, D_MODEL), mod_map(1)),
        ] + [
            pl.BlockSpec((D_MODEL, W_WINDOW_COLS), lambda i, s=s: (0, s),
                         pipeline_mode=pl.Buffered(1)) for s in range(IN_WIDTH // W_WINDOW_COLS)
        ] + [
            pl.BlockSpec((2, tn), lambda i: (0, 0)),
            pl.BlockSpec(pool_w.shape, lambda i: (0, 0, 0)),
            pl.BlockSpec((1, tn), lambda i: (0, 0)),
        ],
        out_specs=[seg] * 6,
        out_shape=[
            jax.ShapeDtypeStruct((m, tn), BF16),
            jax.ShapeDtypeStruct((m, tn), BF16),
            jax.ShapeDtypeStruct((m, tn), BF16),
            jax.ShapeDtypeStruct((m, tn), F32),
            jax.ShapeDtypeStruct((m, tn), BF16),
            jax.ShapeDtypeStruct((m, tn), BF16),
        ],
        scratch_shapes=[pltpu.VMEM((POOL_HALO, tn), F32)],
        compiler_params=pltpu.CompilerParams(
            dimension_semantics=("arbitrary",), vmem_limit_bytes=VMEM_LIMIT),
        name="mix_in",
    )(x2d, nw, mod3, mod3, *([w_in] * (IN_WIDTH // W_WINDOW_COLS)), lb_logits, pool_w, pool_scale)


def _dot_nt(a, b):
    return lax.dot_general(a, b, (((1,), (1,)), ((), ())), preferred_element_type=F32)


def _dot_tn(a, b):
    return lax.dot_general(a, b, (((0,), (0,)), ((), ())), preferred_element_type=F32)


LEVEL_SIZES = (32, 16, 8, 4, 2, 1)
SCAN_GROUP = 16
SCAN_HEADS = 4
LOG2E = 1.4426950408889634


def _row_bcast(v, r):
    return jnp.broadcast_to(v[r:r + 1, :], v.shape)


def _level_exponents(g):
    n = len(g)
    r8 = lax.broadcasted_iota(jnp.int32, (SUBLANES, 1), 0)
    p = list(g)
    for s in (1, 2, 4):
        p = [pi + jnp.where(r8 >= s, pltpu.roll(pi, s, axis=0), 0.0) for pi in p]
    off = [None, _row_bcast(p[0], SUBLANES - 1)]
    for i in range(1, n):
        off.append(off[i] + _row_bcast(p[i], SUBLANES - 1))
    b = [p[0]] + [p[i] + off[i] for i in range(1, n)]

    exps = []
    for hs in LEVEL_SIZES:
        if hs >= SUBLANES:
            w = hs // SUBLANES
            lvl = []
            for i in range(n):
                ref = off[(i // (2 * w)) * 2 * w + w]
                lvl.append(b[i] - ref if (i // w) % 2 == 1 else ref - b[i])
        elif hs == 1:
            lvl = [jnp.where(r8 % 2 == 1, gi, 0.0) for gi in g]
        else:
            lvl = []
            for pi in p:
                ref = _row_bcast(pi, hs - 1)
                for blk in range(1, SUBLANES // (2 * hs)):
                    ref = jnp.where(r8 >= blk * 2 * hs, _row_bcast(pi, blk * 2 * hs + hs - 1), ref)
                lvl.append(-jnp.abs(pi - ref))
        exps.append(lvl)
    return exps, b, off[n]


def _scan_kernel(q_ref, k_ref, gl_ref, v_ref, gt_ref, gw_ref, c_ref, wa_ref, ba_ref,
                 o_ref, mod_ref, st_scr, attn_scr, qd_scr, kd_scr, dec_scr):
    gw = gw_ref[...]
    nt = CHUNK // SUBLANES
    n_groups = SEQ // (CHUNK * SCAN_GROUP)
    n_levels = len(LEVEL_SIZES)
    ti = lax.broadcasted_iota(jnp.int32, (CHUNK, CHUNK), 0)
    si = lax.broadcasted_iota(jnp.int32, (CHUNK, CHUNK), 1)
    owner = jnp.where(ti == si, n_levels, -1)
    for idx, hs in enumerate(LEVEL_SIZES):
        owner = jnp.where(((ti // (2 * hs)) == (si // (2 * hs))) & ((ti // hs) > (si // hs)),
                          idx, owner)
    st_scr[...] = jnp.zeros_like(st_scr)

    def tiles(a):
        return [a[i * SUBLANES:(i + 1) * SUBLANES, :] for i in range(nt)]

    owner_t = tiles(owner)

    def scaled(a_tiles, e_tiles, keep=None):
        zero = jnp.zeros_like(a_tiles[0])
        return jnp.concatenate(
            [a * jnp.exp2(e) if keep is None or keep[i] else zero
             for i, (a, e) in enumerate(zip(a_tiles, e_tiles))], axis=0).astype(BF16)

    def rows_of(c):
        return pl.ds(pl.multiple_of(c * CHUNK, CHUNK), CHUNK)

    def cols_of(hd):
        return slice(hd * HEAD_DIM, (hd + 1) * HEAD_DIM)

    def scores(hd, c):
        rows, cols = rows_of(c), cols_of(hd)
        qb = q_ref[0, rows, cols]
        kb = k_ref[0, rows, cols]
        q = tiles(qb.astype(F32))
        k = tiles(kb.astype(F32))
        exps, b, b_last = _level_exponents(tiles(gl_ref[0, rows, cols] * LOG2E))

        attn = [jnp.where(o == n_levels, d, 0.0) for o, d in zip(owner_t, tiles(_dot_nt(qb, kb)))]
        for idx, (hs, e) in enumerate(zip(LEVEL_SIZES, exps)):
            if hs >= SUBLANES:
                second = [(i // (hs // SUBLANES)) % 2 == 1 for i in range(nt)]
                first = [not s for s in second]
            else:
                second, first = [True] * nt, None
            lvl = tiles(_dot_nt(scaled(q, e, second), scaled(k, e, first)))
            for i in range(nt):
                if second[i]:
                    attn[i] = jnp.where(owner_t[i] == idx, lvl[i], attn[i])
        attn_scr[c] = jnp.concatenate(attn, axis=0).astype(BF16)
        qd_scr[rows, :] = scaled(q, b)
        kd_scr[rows, :] = scaled(k, [b_last - bi for bi in b])
        dec_scr[c] = jnp.exp2(b_last)

    def step(out_of, scores_of, extra=None):
        prev, cur = [], []
        if out_of is not None:
            hp, gp = out_of
            prev = [gp * SCAN_GROUP + j for j in range(SCAN_GROUP)]
        if scores_of is not None:
            hc, gc = scores_of
            cur = [gc * SCAN_GROUP + j for j in range(SCAN_GROUP)]
        n = len(cur)
        cut = [0, n // 4, (3 * n) // 4, n]
        if prev:
            vbs = [v_ref[0, rows_of(c), cols_of(hp)] for c in prev]
            intra = [jnp.dot(attn_scr[c], vb, preferred_element_type=F32)
                     for c, vb in zip(prev, vbs)]
            kv = [_dot_tn(vb, kd_scr[rows_of(c), :]) for c, vb in zip(prev, vbs)]
        for c in cur[cut[0]:cut[1]]:
            scores(hc, c)
        if extra is not None:
            extra()
        if prev:
            st = st_scr[...]
            inter = []
            for j, c in enumerate(prev):
                inter.append(_dot_nt(qd_scr[rows_of(c), :], st.astype(BF16)))
                st = dec_scr[c][0:1, :] * st + kv[j]
            last_of_head = isinstance(gp, int) and gp == n_groups - 1
            st_scr[...] = jnp.zeros_like(st) if last_of_head else st
        for c in cur[cut[1]:cut[2]]:
            scores(hc, c)
        if prev:
            ys = []
            for j, c in enumerate(prev):
                o = intra[j] + inter[j]
                ms = jnp.mean(o * o, axis=-1, keepdims=True)
                gate = gt_ref[0, rows_of(c), cols_of(hp)].astype(F32)
                ys.append((o * lax.rsqrt(ms + EPS) * gw * gate).astype(BF16))
        for c in cur[cut[2]:cut[3]]:
            scores(hc, c)
        if prev:
            for y, c in zip(ys, prev):
                o_ref[0, rows_of(c), cols_of(hp)] = y

    def ada_block():
        mod_ref[...] = _ada_block(c_ref, wa_ref, ba_ref)

    n_heads = q_ref.shape[2] // HEAD_DIM
    step(None, (0, 0))
    for hd in range(n_heads):
        def body(g, carry, hd=hd):
            step((hd, g - 1), (hd, g))
            return carry

        lax.fori_loop(1, n_groups, body, 0)
        if hd + 1 < n_heads:
            step((hd, n_groups - 1), (hd + 1, 0))
        else:
            step((hd, n_groups - 1), None, extra=ada_block)


def _scan(q3, k3, gl3, v3, gt3, gnorm_w, c_pad, w_ada, b_ada, ada_start):
    blk = pl.BlockSpec((1, SEQ, SCAN_HEADS * HEAD_DIM), lambda b, h: (b, 0, h))
    hsteps = HGRN_HEADS // SCAN_HEADS
    n_ada = w_ada.shape[1] - ada_start
    assert n_ada == BATCH * hsteps * ADA_COLS and ada_start % ADA_COLS == 0

    def ada_map(b, h):
        return (0, ada_start // ADA_COLS + b * hsteps + h)

    return pl.pallas_call(
        _scan_kernel,
        grid=(BATCH, hsteps),
        in_specs=[blk, blk, blk, blk, blk, pl.BlockSpec((1, HEAD_DIM), lambda b, h: (0, 0)),
                  pl.BlockSpec((8, D_MODEL), lambda b, h: (0, 0)),
                  pl.BlockSpec((D_MODEL, ADA_COLS), ada_map),
                  pl.BlockSpec((1, ADA_COLS), ada_map)],
        out_specs=[blk, pl.BlockSpec((8, ADA_COLS), lambda b, h: (0, b * hsteps + h))],
        out_shape=[jax.ShapeDtypeStruct((BATCH, SEQ, HGRN_WIDTH), BF16),
                   jax.ShapeDtypeStruct((8, n_ada), F32)],
        scratch_shapes=[
            pltpu.VMEM((HEAD_DIM, HEAD_DIM), F32),
            pltpu.VMEM((SEQ // CHUNK, CHUNK, CHUNK), BF16),
            pltpu.VMEM((SEQ, HEAD_DIM), BF16),
            pltpu.VMEM((SEQ, HEAD_DIM), BF16),
            pltpu.VMEM((SEQ // CHUNK, SUBLANES, HEAD_DIM), F32),
        ],
        compiler_params=pltpu.CompilerParams(
            dimension_semantics=("arbitrary", "arbitrary"), vmem_limit_bytes=VMEM_LIMIT),
        name="hgrn_scan",
    )(q3, k3, gl3, v3, gt3, gnorm_w, c_pad, w_ada, b_ada)


def _wout_kernel(x_ref, yp_ref, yh_ref, *rest):
    nw = (len(rest) - 2) // 2
    wp_refs, wh_refs, (gt_ref, o_ref) = rest[:nw], rest[nw:2 * nw], rest[2 * nw:]
    for r in range(0, x_ref.shape[0], WOUT_SUB_ROWS):
        rows = pl.ds(r, WOUT_SUB_ROWS)
        yp, yh = yp_ref[rows, :], yh_ref[rows, :]
        mix = jnp.concatenate(
            [jnp.dot(yp, wp[...], preferred_element_type=F32)
             + jnp.dot(yh, wh[...], preferred_element_type=F32)
             for wp, wh in zip(wp_refs, wh_refs)], axis=1)
        o_ref[rows, :] = x_ref[rows, :] + gt_ref[0] * mix


def _wout(x2d, yp, yh, w_out, mod3, mod_base):
    m = x2d.shape[0]
    tm = 1024
    tiles_per_batch = SEQ // tm
    mod_rows = mod3.shape[0] // BATCH
    assert POOL_WIDTH == HGRN_WIDTH
    n_win = D_MODEL // W_WINDOW_COLS
    return pl.pallas_call(
        _wout_kernel,
        grid=(m // tm,),
        in_specs=[
            pl.BlockSpec((tm, D_MODEL), lambda i: (i, 0)),
            pl.BlockSpec((tm, POOL_WIDTH), lambda i: (i, 0)),
            pl.BlockSpec((tm, HGRN_WIDTH), lambda i: (i, 0)),
        ] + [
            pl.BlockSpec((POOL_WIDTH, W_WINDOW_COLS), lambda i, r=r, s=s: (r, s),
                         pipeline_mode=pl.Buffered(1))
            for r in range(2) for s in range(n_win)
        ] + [
            pl.BlockSpec((1, 1, D_MODEL),
                         lambda i: ((i // tiles_per_batch) * mod_rows + mod_base, 0, 0)),
        ],
        out_specs=pl.BlockSpec((tm, D_MODEL), lambda i: (i, 0)),
        out_shape=jax.ShapeDtypeStruct((m, D_MODEL), F32),
        compiler_params=pltpu.CompilerParams(
            dimension_semantics=("arbitrary",), vmem_limit_bytes=VMEM_LIMIT),
        name="mix_out",
    )(x2d, yp, yh, *([w_out] * (2 * n_win)), mod3)


def kernel(x, c, w_ada, b_ada, norm1_w, ffn1_gate, ffn1_up, ffn1_down, norm2_w, w_in, pool_w,
           pool_scale, lb_logits, gnorm_w, w_out, norm3_w, ffn2_gate, ffn2_up, ffn2_down,
           final_norm_w):
    B, S, D = x.shape
    assert (B, S, D) == (BATCH, SEQ, D_MODEL) and w_ada.shape[0] == 1
    assert lb_logits.shape == (2, HGRN_WIDTH)

    c_pad = jnp.pad(c, ((0, 8 - B), (0, 0)))
    mod_a = _ada(c_pad, w_ada[0], b_ada, MOD_EARLY * D)[:B]
    mod3a = mod_a.reshape(B * MOD_EARLY, 1, D)

    x2d = x.reshape(B * S, D)
    fw = final_norm_w.reshape(1, D)

    x1, w_in_b, w_out_b, g2_b, u2_b, d2_b = _ffn(
        x2d, norm1_w, mod3a, 0, ffn1_gate[0].astype(BF16), ffn1_up[0].astype(BF16),
        ffn1_down[0].astype(BF16), fw, final_norm=False,
        cast=(w_in[0], w_out[0], ffn2_gate[0], ffn2_up[0], ffn2_down[0]))

    y_pool, q, k, gl, v, gt = _win(x1, norm2_w, mod3a, 3, w_in_b, lb_logits,
                                   pool_w[0].astype(BF16), pool_scale)
    y_hgrn, mod_b = _scan(*(a.reshape(B, S, HGRN_WIDTH) for a in (q, k, gl, v, gt)), gnorm_w,
                          c_pad, w_ada[0], b_ada, MOD_EARLY * D)
    mod3b = mod_b[:B].reshape(B * (N_MOD - MOD_EARLY), 1, D)
    x2 = _wout(x1, y_pool, y_hgrn.reshape(B * S, HGRN_WIDTH), w_out_b, mod3b, 0)

    out, = _ffn(x2, norm3_w, mod3b, 1, g2_b, u2_b, d2_b, fw, final_norm=True)
    return out.reshape(B, S, D)
```

```python
import functools

import jax
import jax.numpy as jnp
from jax import lax
from jax.experimental import pallas as pl
from jax.experimental.pallas import tpu as pltpu

F32 = jnp.float32
BF16 = jnp.bfloat16

D_MODEL = 2048
BATCH = 4
SEQ = 2048
POOL_WIDTH = 1024
POOL_WINDOWS = (2, 4, 8, 16)
POOL_GROUP_DIM = 256
HGRN_WIDTH = 1024
HGRN_HEADS = 8
HEAD_DIM = 128
IN_WIDTH = 5120
CHUNK = 64
SUBLANES = 8
BF16_SUBLANES = 16
WIN_SUB_ROWS = 256
W_WINDOW_COLS = 512
FFN_SUB_ROWS = 512
FFN_MID_SUB_ROWS = 512
WOUT_SUB_ROWS = 512
POOL_HALO = 16
D_FF = 5632
N_MOD = 9
MOD_EARLY = 5
ADA_COLS = 1024
EPS = 1e-6

VMEM_LIMIT = 60 * 1024 * 1024


def _silu(v):
    return v * jax.nn.sigmoid(v)


def _silu_tanh(v):
    return v * (0.5 * jnp.tanh(0.5 * v) + 0.5)


def _norm_modulate(x, nw, sh, sc):
    ms = jnp.mean(x * x, axis=-1, keepdims=True)
    y = x * lax.rsqrt(ms + EPS) * nw
    return y * (1.0 + sc) + sh


def _ada_block(c_ref, w_ref, b_ref):
    ca = _silu(c_ref[...]).astype(BF16)
    return jnp.dot(ca, w_ref[...].astype(BF16), preferred_element_type=F32) + b_ref[...]


def _ada_kernel(c_ref, w_ref, b_ref, o_ref):
    o_ref[...] = _ada_block(c_ref, w_ref, b_ref)


def _ada(c_pad, w_ada, b_ada, n):
    tn = ADA_COLS
    return pl.pallas_call(
        _ada_kernel,
        grid=(n // tn,),
        in_specs=[
            pl.BlockSpec((8, D_MODEL), lambda j: (0, 0)),
            pl.BlockSpec((D_MODEL, tn), lambda j: (0, j)),
            pl.BlockSpec((1, tn), lambda j: (0, j)),
        ],
        out_specs=pl.BlockSpec((8, tn), lambda j: (0, j)),
        out_shape=jax.ShapeDtypeStruct((8, n), F32),
        compiler_params=pltpu.CompilerParams(
            dimension_semantics=("arbitrary",), vmem_limit_bytes=VMEM_LIMIT),
        name="ada_mod",
    )(c_pad, w_ada, b_ada)


def _ffn_kernel(*refs, nj, final_norm, n_cast):
    x_ref, nw_ref, sh_ref, sc_ref, gt_ref, wg_ref, wu_ref, wd_ref, fw_ref = refs[:9]
    cast_in = refs[9:9 + n_cast]
    o_ref = refs[9 + n_cast]
    cast_out = refs[10 + n_cast:10 + 2 * n_cast]
    h_scr, = refs[10 + 2 * n_cast:]
    j = pl.program_id(1)
    tm = x_ref.shape[0]
    subs = [pl.ds(r, FFN_SUB_ROWS) for r in range(0, tm, FFN_SUB_ROWS)]
    mid_rows = FFN_MID_SUB_ROWS if n_cast else tm
    mid_subs = [pl.ds(r, mid_rows) for r in range(0, tm, mid_rows)]

    for w_ref, wb_ref in zip(cast_in, cast_out):
        wb_ref[...] = w_ref[...].astype(BF16)

    def partial_ffn(rows):
        h = h_scr[rows, :]
        g = jnp.dot(h, wg_ref[...], preferred_element_type=F32)
        u = jnp.dot(h, wu_ref[...], preferred_element_type=F32)
        a = (_silu_tanh(g) * u).astype(BF16)
        return jnp.dot(a, wd_ref[...], preferred_element_type=F32)

    @pl.when(j == 0)
    def _():
        for rows in subs:
            h = _norm_modulate(x_ref[rows, :], nw_ref[...], sh_ref[0], sc_ref[0])
            h_scr[rows, :] = h.astype(BF16)
            o_ref[rows, :] = partial_ffn(rows)

    @pl.when((j > 0) & (j < nj - 1))
    def _():
        for rows in mid_subs:
            o_ref[rows, :] += partial_ffn(rows)

    @pl.when(j == nj - 1)
    def _():
        for rows in subs:
            out = x_ref[rows, :] + 0.5 * gt_ref[0] * (o_ref[rows, :] + partial_ffn(rows))
            if final_norm:
                ms = jnp.mean(out * out, axis=-1, keepdims=True)
                out = out * lax.rsqrt(ms + EPS) * fw_ref[...]
            o_ref[rows, :] = out


def _ffn(x2d, nw, mod3, mod_base, wg, wu, wd, fw, *, final_norm, cast=()):
    m = x2d.shape[0]
    tm, tf = 1024, 512
    ni, nj = m // tm, D_FF // tf
    tiles_per_batch = SEQ // tm
    mod_rows = mod3.shape[0] // BATCH

    def mod_map(k):
        return lambda i, j: ((i // tiles_per_batch) * mod_rows + mod_base + k, 0, 0)

    cast_specs = []
    for w in cast:
        rows = w.shape[0]
        nblk = max(n for n in range(1, ni * nj + 1) if rows % (BF16_SUBLANES * n) == 0)
        cast_specs.append(pl.BlockSpec(
            (rows // nblk, w.shape[1]),
            lambda i, j, nblk=nblk: (jnp.minimum(i * nj + j, nblk - 1), 0)))

    kern = functools.partial(_ffn_kernel, nj=nj, final_norm=final_norm, n_cast=len(cast))
    res = pl.pallas_call(
        kern,
        grid=(ni, nj),
        in_specs=[
            pl.BlockSpec((tm, D_MODEL), lambda i, j: (i, 0)),
            pl.BlockSpec((1, D_MODEL), lambda i, j: (0, 0)),
            pl.BlockSpec((1, 1, D_MODEL), mod_map(0)),
            pl.BlockSpec((1, 1, D_MODEL), mod_map(1)),
            pl.BlockSpec((1, 1, D_MODEL), mod_map(2)),
            pl.BlockSpec((D_MODEL, tf), lambda i, j: (0, j)),
            pl.BlockSpec((D_MODEL, tf), lambda i, j: (0, j)),
            pl.BlockSpec((tf, D_MODEL), lambda i, j: (j, 0)),
            pl.BlockSpec((1, D_MODEL), lambda i, j: (0, 0)),
        ] + cast_specs,
        out_specs=[pl.BlockSpec((tm, D_MODEL), lambda i, j: (i, 0))] + cast_specs,
        out_shape=[jax.ShapeDtypeStruct((m, D_MODEL), F32)]
        + [jax.ShapeDtypeStruct(w.shape, BF16) for w in cast],
        scratch_shapes=[pltpu.VMEM((tm, D_MODEL), BF16)],
        compiler_params=pltpu.CompilerParams(
            dimension_semantics=("arbitrary", "arbitrary"), vmem_limit_bytes=VMEM_LIMIT),
        name="ffn_final" if final_norm else "ffn",
    )(x2d, nw, mod3, mod3, mod3, wg, wu, wd, fw, *cast)
    return res


def _pool_group(ext, u, t_abs, window, pw, ps):
    s, k = ext, 1
    while k < window:
        s = s + pltpu.roll(s, k, axis=0)
        k *= 2
    cnt = jnp.minimum(t_abs + 1, window).astype(F32)
    pooled = s[POOL_HALO:, :] * (1.0 / cnt) - u
    return jnp.dot(pooled.astype(BF16), pw, preferred_element_type=F32) * ps


def _win_kernel(x_ref, nw_ref, sh_ref, sc_ref, *rest, tiles_per_batch):
    n_win = IN_WIDTH // W_WINDOW_COLS
    w_refs = rest[:n_win]
    (lbl_ref, pw_ref, ps_ref, yp_ref, q_ref, k_ref, gl_ref, v_ref, gt_ref,
     tail_scr) = rest[n_win:]
    tm, tn = yp_ref.shape
    cg = POOL_GROUP_DIM
    lbl = lbl_ref[...]
    pe = jnp.exp(lbl - jnp.max(lbl, axis=0, keepdims=True))
    p = pe / jnp.sum(pe, axis=0, keepdims=True)
    lb = (p[0:1, :] + p[1:2, :]) - p[0:1, :]

    tile_in_seq = pl.program_id(0) % tiles_per_batch

    @pl.when(tile_in_seq == 0)
    def _():
        tail_scr[...] = jnp.zeros_like(tail_scr)

    tail = tail_scr[...]
    row = lax.broadcasted_iota(jnp.int32, (WIN_SUB_ROWS, 1), 0)

    for r in range(0, tm, WIN_SUB_ROWS):
        rows = pl.ds(r, WIN_SUB_ROWS)
        h = _norm_modulate(x_ref[rows, :], nw_ref[...], sh_ref[0], sc_ref[0]).astype(BF16)

        def project(seg):
            per_seg = tn // W_WINDOW_COLS
            return jnp.concatenate(
                [jnp.dot(h, w_refs[seg * per_seg + c][...], preferred_element_type=F32)
                 for c in range(per_seg)], axis=1)

        u = project(0)
        zq = project(1)
        ext = jnp.concatenate([tail, u], axis=0)
        tail = u[WIN_SUB_ROWS - POOL_HALO:, :]
        t_abs = tile_in_seq * tm + r + row
        for g, window in enumerate(POOL_WINDOWS):
            cols = slice(g * cg, (g + 1) * cg)
            y = _pool_group(ext[:, cols], u[:, cols], t_abs, window, pw_ref[g], ps_ref[:, cols])
            yp_ref[rows, cols] = y.astype(BF16)
        zf = project(2)
        q_ref[rows, :] = _silu(zq).astype(BF16)
        zv = project(3)
        forget = lb + (1.0 - lb) * jax.nn.sigmoid(zf)
        k_ref[rows, :] = (1.0 - forget).astype(BF16)
        gl_ref[rows, :] = jnp.log(forget)
        zg = project(4)
        v_ref[rows, :] = zv.astype(BF16)
        gt_ref[rows, :] = _silu(zg).astype(BF16)
    tail_scr[...] = tail


def _win(x2d, nw, mod3, mod_base, w_in, lb_logits, pool_w, pool_scale):
    m = x2d.shape[0]
    tm, tn = 512, 1024
    assert IN_WIDTH == 5 * tn and POOL_WIDTH == tn and HGRN_WIDTH == tn
    assert max(POOL_WINDOWS) <= POOL_HALO and tn == len(POOL_WINDOWS) * POOL_GROUP_DIM
    tiles_per_batch = SEQ // tm
    mod_rows = mod3.shape[0] // BATCH

    def mod_map(k):
        return lambda i: ((i // tiles_per_batch) * mod_rows + mod_base + k, 0, 0)

    seg = pl.BlockSpec((tm, tn), lambda i: (i, 0))
    return pl.pallas_call(
        functools.partial(_win_kernel, tiles_per_batch=tiles_per_batch),
        grid=(m // tm,),
        in_specs=[
            pl.BlockSpec((tm, D_MODEL), lambda i: (i, 0)),
            pl.BlockSpec((1, D_MODEL), lambda i: (0, 0)),
            pl.BlockSpec((1, 1, D_MODEL), mod_map(0)),
            pl.BlockSpec((1, 1, D_MODEL), mod_map(1)),
        ] + [
            pl.BlockSpec((D_MODEL, W_WINDOW_COLS), lambda i, s=s: (0, s),
                         pipeline_mode=pl.Buffered(1)) for s in range(IN_WIDTH // W_WINDOW_COLS)
        ] + [
            pl.BlockSpec((2, tn), lambda i: (0, 0)),
            pl.BlockSpec(pool_w.shape, lambda i: (0, 0, 0)),
            pl.BlockSpec((1, tn), lambda i: (0, 0)),
        ],
        out_specs=[seg] * 6,
        out_shape=[
            jax.ShapeDtypeStruct((m, tn), BF16),
            jax.ShapeDtypeStruct((m, tn), BF16),
            jax.ShapeDtypeStruct((m, tn), BF16),
            jax.ShapeDtypeStruct((m, tn), F32),
            jax.ShapeDtypeStruct((m, tn), BF16),
            jax.ShapeDtypeStruct((m, tn), BF16),
        ],
        scratch_shapes=[pltpu.VMEM((POOL_HALO, tn), F32)],
        compiler_params=pltpu.CompilerParams(
            dimension_semantics=("arbitrary",), vmem_limit_bytes=VMEM_LIMIT),
        name="mix_in",
    )(x2d, nw, mod3, mod3, *([w_in] * (IN_WIDTH // W_WINDOW_COLS)), lb_logits, pool_w, pool_scale)


def _dot_nt(a, b):
    return lax.dot_general(a, b, (((1,), (1,)), ((), ())), preferred_element_type=F32)


def _dot_tn(a, b):
    return lax.dot_general(a, b, (((0,), (0,)), ((), ())), preferred_element_type=F32)


LEVEL_SIZES = (32, 16, 8, 4, 2, 1)
SCAN_GROUP = 16
SCAN_HEADS = 4
LOG2E = 1.4426950408889634


def _row_bcast(v, r):
    return jnp.broadcast_to(v[r:r + 1, :], v.shape)


def _level_exponents(g):
    n = len(g)
    r8 = lax.broadcasted_iota(jnp.int32, (SUBLANES, 1), 0)
    p = list(g)
    for s in (1, 2, 4):
        p = [pi + jnp.where(r8 >= s, pltpu.roll(pi, s, axis=0), 0.0) for pi in p]
    off = [None, _row_bcast(p[0], SUBLANES - 1)]
    for i in range(1, n):
        off.append(off[i] + _row_bcast(p[i], SUBLANES - 1))
    b = [p[0]] + [p[i] + off[i] for i in range(1, n)]

    exps = []
    for hs in LEVEL_SIZES:
        if hs >= SUBLANES:
            w = hs // SUBLANES
            lvl = []
            for i in range(n):
                ref = off[(i // (2 * w)) * 2 * w + w]
                lvl.append(b[i] - ref if (i // w) % 2 == 1 else ref - b[i])
        elif hs == 1:
            lvl = [jnp.where(r8 % 2 == 1, gi, 0.0) for gi in g]
        else:
            lvl = []
            for pi in p:
                ref = _row_bcast(pi, hs - 1)
                for blk in range(1, SUBLANES // (2 * hs)):
                    ref = jnp.where(r8 >= blk * 2 * hs, _row_bcast(pi, blk * 2 * hs + hs - 1), ref)
                lvl.append(-jnp.abs(pi - ref))
        exps.append(lvl)
    return exps, b, off[n]


def _scan_kernel(q_ref, k_ref, gl_ref, v_ref, gt_ref, gw_ref, c_ref, wa_ref, ba_ref,
                 o_ref, mod_ref, st_scr, attn_scr, qd_scr, kd_scr, dec_scr):
    gw = gw_ref[...]
    nt = CHUNK // SUBLANES
    n_groups = SEQ // (CHUNK * SCAN_GROUP)
    n_levels = len(LEVEL_SIZES)
    ti = lax.broadcasted_iota(jnp.int32, (CHUNK, CHUNK), 0)
    si = lax.broadcasted_iota(jnp.int32, (CHUNK, CHUNK), 1)
    owner = jnp.where(ti == si, n_levels, -1)
    for idx, hs in enumerate(LEVEL_SIZES):
        owner = jnp.where(((ti // (2 * hs)) == (si // (2 * hs))) & ((ti // hs) > (si // hs)),
                          idx, owner)
    st_scr[...] = jnp.zeros_like(st_scr)

    def tiles(a):
        return [a[i * SUBLANES:(i + 1) * SUBLANES, :] for i in range(nt)]

    owner_t = tiles(owner)

    def scaled(a_tiles, e_tiles, keep=None):
        zero = jnp.zeros_like(a_tiles[0])
        return jnp.concatenate(
            [a * jnp.exp2(e) if keep is None or keep[i] else zero
             for i, (a, e) in enumerate(zip(a_tiles, e_tiles))], axis=0).astype(BF16)

    def rows_of(c):
        return pl.ds(pl.multiple_of(c * CHUNK, CHUNK), CHUNK)

    def cols_of(hd):
        return slice(hd * HEAD_DIM, (hd + 1) * HEAD_DIM)

    def scores(hd, c):
        rows, cols = rows_of(c), cols_of(hd)
        qb = q_ref[0, rows, cols]
        kb = k_ref[0, rows, cols]
        q = tiles(qb.astype(F32))
        k = tiles(kb.astype(F32))
        exps, b, b_last = _level_exponents(tiles(gl_ref[0, rows, cols] * LOG2E))

        attn = [jnp.where(o == n_levels, d, 0.0) for o, d in zip(owner_t, tiles(_dot_nt(qb, kb)))]
        for idx, (hs, e) in enumerate(zip(LEVEL_SIZES, exps)):
            if hs >= SUBLANES:
                second = [(i // (hs // SUBLANES)) % 2 == 1 for i in range(nt)]
                first = [not s for s in second]
            else:
                second, first = [True] * nt, None
            lvl = tiles(_dot_nt(scaled(q, e, second), scaled(k, e, first)))
            for i in range(nt):
                if second[i]:
                    attn[i] = jnp.where(owner_t[i] == idx, lvl[i], attn[i])
        attn_scr[c] = jnp.concatenate(attn, axis=0).astype(BF16)
        qd_scr[rows, :] = scaled(q, b)
        kd_scr[rows, :] = scaled(k, [b_last - bi for bi in b])
        dec_scr[c] = jnp.exp2(b_last)

    def step(out_of, scores_of, extra=None):
        prev, cur = [], []
        if out_of is not None:
            hp, gp = out_of
            prev = [gp * SCAN_GROUP + j for j in range(SCAN_GROUP)]
        if scores_of is not None:
            hc, gc = scores_of
            cur = [gc * SCAN_GROUP + j for j in range(SCAN_GROUP)]
        n = len(cur)
        cut = [0, n // 4, (3 * n) // 4, n]
        if prev:
            vbs = [v_ref[0, rows_of(c), cols_of(hp)] for c in prev]
            intra = [jnp.dot(attn_scr[c], vb, preferred_element_type=F32)
                     for c, vb in zip(prev, vbs)]
            kv = [_dot_tn(vb, kd_scr[rows_of(c), :]) for c, vb in zip(prev, vbs)]
        for c in cur[cut[0]:cut[1]]:
            scores(hc, c)
        if extra is not None:
            extra()
        if prev:
            st = st_scr[...]
            inter = []
            for j, c in enumerate(prev):
                inter.append(_dot_nt(qd_scr[rows_of(c), :], st.astype(BF16)))
                st = dec_scr[c][0:1, :] * st + kv[j]
            last_of_head = isinstance(gp, int) and gp == n_groups - 1
            st_scr[...] = jnp.zeros_like(st) if last_of_head else st
        for c in cur[cut[1]:cut[2]]:
            scores(hc, c)
        if prev:
            ys = []
            for j, c in enumerate(prev):
                o = intra[j] + inter[j]
                ms = jnp.mean(o * o, axis=-1, keepdims=True)
                gate = gt_ref[0, rows_of(c), cols_of(hp)].astype(F32)
                ys.append((o * lax.rsqrt(ms + EPS) * gw * gate).astype(BF16))
        for c in cur[cut[2]:cut[3]]:
            scores(hc, c)
        if prev:
            for y, c in zip(ys, prev):
                o_ref[0, rows_of(c), cols_of(hp)] = y

    def ada_block():
        mod_ref[...] = _ada_block(c_ref, wa_ref, ba_ref)

    n_heads = q_ref.shape[2] // HEAD_DIM
    step(None, (0, 0))
    for hd in range(n_heads):
        def body(g, carry, hd=hd):
            step((hd, g - 1), (hd, g))
            return carry

        lax.fori_loop(1, n_groups, body, 0)
        if hd + 1 < n_heads:
            step((hd, n_groups - 1), (hd + 1, 0))
        else:
            step((hd, n_groups - 1), None, extra=ada_block)


def _scan(q3, k3, gl3, v3, gt3, gnorm_w, c_pad, w_ada, b_ada, ada_start):
    blk = pl.BlockSpec((1, SEQ, SCAN_HEADS * HEAD_DIM), lambda b, h: (b, 0, h))
    hsteps = HGRN_HEADS // SCAN_HEADS
    n_ada = w_ada.shape[1] - ada_start
    assert n_ada == BATCH * hsteps * ADA_COLS and ada_start % ADA_COLS == 0

    def ada_map(b, h):
        return (0, ada_start // ADA_COLS + b * hsteps + h)

    return pl.pallas_call(
        _scan_kernel,
        grid=(BATCH, hsteps),
        in_specs=[blk, blk, blk, blk, blk, pl.BlockSpec((1, HEAD_DIM), lambda b, h: (0, 0)),
                  pl.BlockSpec((8, D_MODEL), lambda b, h: (0, 0)),
                  pl.BlockSpec((D_MODEL, ADA_COLS), ada_map),
                  pl.BlockSpec((1, ADA_COLS), ada_map)],
        out_specs=[blk, pl.BlockSpec((8, ADA_COLS), lambda b, h: (0, b * hsteps + h))],
        out_shape=[jax.ShapeDtypeStruct((BATCH, SEQ, HGRN_WIDTH), BF16),
                   jax.ShapeDtypeStruct((8, n_ada), F32)],
        scratch_shapes=[
            pltpu.VMEM((HEAD_DIM, HEAD_DIM), F32),
            pltpu.VMEM((SEQ // CHUNK, CHUNK, CHUNK), BF16),
            pltpu.VMEM((SEQ, HEAD_DIM), BF16),
            pltpu.VMEM((SEQ, HEAD_DIM), BF16),
            pltpu.VMEM((SEQ // CHUNK, SUBLANES, HEAD_DIM), F32),
        ],
        compiler_params=pltpu.CompilerParams(
            dimension_semantics=("arbitrary", "arbitrary"), vmem_limit_bytes=VMEM_LIMIT),
        name="hgrn_scan",
    )(q3, k3, gl3, v3, gt3, gnorm_w, c_pad, w_ada, b_ada)


def _wout_kernel(x_ref, yp_ref, yh_ref, wp_ref, wh_ref, gt_ref, o_ref):
    for r in range(0, x_ref.shape[0], WOUT_SUB_ROWS):
        rows = pl.ds(r, WOUT_SUB_ROWS)
        mix = (jnp.dot(yp_ref[rows, :], wp_ref[...], preferred_element_type=F32)
               + jnp.dot(yh_ref[rows, :], wh_ref[...], preferred_element_type=F32))
        o_ref[rows, :] = x_ref[rows, :] + gt_ref[0] * mix


def _wout(x2d, yp, yh, w_out, mod3, mod_base):
    m = x2d.shape[0]
    tm = 1024
    tiles_per_batch = SEQ // tm
    mod_rows = mod3.shape[0] // BATCH
    return pl.pallas_call(
        _wout_kernel,
        grid=(m // tm,),
        in_specs=[
            pl.BlockSpec((tm, D_MODEL), lambda i: (i, 0)),
            pl.BlockSpec((tm, POOL_WIDTH), lambda i: (i, 0)),
            pl.BlockSpec((tm, HGRN_WIDTH), lambda i: (i, 0)),
            pl.BlockSpec((POOL_WIDTH, D_MODEL), lambda i: (0, 0), pipeline_mode=pl.Buffered(1)),
            pl.BlockSpec((HGRN_WIDTH, D_MODEL), lambda i: (1, 0), pipeline_mode=pl.Buffered(1)),
            pl.BlockSpec((1, 1, D_MODEL),
                         lambda i: ((i // tiles_per_batch) * mod_rows + mod_base, 0, 0)),
        ],
        out_specs=pl.BlockSpec((tm, D_MODEL), lambda i: (i, 0)),
        out_shape=jax.ShapeDtypeStruct((m, D_MODEL), F32),
        compiler_params=pltpu.CompilerParams(
            dimension_semantics=("arbitrary",), vmem_limit_bytes=VMEM_LIMIT),
        name="mix_out",
    )(x2d, yp, yh, w_out, w_out, mod3)


def kernel(x, c, w_ada, b_ada, norm1_w, ffn1_gate, ffn1_up, ffn1_down, norm2_w, w_in, pool_w,
           pool_scale, lb_logits, gnorm_w, w_out, norm3_w, ffn2_gate, ffn2_up, ffn2_down,
           final_norm_w):
    B, S, D = x.shape
    assert (B, S, D) == (BATCH, SEQ, D_MODEL) and w_ada.shape[0] == 1
    assert lb_logits.shape == (2, HGRN_WIDTH)

    c_pad = jnp.pad(c, ((0, 8 - B), (0, 0)))
    mod_a = _ada(c_pad, w_ada[0], b_ada, MOD_EARLY * D)[:B]
    mod3a = mod_a.reshape(B * MOD_EARLY, 1, D)

    x2d = x.reshape(B * S, D)
    fw = final_norm_w.reshape(1, D)

    x1, w_in_b, w_out_b, g2_b, u2_b, d2_b = _ffn(
        x2d, norm1_w, mod3a, 0, ffn1_gate[0].astype(BF16), ffn1_up[0].astype(BF16),
        ffn1_down[0].astype(BF16), fw, final_norm=False,
        cast=(w_in[0], w_out[0], ffn2_gate[0], ffn2_up[0], ffn2_down[0]))

    y_pool, q, k, gl, v, gt = _win(x1, norm2_w, mod3a, 3, w_in_b, lb_logits,
                                   pool_w[0].astype(BF16), pool_scale)
    y_hgrn, mod_b = _scan(*(a.reshape(B, S, HGRN_WIDTH) for a in (q, k, gl, v, gt)), gnorm_w,
                          c_pad, w_ada[0], b_ada, MOD_EARLY * D)
    mod3b = mod_b[:B].reshape(B * (N_MOD - MOD_EARLY), 1, D)
    x2 = _wout(x1, y_pool, y_hgrn.reshape(B * S, HGRN_WIDTH), w_out_b, mod3b, 0)

    out, = _ffn(x2, norm3_w, mod3b, 1, g2_b, u2_b, d2_b, fw, final_norm=True)
    return out.reshape(B, S, D)
```
